```python
import math
import jax, jax.numpy as jnp
from jax import lax
import numpy as np

D_MODEL = 1024
BATCH = 1
SEQ = 16384
DEPTH = 4

N_META = 16
RMS_EPS = 1e-6
D_FF = 4 * D_MODEL
N_EVEN = (DEPTH + 1) // 2
N_ODD = DEPTH // 2
RWKV_HEAD = 64
RWKV_WIDTH = D_MODEL // 2
RWKV_HEADS = RWKV_WIDTH // RWKV_HEAD
DECAY_RANK = 64
ICLR_RANK = 64
GATE_RANK = 128
GN_EPS = RWKV_HEAD * 1e-5
POOL_WIDTH = D_MODEL - RWKV_WIDTH
POOL_WINDOWS = (2, 4, 8, 16)
POOL_GROUPS = len(POOL_WINDOWS)
POOL_GROUP_W = POOL_WIDTH // POOL_GROUPS
SHIFT_WIDTH = 3 * RWKV_WIDTH + DECAY_RANK + ICLR_RANK + GATE_RANK
EVEN_IN = SHIFT_WIDTH + POOL_WIDTH
RWKV_SPLITS = (RWKV_WIDTH, 2 * RWKV_WIDTH, 3 * RWKV_WIDTH,
               3 * RWKV_WIDTH + DECAY_RANK, 3 * RWKV_WIDTH + DECAY_RANK + ICLR_RANK)
DIFF_HEADS = 8
DIFF_HEAD = D_MODEL // (2 * DIFF_HEADS)
DIFF_IN = 3 * D_MODEL
SUBLN_EPS = 1e-5
ROPE_THETA = 10000.0
Q_BLOCK = 128

kernel_name = 'hybrid_rwkv7_pool_diffattn'


def rmsnorm(t, g, eps=RMS_EPS):
    tf = t.astype(jnp.float32)
    tf = tf * lax.rsqrt(jnp.mean(tf * tf, axis=-1, keepdims=True) + eps)
    return (tf * g.astype(jnp.float32)).astype(t.dtype)


def rope_tables(length):
    pos = jnp.arange(length, dtype=jnp.float32)
    inv = ROPE_THETA ** (-jnp.arange(0, DIFF_HEAD, 2, dtype=jnp.float32) / DIFF_HEAD)
    ang = pos[:, None] * inv[None, :]
    ang = jnp.concatenate([ang, ang], axis=-1)
    return jnp.cos(ang), jnp.sin(ang)


def apply_rope(t, cos, sin):
    tf = t.astype(jnp.float32)
    t1, t2 = jnp.split(tf, 2, axis=-1)
    rot = jnp.concatenate([-t2, t1], axis=-1)
    c = cos[None, :, None, None, :]
    s = sin[None, :, None, None, :]
    return (tf * c + rot * s).astype(t.dtype)


def rwkv7_scan(r, decay, k, v, kk, a):
    B, L, H, N = r.shape

    def step(S, inp):
        r_t, w_t, k_t, v_t, kk_t, a_t = inp
        sa = jnp.einsum('bhvk,bhk->bhv', S, -kk_t)
        S = (S * w_t[:, :, None, :]
             + sa[..., None] * (kk_t * a_t)[:, :, None, :]
             + v_t[..., None] * k_t[:, :, None, :])
        o = jnp.einsum('bhvk,bhk->bhv', S, r_t)
        return S, o

    xs = tuple(jnp.moveaxis(t, 1, 0) for t in (r, decay, k, v, kk, a))
    S0 = jnp.zeros((B, H, N, N), jnp.float32)
    _, o = lax.scan(step, S0, xs)
    return jnp.moveaxis(o, 0, 1)


def rwkv_pool_mixer(h, w_in, mu, w0, w_up, a0, a_up, g_up, k_k, k_a, r_k,
                    ln_w, ln_b, pool_w, pool_scale, w_out):
    B, L, _ = h.shape
    f32 = jnp.float32
    y = h @ w_in
    ys = y[..., :SHIFT_WIDTH]
    prev = jnp.pad(ys, ((0, 0), (1, 0), (0, 0)))[:, :L]
    ys = ys + (prev - ys) * mu
    r, k, v, wd, ad, gd = jnp.split(ys, RWKV_SPLITS, axis=-1)
    wlog = -jax.nn.softplus(-(w0 + jnp.tanh(wd) @ w_up).astype(f32)) - 0.5
    decay = jnp.exp(-jnp.exp(wlog))
    a = jax.nn.sigmoid((a0 + ad @ a_up).astype(f32))
    g = (jax.nn.sigmoid(gd) @ g_up).astype(f32)
    hs = lambda t: t.reshape(B, L, RWKV_HEADS, RWKV_HEAD)
    kf = k.astype(f32)
    kk = hs(kf * k_k)
    kk = kk / jnp.maximum(jnp.linalg.norm(kk, axis=-1, keepdims=True), 1e-12)
    kf = kf * (1.0 + (a - 1.0) * k_a)
    rf, kf, vf, a_h, dec = hs(r.astype(f32)), hs(kf), hs(v.astype(f32)), hs(a), hs(decay)
    o = rwkv7_scan(rf, dec, kf, vf, kk, a_h)
    mean = jnp.mean(o, axis=-1, keepdims=True)
    var = jnp.mean(jnp.square(o - mean), axis=-1, keepdims=True)
    o = ((o - mean) * lax.rsqrt(var + GN_EPS)).reshape(B, L, RWKV_WIDTH) * ln_w + ln_b
    bonus = jnp.sum(rf * kf * r_k, axis=-1, keepdims=True) * vf
    o = ((o + bonus.reshape(B, L, RWKV_WIDTH)) * g).astype(h.dtype)
    u = y[..., SHIFT_WIDTH:].reshape(B, L, POOL_GROUPS, POOL_GROUP_W).astype(f32)
    t_idx = jnp.arange(L)
    diffs = []
    for gi, win in enumerate(POOL_WINDOWS):
        ug = u[:, :, gi]
        c = jnp.cumsum(ug, axis=1)
        lag = jnp.pad(c, ((0, 0), (win, 0), (0, 0)))[:, :L]
        cnt = jnp.minimum(t_idx + 1, win).astype(f32)[None, :, None]
        diffs.append((c - lag) / cnt - ug)
    d = jnp.stack(diffs, axis=2)
    z = jnp.einsum('blgc,gcd->blgd', d, pool_w.astype(f32)).reshape(B, L, POOL_WIDTH)
    z = (z * pool_scale).astype(h.dtype)
    return jnp.concatenate([o, z], axis=-1) @ w_out


def causal_diff_attention(q, k, v, lam):
    B, L, H, _, dh = q.shape
    nb = -(-L // Q_BLOCK)
    Lp = nb * Q_BLOCK
    pad = Lp - L
    q = jnp.pad(q, ((0, 0), (0, pad), (0, 0), (0, 0), (0, 0)))
    k = jnp.pad(k, ((0, 0), (0, pad), (0, 0), (0, 0), (0, 0)))
    v = jnp.pad(v, ((0, 0), (0, pad), (0, 0), (0, 0)))
    qb = jnp.moveaxis(q.reshape(B, nb, Q_BLOCK, H, 2, dh), 1, 0)
    kpos = jnp.arange(Lp)
    scale = DIFF_HEAD ** -0.5

    def one_block(args):
        qblk, start = args
        s = jnp.einsum('bqhcd,bkhcd->bhcqk', qblk, k).astype(jnp.float32) * scale
        qpos = start + jnp.arange(Q_BLOCK)
        mask = kpos[None, :] <= qpos[:, None]
        s = jnp.where(mask, s, -jnp.inf)
        p = jax.nn.softmax(s, axis=-1)
        attn = p[:, :, 0] - lam * p[:, :, 1]
        return jnp.einsum('bhqk,bkhe->bqhe', attn.astype(v.dtype), v)

    out = lax.map(one_block, (qb, jnp.arange(nb) * Q_BLOCK))
    out = jnp.moveaxis(out, 0, 1).reshape(B, Lp, H, 2 * dh)
    return out[:, :L]


def diff_attn_mixer(h, w_in, lam_vecs, subln_w, w_out, cos, sin, layer):
    B, L, _ = h.shape
    y = h @ w_in
    q, k, v = jnp.split(y, 3, axis=-1)
    q = apply_rope(q.reshape(B, L, DIFF_HEADS, 2, DIFF_HEAD), cos, sin)
    k = apply_rope(k.reshape(B, L, DIFF_HEADS, 2, DIFF_HEAD), cos, sin)
    v = v.reshape(B, L, DIFF_HEADS, 2 * DIFF_HEAD)
    lam_init = 0.8 - 0.6 * math.exp(-0.3 * layer)
    lv = lam_vecs.astype(jnp.float32)
    lam = jnp.exp(jnp.sum(lv[0] * lv[1])) - jnp.exp(jnp.sum(lv[2] * lv[3])) + lam_init
    o = causal_diff_attention(q, k, v, lam)
    o = rmsnorm(o, subln_w, SUBLN_EPS) * (1.0 - lam_init)
    return o.reshape(B, L, D_MODEL).astype(h.dtype) @ w_out


def sq_relu_mlp(h, w1, w2):
    return jnp.square(jax.nn.relu(h @ w1)) @ w2


def setup_inputs(seed: int = 0) -> dict:
    key = jax.random.key(seed)
    ks = jax.random.split(key, 24)
    nrm = lambda i, shape, s: jax.random.normal(ks[i], shape, jnp.float32) * s
    RW = RWKV_WIDTH
    return {
        'x': nrm(0, (BATCH, SEQ, D_MODEL), 1.0),
        'meta': nrm(1, (N_META, D_MODEL), 1.0),
        'norm_g': 1.0 + nrm(2, (DEPTH, 4, D_MODEL), 0.1),
        'mlp_w1': nrm(3, (DEPTH, D_MODEL, D_FF), D_MODEL ** -0.5),
        'mlp_w2': nrm(4, (DEPTH, D_FF, D_MODEL), D_FF ** -0.5),
        'ev_w_in': nrm(5, (N_EVEN, D_MODEL, EVEN_IN), D_MODEL ** -0.5),
        'ev_mu': jax.random.uniform(ks[6], (N_EVEN, SHIFT_WIDTH), jnp.float32),
        'ev_w0': jax.random.uniform(ks[7], (N_EVEN, RW), jnp.float32, -4.0, 1.0),
        'ev_w_up': nrm(8, (N_EVEN, DECAY_RANK, RW), 0.5 * DECAY_RANK ** -0.5),
        'ev_a0': nrm(9, (N_EVEN, RW), 0.1),
        'ev_a_up': nrm(10, (N_EVEN, ICLR_RANK, RW), ICLR_RANK ** -0.5),
        'ev_g_up': nrm(11, (N_EVEN, GATE_RANK, RW), GATE_RANK ** -0.5),
        'ev_k_k': 0.85 + nrm(12, (N_EVEN, RW), 0.05),
        'ev_k_a': 1.0 + nrm(13, (N_EVEN, RW), 0.05),
        'ev_r_k': nrm(14, (N_EVEN, RWKV_HEADS, RWKV_HEAD), 0.1),
        'ev_ln_w': 1.0 + nrm(15, (N_EVEN, RW), 0.1),
        'ev_ln_b': nrm(16, (N_EVEN, RW), 0.02),
        'ev_pool_w': nrm(17, (N_EVEN, POOL_GROUPS, POOL_GROUP_W, POOL_GROUP_W), POOL_GROUP_W ** -0.5),
        'ev_pool_scale': 1.0 + nrm(18, (N_EVEN, POOL_WIDTH), 0.1),
        'ev_w_out': nrm(19, (N_EVEN, D_MODEL, D_MODEL), D_MODEL ** -0.5),
        'od_w_in': nrm(20, (N_ODD, D_MODEL, DIFF_IN), D_MODEL ** -0.5),
        'od_lambda': nrm(21, (N_ODD, 4, DIFF_HEAD), 0.1),
        'od_subln_w': 1.0 + nrm(22, (N_ODD, 2 * DIFF_HEAD), 0.1),
        'od_w_out': nrm(23, (N_ODD, D_MODEL, D_MODEL), D_MODEL ** -0.5),
    }


def reference(x, meta, norm_g, mlp_w1, mlp_w2, ev_w_in, ev_mu, ev_w0, ev_w_up, ev_a0,
              ev_a_up, ev_g_up, ev_k_k, ev_k_a, ev_r_k, ev_ln_w, ev_ln_b, ev_pool_w,
              ev_pool_scale, ev_w_out, od_w_in, od_lambda, od_subln_w, od_w_out):
    B = x.shape[0]
    h = jnp.concatenate([jnp.broadcast_to(meta[None].astype(x.dtype), (B, N_META, D_MODEL)), x], axis=1)
    L = h.shape[1]
    cos, sin = rope_tables(L)
    for i in range(DEPTH):
        g = norm_g[i]
        j = i // 2
        hn = rmsnorm(h, g[0])
        if i % 2 == 0:
            m = rwkv_pool_mixer(hn, ev_w_in[j], ev_mu[j], ev_w0[j], ev_w_up[j], ev_a0[j],
                                ev_a_up[j], ev_g_up[j], ev_k_k[j], ev_k_a[j], ev_r_k[j],
                                ev_ln_w[j], ev_ln_b[j], ev_pool_w[j], ev_pool_scale[j], ev_w_out[j])
        else:
            m = diff_attn_mixer(hn, od_w_in[j], od_lambda[j], od_subln_w[j], od_w_out[j], cos, sin, i)
        h = h + rmsnorm(m, g[1])
        f = sq_relu_mlp(rmsnorm(h, g[2]), mlp_w1[i], mlp_w2[i])
        h = h + rmsnorm(f, g[3])
    return h[:, N_META:]
```

```python
import functools
import math

import jax
import jax.numpy as jnp
from jax import lax
from jax.experimental import pallas as pl
from jax.experimental.pallas import tpu as pltpu

F32, BF16 = jnp.float32, jnp.bfloat16

D_MODEL = 1024
N_META = 16
RMS_EPS = 1e-6
D_FF = 4 * D_MODEL
RWKV_HEAD = 64
RWKV_WIDTH = D_MODEL // 2
DECAY_RANK = 64
ICLR_RANK = 64
GATE_RANK = 128
GN_EPS = RWKV_HEAD * 1e-5
POOL_WIDTH = D_MODEL - RWKV_WIDTH
POOL_WINDOWS = (2, 4, 8, 16)
POOL_GROUP_W = POOL_WIDTH // len(POOL_WINDOWS)
POOL_CARRY = 16
SHIFT_WIDTH = 3 * RWKV_WIDTH + DECAY_RANK + ICLR_RANK + GATE_RANK
EVEN_IN = SHIFT_WIDTH + POOL_WIDTH
DIFF_HEADS = 8
DIFF_HEAD = D_MODEL // (2 * DIFF_HEADS)
SUBLN_EPS = 1e-5
ROPE_THETA = 10000.0

LANES = 128
HALF = LANES // 2
ROW_ALIGN = 256
CHUNK = 64
ATT_BLOCK = 256
FF_CHUNK = 1024
NEG_BIG = -1e30
VMEM_LIMIT = 56 * 1024 * 1024


def _pick_tile(n, candidates):
    for c in candidates:
        if n % c == 0:
            return c
    raise ValueError(f"no tile in {candidates} divides {n}")


def _rms(t, g, eps):
    return t * lax.rsqrt(jnp.mean(t * t, axis=-1, keepdims=True) + eps) * g


def _split2(x):
    hi = x.astype(BF16)
    lo = (x - hi.astype(F32)).astype(BF16)
    return hi, lo


def _dot(a, b):
    return jnp.dot(a.astype(BF16), b.astype(BF16), preferred_element_type=F32)


def _dot3(a, b):
    ah, al = _split2(a)
    bh, bl = _split2(b)
    d = functools.partial(jnp.dot, preferred_element_type=F32)
    return d(ah, bh) + (d(ah, bl) + d(al, bh))


def _dot_nt3(a, b):
    ah, al = _split2(a)
    bh, bl = _split2(b)
    d = functools.partial(lax.dot_general, dimension_numbers=(((1,), (1,)), ((), ())),
                          preferred_element_type=F32)
    return d(ah, bh) + (d(ah, bl) + d(al, bh))


def _dot_exact_lhs(a_bf16, b):
    bh, bl = _split2(b)
    d = functools.partial(jnp.dot, preferred_element_type=F32)
    return d(a_bf16, bh) + d(a_bf16, bl)


def _head_sum(x, seg_ref):
    return _dot_exact_lhs_rhs(x, seg_ref[...])


def _dot_exact_lhs_rhs(x, ones_bf16):
    xh, xl = _split2(x)
    d = functools.partial(jnp.dot, preferred_element_type=F32)
    return d(xh, ones_bf16) + d(xl, ones_bf16)


def _sigmoid(x):
    return 1.0 / (1.0 + jnp.exp(-x))


def _softplus(x):
    return jnp.maximum(x, 0.0) + jnp.log(1.0 + jnp.exp(-jnp.abs(x)))


def _mlp_residual(hm, g2, g3, w1_ref, w2_ref):
    n = _rms(hm, g2, RMS_EPS).astype(BF16)
    acc = jnp.zeros(hm.shape, F32)
    for c in range(D_FF // FF_CHUNK):
        cols = slice(c * FF_CHUNK, (c + 1) * FF_CHUNK)
        a = jnp.dot(n, w1_ref[:, cols], preferred_element_type=F32)
        a = jnp.square(jnp.maximum(a, 0.0)).astype(BF16)
        acc = acc + jnp.dot(a, w2_ref[cols, :], preferred_element_type=F32)
    return hm + _rms(acc, g3, RMS_EPS)


def _even_in_kernel(h_ref, g0_ref, win_ref, mu_ref, w0_ref, lora_ref, a0_ref, gup_ref,
                    kk_ref, ka_ref, seg_ref, poolw_ref, pscale_ref,
                    r_out, k_out, v_out, na_out, b_out, lw_out, g_out, z_out,
                    ycarry, ucarry, *, tm):
    i = pl.program_id(0)

    @pl.when(i == 0)
    def _():
        ycarry[...] = jnp.zeros(ycarry.shape, F32)
        ucarry[...] = jnp.zeros(ucarry.shape, F32)

    hn = _rms(h_ref[...], g0_ref[...], RMS_EPS).astype(BF16)
    y = jnp.dot(hn, win_ref[...], preferred_element_type=F32)

    ysh = y[:, :SHIFT_WIDTH]
    row = lax.broadcasted_iota(jnp.int32, (tm, 1), 0)
    prev = jnp.where(row == 0, ycarry[7:8, :], pltpu.roll(ysh, 1, axis=0))
    ycarry[...] = ysh[tm - 8:, :]
    ys = ysh + (prev - ysh) * mu_ref[...]

    rw = RWKV_WIDTH
    r = ys[:, 0:rw]
    k = ys[:, rw:2 * rw]
    v = ys[:, 2 * rw:3 * rw]
    wa = ys[:, 3 * rw:3 * rw + LANES]
    gd = ys[:, 3 * rw + LANES:SHIFT_WIDTH]

    lane = lax.broadcasted_iota(jnp.int32, (1, LANES), 1)
    lora_in = jnp.where(lane < DECAY_RANK, jnp.tanh(wa), wa)
    lora = _dot(lora_in, lora_ref[...])
    wlog = -_softplus(-(w0_ref[...] + lora[:, :rw])) - 0.5
    logw = -jnp.exp(wlog)
    a = _sigmoid(a0_ref[...] + lora[:, rw:])
    g = _dot(_sigmoid(gd), gup_ref[...])

    kk = k * kk_ref[...]
    kk = kk * lax.rsqrt(jnp.maximum(_head_sum(kk * kk, seg_ref), 1e-24))
    k2 = k * (1.0 + (a - 1.0) * ka_ref[...])

    r_out[...] = r
    k_out[...] = k2
    v_out[...] = v
    na_out[...] = -kk
    b_out[...] = kk * a
    lw_out[...] = logw
    g_out[...] = g

    u = y[:, SHIFT_WIDTH:]
    ext = jnp.concatenate([ucarry[...], u], axis=0)
    ucarry[...] = u[tm - POOL_CARRY:, :]
    t_idx = i * tm + row
    for gi, win in enumerate(POOL_WINDOWS):
        cols = slice(gi * POOL_GROUP_W, (gi + 1) * POOL_GROUP_W)
        s = ext[:, cols]
        span = 1
        while span < win:
            s = s + pltpu.roll(s, span, axis=0)
            span *= 2
        cnt = jnp.minimum(t_idx + 1, win).astype(F32)
        d = s[POOL_CARRY:, :] / cnt - u[:, cols]
        z_out[:, cols] = _dot(d, poolw_ref[gi]) * pscale_ref[:, cols]


def _even_in(h, g0, w_in, mu, w0, lora_w, a0, g_up, k_k, k_a, seg, pool_w, pool_scale):
    lp = h.shape[0]
    tm = _pick_tile(lp, (256,))
    rw = RWKV_WIDTH
    row_spec = lambda width: pl.BlockSpec((tm, width), lambda i: (i, 0))
    full = lambda arr: pl.BlockSpec(arr.shape, lambda i: (0,) * arr.ndim)
    out_sds = jax.ShapeDtypeStruct((lp, rw), F32)
    return pl.pallas_call(
        functools.partial(_even_in_kernel, tm=tm),
        grid=(lp // tm,),
        in_specs=[row_spec(D_MODEL), full(g0), full(w_in), full(mu), full(w0), full(lora_w), full(a0),
                  full(g_up), full(k_k), full(k_a), full(seg), full(pool_w), full(pool_scale)],
        out_specs=[row_spec(rw)] * 8,
        out_shape=[out_sds] * 8,
        scratch_shapes=[pltpu.VMEM((8, SHIFT_WIDTH), F32), pltpu.VMEM((POOL_CARRY, POOL_WIDTH), F32)],
        compiler_params=pltpu.CompilerParams(dimension_semantics=("arbitrary",),
                                             vmem_limit_bytes=VMEM_LIMIT),
        name="even_in",
    )(h, g0, w_in, mu, w0, lora_w, a0, g_up, k_k, k_a, seg, pool_w, pool_scale)


def _scan_kernel(r_ref, k_ref, v_ref, na_ref, b_ref, lw_ref, o_ref, h_scr):
    @pl.when(pl.program_id(0) == 0)
    def _():
        h_scr[...] = jnp.zeros(h_scr.shape, F32)

    c = CHUNK
    row = lax.broadcasted_iota(jnp.int32, (c, c), 0)
    col = lax.broadcasted_iota(jnp.int32, (c, c), 1)
    strict = col < row
    incl = col <= row
    tri = jnp.where(incl, 1.0, 0.0).astype(BF16)

    lw = lw_ref[...]
    lw_hi = lw.astype(BF16)
    lw_r = lw - lw_hi.astype(F32)
    lw_mid = lw_r.astype(BF16)
    lw_lo = (lw_r - lw_mid.astype(F32)).astype(BF16)
    d = functools.partial(jnp.dot, preferred_element_type=F32)
    cum = d(tri, lw_hi) + (d(tri, lw_mid) + d(tri, lw_lo))

    cum_end = cum[c - 1:c, :]
    e_pos = jnp.exp(cum)
    e_neg = jnp.exp(-cum)
    e_prev = jnp.exp(cum - lw)
    e_end = jnp.exp(cum_end - cum)
    p_end = jnp.exp(cum_end)

    r_t = r_ref[...] * e_pos
    a_t = na_ref[...] * e_prev
    b_all = b_ref[...]
    k_all = k_ref[...]
    b_t = b_all * e_neg
    k_t = k_all * e_neg
    b_h = b_all * e_end
    k_h = k_all * e_end
    v_all = v_ref[...]

    lane = lax.broadcasted_iota(jnp.int32, (1, LANES), 1)
    mlo = lane < HALF
    prow = lax.broadcasted_iota(jnp.int32, (LANES, LANES), 0)
    pcol = lax.broadcasted_iota(jnp.int32, (LANES, LANES), 1)
    same_head = (prow < HALF) == (pcol < HALF)
    diag = prow == pcol

    for p in range(RWKV_WIDTH // LANES):
        cols = slice(p * LANES, (p + 1) * LANES)
        rt, at, bt, kt, bh, kh, vp = (t[:, cols] for t in (r_t, a_t, b_t, k_t, b_h, k_h, v_all))
        zero = jnp.zeros_like(at)
        lhs4 = jnp.concatenate([jnp.where(mlo, at, zero), jnp.where(mlo, zero, at),
                                jnp.where(mlo, rt, zero), jnp.where(mlo, zero, rt)], axis=0)
        a_b = _dot_nt3(lhs4, bt)
        a_k = _dot_nt3(lhs4, kt)

        w_parts, u0_parts, q_parts, o_parts = [], [], [], []
        for hh in range(2):
            n_mat = jnp.where(strict, a_b[hh * c:(hh + 1) * c], 0.0)
            a_ak = jnp.where(strict, a_k[hh * c:(hh + 1) * c], 0.0)
            a_rb = jnp.where(incl, a_b[(2 + hh) * c:(3 + hh) * c], 0.0)
            a_rk = jnp.where(incl, a_k[(2 + hh) * c:(3 + hh) * c], 0.0)
            x = jnp.concatenate([at, _dot3(a_ak, vp)], axis=1)
            m = n_mat
            levels = int(math.log2(c))
            for lev in range(levels):
                x = x + _dot3(m, x)
                if lev < levels - 1:
                    m = _dot3(m, m)
            w_h = x[:, :LANES]
            u0_h = x[:, LANES:]
            w_parts.append(w_h)
            u0_parts.append(u0_h)
            q_parts.append(_dot3(a_rb, w_h))
            o_parts.append(_dot3(jnp.concatenate([a_rb, a_rk], axis=1),
                                 jnp.concatenate([u0_h, vp], axis=0)))
        w_p = jnp.where(mlo, w_parts[0], w_parts[1])
        u0_p = jnp.where(mlo, u0_parts[0], u0_parts[1])
        q_hat = rt + jnp.where(mlo, q_parts[0], q_parts[1])
        o_hat = jnp.where(mlo, o_parts[0], o_parts[1])

        bh_t = bh.T
        kh_t = kh.T
        g_mat = jnp.where(same_head, _dot3(bh_t, w_p), 0.0) + jnp.where(diag, p_end[:, cols], 0.0)
        j_mat = jnp.where(same_head, _dot3(jnp.concatenate([bh_t, kh_t], axis=1),
                                           jnp.concatenate([u0_p, vp], axis=0)), 0.0)
        h0 = h_scr[p]
        o_ref[:, cols] = _dot3(q_hat, h0) + o_hat
        h_scr[p] = _dot3(g_mat, h0) + j_mat


def _rwkv_scan(r, k2, v, na, b, logw):
    lp, rw = r.shape
    spec = pl.BlockSpec((CHUNK, rw), lambda i: (i, 0))
    return pl.pallas_call(
        _scan_kernel,
        grid=(lp // CHUNK,),
        in_specs=[spec] * 6,
        out_specs=spec,
        out_shape=jax.ShapeDtypeStruct((lp, rw), F32),
        scratch_shapes=[pltpu.VMEM((rw // LANES, LANES, LANES), F32)],
        compiler_params=pltpu.CompilerParams(dimension_semantics=("arbitrary",),
                                             vmem_limit_bytes=VMEM_LIMIT),
        name="rwkv_scan",
    )(r, k2, v, na, b, logw)


def _even_out_kernel(h_ref, o_ref, r_ref, k_ref, v_ref, g_ref, z_ref, lnw_ref, lnb_ref, rk_ref, seg_ref,
                     wout_ref, g1_ref, g2_ref, g3_ref, w1_ref, w2_ref, out_ref):
    inv_n = 1.0 / RWKV_HEAD
    o = o_ref[...]
    mean = _head_sum(o, seg_ref) * inv_n
    dev = o - mean
    var = _head_sum(dev * dev, seg_ref) * inv_n
    on = dev * lax.rsqrt(var + GN_EPS) * lnw_ref[...] + lnb_ref[...]
    bonus = _head_sum(r_ref[...] * k_ref[...] * rk_ref[...], seg_ref) * v_ref[...]
    om = (on + bonus) * g_ref[...]
    rw = RWKV_WIDTH
    m = _dot(om, wout_ref[:rw, :]) + _dot(z_ref[...], wout_ref[rw:, :])
    hm = h_ref[...] + _rms(m, g1_ref[...], RMS_EPS)
    out_ref[...] = _mlp_residual(hm, g2_ref[...], g3_ref[...], w1_ref, w2_ref)


def _weight_spec(arr):
    return pl.BlockSpec(arr.shape, lambda i: (0,) * arr.ndim, pipeline_mode=pl.Buffered(1))


def _even_out(h, o, r, k2, v, g, z, ln_w, ln_b, r_k, seg, w_out, g1, g2, g3, w1, w2):
    lp = h.shape[0]
    tm = _pick_tile(lp, (256,))
    row_spec = lambda width: pl.BlockSpec((tm, width), lambda i: (i, 0))
    full = lambda arr: pl.BlockSpec(arr.shape, lambda i: (0,) * arr.ndim)
    rw = RWKV_WIDTH
    return pl.pallas_call(
        _even_out_kernel,
        grid=(lp // tm,),
        in_specs=[row_spec(D_MODEL)] + [row_spec(rw)] * 6 +
                 [full(ln_w), full(ln_b), full(r_k), full(seg), _weight_spec(w_out), full(g1), full(g2),
                  full(g3), _weight_spec(w1), _weight_spec(w2)],
        out_specs=row_spec(D_MODEL),
        out_shape=jax.ShapeDtypeStruct((lp, D_MODEL), F32),
        compiler_params=pltpu.CompilerParams(dimension_semantics=("parallel",),
                                             vmem_limit_bytes=VMEM_LIMIT),
        name="even_out",
    )(h, o, r, k2, v, g, z, ln_w, ln_b, r_k, seg, w_out, g1, g2, g3, w1, w2)


def _odd_in_kernel(h_ref, g0_ref, w_ref, wrot_ref, cos_ref, sin_ref, q_out, k_out, v_out):
    hn = _rms(h_ref[...], g0_ref[...], RMS_EPS).astype(BF16)
    y = jnp.dot(hn, w_ref[...], preferred_element_type=F32)
    yr = jnp.dot(hn, wrot_ref[...], preferred_element_type=F32)
    cos = cos_ref[...]
    sin = sin_ref[...]
    scale = DIFF_HEAD ** -0.5
    for j in range(D_MODEL // LANES):
        cols = slice(j * LANES, (j + 1) * LANES)
        kcols = slice(D_MODEL + j * LANES, D_MODEL + (j + 1) * LANES)
        q_out[:, cols] = ((y[:, cols] * cos + yr[:, cols] * sin) * scale).astype(BF16)
        k_out[:, cols] = (y[:, kcols] * cos + yr[:, kcols] * sin).astype(BF16)
    v_out[...] = y[:, 2 * D_MODEL:].astype(BF16)


def _odd_in(h, g0, w, w_rot, cos, sin):
    lp = h.shape[0]
    tm = _pick_tile(lp, (256,))
    row_spec = lambda width: pl.BlockSpec((tm, width), lambda i: (i, 0))
    full = lambda arr: pl.BlockSpec(arr.shape, lambda i: (0,) * arr.ndim)
    sds = jax.ShapeDtypeStruct((lp, D_MODEL), BF16)
    return pl.pallas_call(
        _odd_in_kernel,
        grid=(lp // tm,),
        in_specs=[row_spec(D_MODEL), full(g0), _weight_spec(w), _weight_spec(w_rot),
                  row_spec(LANES), row_spec(LANES)],
        out_specs=[row_spec(D_MODEL)] * 3,
        out_shape=[sds] * 3,
        compiler_params=pltpu.CompilerParams(dimension_semantics=("parallel",),
                                             vmem_limit_bytes=VMEM_LIMIT),
        name="odd_in",
    )(h, g0, w, w_rot, cos, sin)


def _attn_kernel(lam_ref, sw_ref, q_ref, k_ref, v_ref, o_ref, *, lam_init, tq):
    i = pl.program_id(1)
    lane = lax.broadcasted_iota(jnp.int32, (1, LANES), 1)
    mlo = lane < HALF
    q = q_ref[...]
    zq = jnp.zeros_like(q)
    q2 = jnp.concatenate([jnp.where(mlo, q, zq), jnp.where(mlo, zq, q)], axis=0)
    nt = (((1,), (1,)), ((), ()))

    def step(kb, vb, carry, mask):
        m, l, acc = carry
        s = lax.dot_general(q2, kb, nt, preferred_element_type=F32)
        if mask is not None:
            s = jnp.where(mask, s, NEG_BIG)
        m_new = jnp.maximum(m, jnp.max(s, axis=-1, keepdims=True))
        alpha = jnp.exp(m - m_new)
        p = jnp.exp(s - m_new)
        l = alpha * l + jnp.sum(p, axis=-1, keepdims=True)
        acc = alpha * acc + jnp.dot(p.astype(BF16), vb, preferred_element_type=F32)
        return m_new, l, acc

    def body(j, carry):
        off = pl.multiple_of(j * tq, tq)
        return step(k_ref[pl.ds(off, tq), :], v_ref[pl.ds(off, tq), :], carry, None)

    init = (jnp.full((2 * tq, 1), NEG_BIG, F32), jnp.zeros((2 * tq, 1), F32),
            jnp.zeros((2 * tq, LANES), F32))
    carry = lax.fori_loop(0, i, body, init)
    qrow = lax.broadcasted_iota(jnp.int32, (2 * tq, tq), 0)
    qrow = jnp.where(qrow >= tq, qrow - tq, qrow)
    kcol = lax.broadcasted_iota(jnp.int32, (2 * tq, tq), 1)
    off = pl.multiple_of(i * tq, tq)
    _, l, acc = step(k_ref[pl.ds(off, tq), :], v_ref[pl.ds(off, tq), :], carry, kcol <= qrow)

    lv = lam_ref[...]
    lam = (jnp.exp(jnp.sum(lv[0:1] * lv[1:2], axis=-1, keepdims=True))
           - jnp.exp(jnp.sum(lv[2:3] * lv[3:4], axis=-1, keepdims=True)) + lam_init)
    o = acc / l
    o = o[:tq] - lam * o[tq:]
    o = _rms(o, sw_ref[...], SUBLN_EPS) * (1.0 - lam_init)
    o_ref[...] = o.astype(BF16)


def _diff_attn(q, k, v, lam_vecs, subln_w, lam_init):
    lp = q.shape[0]
    tq = ATT_BLOCK
    blk = pl.BlockSpec((tq, LANES), lambda h, i: (i, h))
    head = pl.BlockSpec((lp, LANES), lambda h, i: (0, h))
    full = lambda arr: pl.BlockSpec(arr.shape, lambda h, i: (0,) * arr.ndim)
    return pl.pallas_call(
        functools.partial(_attn_kernel, lam_init=lam_init, tq=tq),
        grid=(DIFF_HEADS, lp // tq),
        in_specs=[full(lam_vecs), full(subln_w), blk, head, head],
        out_specs=blk,
        out_shape=jax.ShapeDtypeStruct((lp, D_MODEL), BF16),
        compiler_params=pltpu.CompilerParams(dimension_semantics=("parallel", "parallel"),
                                             vmem_limit_bytes=VMEM_LIMIT),
        name="diff_attn",
    )(lam_vecs, subln_w, q, k, v)


def _odd_out_kernel(h_ref, o_ref, wout_ref, g1_ref, g2_ref, g3_ref, w1_ref, w2_ref, out_ref):
    m = jnp.dot(o_ref[...], wout_ref[...], preferred_element_type=F32)
    hm = h_ref[...] + _rms(m, g1_ref[...], RMS_EPS)
    out_ref[...] = _mlp_residual(hm, g2_ref[...], g3_ref[...], w1_ref, w2_ref)


def _odd_out(h, o, w_out, g1, g2, g3, w1, w2):
    lp = h.shape[0]
    tm = _pick_tile(lp, (256,))
    row_spec = lambda width: pl.BlockSpec((tm, width), lambda i: (i, 0))
    full = lambda arr: pl.BlockSpec(arr.shape, lambda i: (0,) * arr.ndim)
    return pl.pallas_call(
        _odd_out_kernel,
        grid=(lp // tm,),
        in_specs=[row_spec(D_MODEL), row_spec(D_MODEL), _weight_spec(w_out), full(g1), full(g2), full(g3),
                  _weight_spec(w1), _weight_spec(w2)],
        out_specs=row_spec(D_MODEL),
        out_shape=jax.ShapeDtypeStruct((lp, D_MODEL), F32),
        compiler_params=pltpu.CompilerParams(dimension_semantics=("parallel",),
                                             vmem_limit_bytes=VMEM_LIMIT),
        name="odd_out",
    )(h, o, w_out, g1, g2, g3, w1, w2)


def _rotate_half_weights(w):
    d_in, d_out = w.shape
    w4 = w.reshape(d_in, d_out // DIFF_HEAD, 2, DIFF_HEAD // 2)
    return jnp.concatenate([-w4[:, :, 1:2], w4[:, :, 0:1]], axis=2).reshape(d_in, d_out)


def _forward(x, meta, norm_g, mlp_w1, mlp_w2, ev, od):
    seq = x.shape[0]
    length = N_META + seq
    lp = -(-length // ROW_ALIGN) * ROW_ALIGN
    h = jnp.concatenate([meta.astype(x.dtype), x, jnp.zeros((lp - length, D_MODEL), x.dtype)], axis=0)

    pos = jnp.arange(lp, dtype=F32)
    inv = ROPE_THETA ** (-jnp.arange(0, DIFF_HEAD, 2, dtype=F32) / DIFF_HEAD)
    ang = pos[:, None] * inv[None, :]
    ang = jnp.concatenate([ang, ang, ang, ang], axis=-1)
    cos, sin = jnp.cos(ang), jnp.sin(ang)

    head_id = jnp.arange(RWKV_WIDTH) // RWKV_HEAD
    seg = (head_id[:, None] == head_id[None, :]).astype(BF16)
    row2 = lambda t: t.reshape(1, -1)

    depth = norm_g.shape[0]
    for i in range(depth):
        g = norm_g[i]
        g0, g1, g2, g3 = (row2(g[n]) for n in range(4))
        w1 = mlp_w1[i].astype(BF16)
        w2 = mlp_w2[i].astype(BF16)
        j = i // 2
        if i % 2 == 0:
            (w_in, mu, w0, w_up, a0, a_up, g_up, k_k, k_a, r_k, ln_w, ln_b, pool_w, pool_scale,
             w_out) = (t[j] for t in ev)
            zeros = jnp.zeros((DECAY_RANK, RWKV_WIDTH), F32)
            lora_w = jnp.concatenate([jnp.concatenate([w_up, zeros], axis=1),
                                      jnp.concatenate([zeros, a_up], axis=1)], axis=0).astype(BF16)
            r, k2, v, na, b, logw, gate, z = _even_in(
                h, g0, w_in.astype(BF16), row2(mu), row2(w0), lora_w, row2(a0), g_up.astype(BF16),
                row2(k_k), row2(k_a), seg, pool_w.astype(BF16), row2(pool_scale))
            o = _rwkv_scan(r, k2, v, na, b, logw)
            h = _even_out(h, o, r, k2, v, gate, z, row2(ln_w), row2(ln_b), row2(r_k), seg,
                          w_out.astype(BF16), g1, g2, g3, w1, w2)
        else:
            w_in, lam_vecs, subln_w, w_out = (t[j] for t in od)
            w_rot = _rotate_half_weights(w_in[:, :2 * D_MODEL])
            q, k, v = _odd_in(h, g0, w_in.astype(BF16), w_rot.astype(BF16), cos, sin)
            lam_init = 0.8 - 0.6 * math.exp(-0.3 * i)
            o = _diff_attn(q, k, v, lam_vecs, row2(subln_w), lam_init)
            h = _odd_out(h, o, w_out.astype(BF16), g1, g2, g3, w1, w2)
    return h[N_META:length]


def kernel(x, meta, norm_g, mlp_w1, mlp_w2, ev_w_in, ev_mu, ev_w0, ev_w_up, ev_a0, ev_a_up, ev_g_up, ev_k_k,
           ev_k_a, ev_r_k, ev_ln_w, ev_ln_b, ev_pool_w, ev_pool_scale, ev_w_out, od_w_in, od_lambda,
           od_subln_w, od_w_out):
    ev = (ev_w_in, ev_mu, ev_w0, ev_w_up, ev_a0, ev_a_up, ev_g_up, ev_k_k, ev_k_a, ev_r_k, ev_ln_w, ev_ln_b,
          ev_pool_w, ev_pool_scale, ev_w_out)
    od = (od_w_in, od_lambda, od_subln_w, od_w_out)
    outs = [_forward(x[bi], meta, norm_g, mlp_w1, mlp_w2, ev, od) for bi in range(x.shape[0])]
    return jnp.stack(outs, axis=0)
```

```python
import functools
import math

import jax
import jax.numpy as jnp
from jax import lax
from jax.experimental import pallas as pl
from jax.experimental.pallas import tpu as pltpu

F32, BF16 = jnp.float32, jnp.bfloat16

D_MODEL = 1024
N_META = 16
RMS_EPS = 1e-6
D_FF = 4 * D_MODEL
RWKV_HEAD = 64
RWKV_WIDTH = D_MODEL // 2
DECAY_RANK = 64
ICLR_RANK = 64
GATE_RANK = 128
GN_EPS = RWKV_HEAD * 1e-5
POOL_WIDTH = D_MODEL - RWKV_WIDTH
POOL_WINDOWS = (2, 4, 8, 16)
POOL_GROUP_W = POOL_WIDTH // len(POOL_WINDOWS)
POOL_CARRY = 16
SHIFT_WIDTH = 3 * RWKV_WIDTH + DECAY_RANK + ICLR_RANK + GATE_RANK
EVEN_IN = SHIFT_WIDTH + POOL_WIDTH
DIFF_HEADS = 8
DIFF_HEAD = D_MODEL // (2 * DIFF_HEADS)
SUBLN_EPS = 1e-5
ROPE_THETA = 10000.0

LANES = 128
HALF = LANES // 2
ROW_ALIGN = 256
CHUNK = 64
ATT_Q_BLOCK = 256
ATT_K_BLOCK = 1024
FF_CHUNK = 1024
NEG_BIG = -1e30
VMEM_LIMIT = 56 * 1024 * 1024


def _pick_tile(n, candidates):
    for c in candidates:
        if n % c == 0:
            return c
    raise ValueError(f"no tile in {candidates} divides {n}")


def _rms(t, g, eps):
    return t * lax.rsqrt(jnp.mean(t * t, axis=-1, keepdims=True) + eps) * g


def _split2(x):
    hi = x.astype(BF16)
    lo = (x - hi.astype(F32)).astype(BF16)
    return hi, lo


def _dot(a, b):
    return jnp.dot(a.astype(BF16), b.astype(BF16), preferred_element_type=F32)


def _dot3(a, b):
    ah, al = _split2(a)
    bh, bl = _split2(b)
    d = functools.partial(jnp.dot, preferred_element_type=F32)
    return d(ah, bh) + (d(ah, bl) + d(al, bh))


def _dot_nt3(a, b):
    ah, al = _split2(a)
    bh, bl = _split2(b)
    d = functools.partial(lax.dot_general, dimension_numbers=(((1,), (1,)), ((), ())),
                          preferred_element_type=F32)
    return d(ah, bh) + (d(ah, bl) + d(al, bh))


def _dot_exact_lhs(a_bf16, b):
    bh, bl = _split2(b)
    d = functools.partial(jnp.dot, preferred_element_type=F32)
    return d(a_bf16, bh) + d(a_bf16, bl)


def _head_sum(x, seg_ref):
    return _dot_exact_lhs_rhs(x, seg_ref[...])


def _dot_exact_lhs_rhs(x, ones_bf16):
    xh, xl = _split2(x)
    d = functools.partial(jnp.dot, preferred_element_type=F32)
    return d(xh, ones_bf16) + d(xl, ones_bf16)


def _sigmoid(x):
    return 1.0 / (1.0 + jnp.exp(-x))


def _softplus(x):
    return jnp.maximum(x, 0.0) + jnp.log(1.0 + jnp.exp(-jnp.abs(x)))


def _mlp_residual(hm, g2, g3, w1_ref, w2_ref):
    n = _rms(hm, g2, RMS_EPS).astype(BF16)
    acc = jnp.zeros(hm.shape, F32)
    for c in range(D_FF // FF_CHUNK):
        cols = slice(c * FF_CHUNK, (c + 1) * FF_CHUNK)
        a = jnp.dot(n, w1_ref[:, cols], preferred_element_type=F32)
        a = jnp.square(jnp.maximum(a, 0.0)).astype(BF16)
        acc = acc + jnp.dot(a, w2_ref[cols, :], preferred_element_type=F32)
    return hm + _rms(acc, g3, RMS_EPS)


def _even_in_kernel(h_ref, g0_ref, win_ref, mu_ref, w0_ref, lora_ref, a0_ref, gup_ref,
                    kk_ref, ka_ref, seg_ref, poolw_ref, pscale_ref,
                    r_out, k_out, v_out, na_out, b_out, lw_out, g_out, z_out,
                    ycarry, ucarry, *, tm):
    i = pl.program_id(0)

    @pl.when(i == 0)
    def _():
        ycarry[...] = jnp.zeros(ycarry.shape, F32)
        ucarry[...] = jnp.zeros(ucarry.shape, F32)

    hn = _rms(h_ref[...], g0_ref[...], RMS_EPS).astype(BF16)
    y = jnp.dot(hn, win_ref[...], preferred_element_type=F32)

    ysh = y[:, :SHIFT_WIDTH]
    row = lax.broadcasted_iota(jnp.int32, (tm, 1), 0)
    prev = jnp.where(row == 0, ycarry[7:8, :], pltpu.roll(ysh, 1, axis=0))
    ycarry[...] = ysh[tm - 8:, :]
    ys = ysh + (prev - ysh) * mu_ref[...]

    rw = RWKV_WIDTH
    r = ys[:, 0:rw]
    k = ys[:, rw:2 * rw]
    v = ys[:, 2 * rw:3 * rw]
    wa = ys[:, 3 * rw:3 * rw + LANES]
    gd = ys[:, 3 * rw + LANES:SHIFT_WIDTH]

    lane = lax.broadcasted_iota(jnp.int32, (1, LANES), 1)
    lora_in = jnp.where(lane < DECAY_RANK, jnp.tanh(wa), wa)
    lora = _dot(lora_in, lora_ref[...])
    wlog = -_softplus(-(w0_ref[...] + lora[:, :rw])) - 0.5
    logw = -jnp.exp(wlog)
    a = _sigmoid(a0_ref[...] + lora[:, rw:])
    g = _dot(_sigmoid(gd), gup_ref[...])

    kk = k * kk_ref[...]
    kk = kk * lax.rsqrt(jnp.maximum(_head_sum(kk * kk, seg_ref), 1e-24))
    k2 = k * (1.0 + (a - 1.0) * ka_ref[...])

    r_out[...] = r
    k_out[...] = k2
    v_out[...] = v
    na_out[...] = -kk
    b_out[...] = kk * a
    lw_out[...] = logw
    g_out[...] = g

    u = y[:, SHIFT_WIDTH:]
    ext = jnp.concatenate([ucarry[...], u], axis=0)
    ucarry[...] = u[tm - POOL_CARRY:, :]
    t_idx = i * tm + row
    for gi, win in enumerate(POOL_WINDOWS):
        cols = slice(gi * POOL_GROUP_W, (gi + 1) * POOL_GROUP_W)
        s = ext[:, cols]
        span = 1
        while span < win:
            s = s + pltpu.roll(s, span, axis=0)
            span *= 2
        cnt = jnp.minimum(t_idx + 1, win).astype(F32)
        d = s[POOL_CARRY:, :] / cnt - u[:, cols]
        z_out[:, cols] = _dot(d, poolw_ref[gi]) * pscale_ref[:, cols]


def _even_in(h, g0, w_in, mu, w0, lora_w, a0, g_up, k_k, k_a, seg, pool_w, pool_scale):
    lp = h.shape[0]
    tm = _pick_tile(lp, (256,))
    rw = RWKV_WIDTH
    row_spec = lambda width: pl.BlockSpec((tm, width), lambda i: (i, 0))
    full = lambda arr: pl.BlockSpec(arr.shape, lambda i: (0,) * arr.ndim)
    out_sds = jax.ShapeDtypeStruct((lp, rw), F32)
    return pl.pallas_call(
        functools.partial(_even_in_kernel, tm=tm),
        grid=(lp // tm,),
        in_specs=[row_spec(D_MODEL), full(g0), full(w_in), full(mu), full(w0), full(lora_w), full(a0),
                  full(g_up), full(k_k), full(k_a), full(seg), full(pool_w), full(pool_scale)],
        out_specs=[row_spec(rw)] * 8,
        out_shape=[out_sds] * 8,
        scratch_shapes=[pltpu.VMEM((8, SHIFT_WIDTH), F32), pltpu.VMEM((POOL_CARRY, POOL_WIDTH), F32)],
        compiler_params=pltpu.CompilerParams(dimension_semantics=("arbitrary",),
                                             vmem_limit_bytes=VMEM_LIMIT),
        name="even_in",
    )(h, g0, w_in, mu, w0, lora_w, a0, g_up, k_k, k_a, seg, pool_w, pool_scale)


def _scan_kernel(r_ref, k_ref, v_ref, na_ref, b_ref, lw_ref, o_ref, h_scr):
    @pl.when(pl.program_id(0) == 0)
    def _():
        h_scr[...] = jnp.zeros(h_scr.shape, F32)

    c = CHUNK
    row = lax.broadcasted_iota(jnp.int32, (c, c), 0)
    col = lax.broadcasted_iota(jnp.int32, (c, c), 1)
    strict = col < row
    incl = col <= row
    tri = jnp.where(incl, 1.0, 0.0).astype(BF16)

    lw = lw_ref[...]
    lw_hi = lw.astype(BF16)
    lw_r = lw - lw_hi.astype(F32)
    lw_mid = lw_r.astype(BF16)
    lw_lo = (lw_r - lw_mid.astype(F32)).astype(BF16)
    d = functools.partial(jnp.dot, preferred_element_type=F32)
    cum = d(tri, lw_hi) + (d(tri, lw_mid) + d(tri, lw_lo))

    cum_end = cum[c - 1:c, :]
    e_pos = jnp.exp(cum)
    e_neg = jnp.exp(-cum)
    e_prev = jnp.exp(cum - lw)
    e_end = jnp.exp(cum_end - cum)
    p_end = jnp.exp(cum_end)

    r_t = r_ref[...] * e_pos
    a_t = na_ref[...] * e_prev
    b_all = b_ref[...]
    k_all = k_ref[...]
    b_t = b_all * e_neg
    k_t = k_all * e_neg
    b_h = b_all * e_end
    k_h = k_all * e_end
    v_all = v_ref[...]

    lane = lax.broadcasted_iota(jnp.int32, (1, LANES), 1)
    mlo = lane < HALF
    prow = lax.broadcasted_iota(jnp.int32, (LANES, LANES), 0)
    pcol = lax.broadcasted_iota(jnp.int32, (LANES, LANES), 1)
    same_head = (prow < HALF) == (pcol < HALF)
    diag = prow == pcol

    for p in range(RWKV_WIDTH // LANES):
        cols = slice(p * LANES, (p + 1) * LANES)
        rt, at, bt, kt, bh, kh, vp = (t[:, cols] for t in (r_t, a_t, b_t, k_t, b_h, k_h, v_all))
        zero = jnp.zeros_like(at)
        lhs4 = jnp.concatenate([jnp.where(mlo, at, zero), jnp.where(mlo, zero, at),
                                jnp.where(mlo, rt, zero), jnp.where(mlo, zero, rt)], axis=0)
        a_b = _dot_nt3(lhs4, bt)
        a_k = _dot_nt3(lhs4, kt)

        w_parts, u0_parts, q_parts, o_parts = [], [], [], []
        for hh in range(2):
            n_mat = jnp.where(strict, a_b[hh * c:(hh + 1) * c], 0.0)
            a_ak = jnp.where(strict, a_k[hh * c:(hh + 1) * c], 0.0)
            a_rb = jnp.where(incl, a_b[(2 + hh) * c:(3 + hh) * c], 0.0)
            a_rk = jnp.where(incl, a_k[(2 + hh) * c:(3 + hh) * c], 0.0)
            x = jnp.concatenate([at, _dot3(a_ak, vp)], axis=1)
            m = n_mat
            levels = int(math.log2(c))
            for lev in range(levels):
                x = x + _dot3(m, x)
                if lev < levels - 1:
                    m = _dot3(m, m)
            w_h = x[:, :LANES]
            u0_h = x[:, LANES:]
            w_parts.append(w_h)
            u0_parts.append(u0_h)
            q_parts.append(_dot3(a_rb, w_h))
            o_parts.append(_dot3(jnp.concatenate([a_rb, a_rk], axis=1),
                                 jnp.concatenate([u0_h, vp], axis=0)))
        w_p = jnp.where(mlo, w_parts[0], w_parts[1])
        u0_p = jnp.where(mlo, u0_parts[0], u0_parts[1])
        q_hat = rt + jnp.where(mlo, q_parts[0], q_parts[1])
        o_hat = jnp.where(mlo, o_parts[0], o_parts[1])

        bh_t = bh.T
        kh_t = kh.T
        g_mat = jnp.where(same_head, _dot3(bh_t, w_p), 0.0) + jnp.where(diag, p_end[:, cols], 0.0)
        j_mat = jnp.where(same_head, _dot3(jnp.concatenate([bh_t, kh_t], axis=1),
                                           jnp.concatenate([u0_p, vp], axis=0)), 0.0)
        h0 = h_scr[p]
        o_ref[:, cols] = _dot3(q_hat, h0) + o_hat
        h_scr[p] = _dot3(g_mat, h0) + j_mat


def _rwkv_scan(r, k2, v, na, b, logw):
    lp, rw = r.shape
    spec = pl.BlockSpec((CHUNK, rw), lambda i: (i, 0))
    return pl.pallas_call(
        _scan_kernel,
        grid=(lp // CHUNK,),
        in_specs=[spec] * 6,
        out_specs=spec,
        out_shape=jax.ShapeDtypeStruct((lp, rw), F32),
        scratch_shapes=[pltpu.VMEM((rw // LANES, LANES, LANES), F32)],
        compiler_params=pltpu.CompilerParams(dimension_semantics=("arbitrary",),
                                             vmem_limit_bytes=VMEM_LIMIT),
        name="rwkv_scan",
    )(r, k2, v, na, b, logw)


def _even_out_kernel(h_ref, o_ref, r_ref, k_ref, v_ref, g_ref, z_ref, lnw_ref, lnb_ref, rk_ref, seg_ref,
                     wout_ref, g1_ref, g2_ref, g3_ref, w1_ref, w2_ref, out_ref):
    inv_n = 1.0 / RWKV_HEAD
    o = o_ref[...]
    mean = _head_sum(o, seg_ref) * inv_n
    dev = o - mean
    var = _head_sum(dev * dev, seg_ref) * inv_n
    on = dev * lax.rsqrt(var + GN_EPS) * lnw_ref[...] + lnb_ref[...]
    bonus = _head_sum(r_ref[...] * k_ref[...] * rk_ref[...], seg_ref) * v_ref[...]
    om = (on + bonus) * g_ref[...]
    rw = RWKV_WIDTH
    m = _dot(om, wout_ref[:rw, :]) + _dot(z_ref[...], wout_ref[rw:, :])
    hm = h_ref[...] + _rms(m, g1_ref[...], RMS_EPS)
    out_ref[...] = _mlp_residual(hm, g2_ref[...], g3_ref[...], w1_ref, w2_ref)


def _weight_spec(arr):
    return pl.BlockSpec(arr.shape, lambda i: (0,) * arr.ndim, pipeline_mode=pl.Buffered(1))


def _even_out(h, o, r, k2, v, g, z, ln_w, ln_b, r_k, seg, w_out, g1, g2, g3, w1, w2):
    lp = h.shape[0]
    tm = _pick_tile(lp, (256,))
    row_spec = lambda width: pl.BlockSpec((tm, width), lambda i: (i, 0))
    full = lambda arr: pl.BlockSpec(arr.shape, lambda i: (0,) * arr.ndim)
    rw = RWKV_WIDTH
    return pl.pallas_call(
        _even_out_kernel,
        grid=(lp // tm,),
        in_specs=[row_spec(D_MODEL)] + [row_spec(rw)] * 6 +
                 [full(ln_w), full(ln_b), full(r_k), full(seg), _weight_spec(w_out), full(g1), full(g2),
                  full(g3), _weight_spec(w1), _weight_spec(w2)],
        out_specs=row_spec(D_MODEL),
        out_shape=jax.ShapeDtypeStruct((lp, D_MODEL), F32),
        compiler_params=pltpu.CompilerParams(dimension_semantics=("parallel",),
                                             vmem_limit_bytes=VMEM_LIMIT),
        name="even_out",
    )(h, o, r, k2, v, g, z, ln_w, ln_b, r_k, seg, w_out, g1, g2, g3, w1, w2)


def _odd_in_kernel(h_ref, g0_ref, w_ref, wrot_ref, cos_ref, sin_ref, q_out, k_out, v_out):
    hn = _rms(h_ref[...], g0_ref[...], RMS_EPS).astype(BF16)
    y = jnp.dot(hn, w_ref[...], preferred_element_type=F32)
    yr = jnp.dot(hn, wrot_ref[...], preferred_element_type=F32)
    cos = cos_ref[...]
    sin = sin_ref[...]
    scale = DIFF_HEAD ** -0.5 * math.log2(math.e)
    for j in range(D_MODEL // LANES):
        cols = slice(j * LANES, (j + 1) * LANES)
        kcols = slice(D_MODEL + j * LANES, D_MODEL + (j + 1) * LANES)
        q_out[:, cols] = ((y[:, cols] * cos + yr[:, cols] * sin) * scale).astype(BF16)
        k_out[:, cols] = (y[:, kcols] * cos + yr[:, kcols] * sin).astype(BF16)
    v_out[...] = y[:, 2 * D_MODEL:].astype(BF16)


def _odd_in(h, g0, w, w_rot, cos, sin):
    lp = h.shape[0]
    tm = _pick_tile(lp, (256,))
    row_spec = lambda width: pl.BlockSpec((tm, width), lambda i: (i, 0))
    full = lambda arr: pl.BlockSpec(arr.shape, lambda i: (0,) * arr.ndim)
    sds = jax.ShapeDtypeStruct((lp, D_MODEL), BF16)
    return pl.pallas_call(
        _odd_in_kernel,
        grid=(lp // tm,),
        in_specs=[row_spec(D_MODEL), full(g0), _weight_spec(w), _weight_spec(w_rot),
                  row_spec(LANES), row_spec(LANES)],
        out_specs=[row_spec(D_MODEL)] * 3,
        out_shape=[sds] * 3,
        compiler_params=pltpu.CompilerParams(dimension_semantics=("parallel",),
                                             vmem_limit_bytes=VMEM_LIMIT),
        name="odd_in",
    )(h, g0, w, w_rot, cos, sin)


def _attn_kernel(lam_ref, sw_ref, q_ref, k_ref, v_ref, o_ref, *, lam_init, tq, tk):
    i = pl.program_id(1)
    lane = lax.broadcasted_iota(jnp.int32, (1, LANES), 1)
    mlo = lane < HALF
    q = q_ref[...]
    zq = jnp.zeros_like(q)
    q2 = jnp.concatenate([jnp.where(mlo, q, zq), jnp.where(mlo, zq, q)], axis=0)
    nt = (((1,), (1,)), ((), ()))

    def block(j, carry, masked):
        m, l, acc = carry
        off = pl.multiple_of(j * tk, tk)
        s = lax.dot_general(q2, k_ref[pl.ds(off, tk), :], nt, preferred_element_type=F32)
        if masked:
            qrow = lax.broadcasted_iota(jnp.int32, (2 * tq, tk), 0)
            qrow = jnp.where(qrow >= tq, qrow - tq, qrow)
            kcol = lax.broadcasted_iota(jnp.int32, (2 * tq, tk), 1)
            s = jnp.where(kcol - qrow <= i * tq - off, s, NEG_BIG)
        m_new = jnp.maximum(m, jnp.max(s, axis=-1, keepdims=True))
        alpha = jnp.exp2(m - m_new)
        p = jnp.exp2(s - m_new)
        l = alpha * l + jnp.sum(p, axis=-1, keepdims=True)
        acc = alpha * acc + jnp.dot(p.astype(BF16), v_ref[pl.ds(off, tk), :], preferred_element_type=F32)
        return m_new, l, acc

    init = (jnp.full((2 * tq, 1), NEG_BIG, F32), jnp.zeros((2 * tq, 1), F32),
            jnp.zeros((2 * tq, LANES), F32))
    n_full = (i * tq) // tk
    carry = lax.fori_loop(0, n_full, lambda j, c: block(j, c, False), init)
    _, l, acc = block(n_full, carry, True)

    lv = lam_ref[...]
    lam = (jnp.exp(jnp.sum(lv[0:1] * lv[1:2], axis=-1, keepdims=True))
           - jnp.exp(jnp.sum(lv[2:3] * lv[3:4], axis=-1, keepdims=True)) + lam_init)
    o = acc / l
    o = o[:tq] - lam * o[tq:]
    o = _rms(o, sw_ref[...], SUBLN_EPS) * (1.0 - lam_init)
    o_ref[...] = o.astype(BF16)


def _diff_attn(q, k, v, lam_vecs, subln_w, lam_init):
    lp = q.shape[0]
    tq, tk = ATT_Q_BLOCK, ATT_K_BLOCK
    lpk = -(-lp // tk) * tk
    k = jnp.pad(k, ((0, lpk - lp), (0, 0)))
    v = jnp.pad(v, ((0, lpk - lp), (0, 0)))
    blk = pl.BlockSpec((tq, LANES), lambda h, i: (i, h))
    head = pl.BlockSpec((lpk, LANES), lambda h, i: (0, h))
    full = lambda arr: pl.BlockSpec(arr.shape, lambda h, i: (0,) * arr.ndim)
    return pl.pallas_call(
        functools.partial(_attn_kernel, lam_init=lam_init, tq=tq, tk=tk),
        grid=(DIFF_HEADS, lp // tq),
        in_specs=[full(lam_vecs), full(subln_w), blk, head, head],
        out_specs=blk,
        out_shape=jax.ShapeDtypeStruct((lp, D_MODEL), BF16),
        compiler_params=pltpu.CompilerParams(dimension_semantics=("parallel", "parallel"),
                                             vmem_limit_bytes=VMEM_LIMIT),
        name="diff_attn",
    )(lam_vecs, subln_w, q, k, v)


def _odd_out_kernel(h_ref, o_ref, wout_ref, g1_ref, g2_ref, g3_ref, w1_ref, w2_ref, out_ref):
    m = jnp.dot(o_ref[...], wout_ref[...], preferred_element_type=F32)
    hm = h_ref[...] + _rms(m, g1_ref[...], RMS_EPS)
    out_ref[...] = _mlp_residual(hm, g2_ref[...], g3_ref[...], w1_ref, w2_ref)


def _odd_out(h, o, w_out, g1, g2, g3, w1, w2):
    lp = h.shape[0]
    tm = _pick_tile(lp, (256,))
    row_spec = lambda width: pl.BlockSpec((tm, width), lambda i: (i, 0))
    full = lambda arr: pl.BlockSpec(arr.shape, lambda i: (0,) * arr.ndim)
    return pl.pallas_call(
        _odd_out_kernel,
        grid=(lp // tm,),
        in_specs=[row_spec(D_MODEL), row_spec(D_MODEL), _weight_spec(w_out), full(g1), full(g2), full(g3),
                  _weight_spec(w1), _weight_spec(w2)],
        out_specs=row_spec(D_MODEL),
        out_shape=jax.ShapeDtypeStruct((lp, D_MODEL), F32),
        compiler_params=pltpu.CompilerParams(dimension_semantics=("parallel",),
                                             vmem_limit_bytes=VMEM_LIMIT),
        name="odd_out",
    )(h, o, w_out, g1, g2, g3, w1, w2)


def _rotate_half_weights(w):
    d_in, d_out = w.shape
    w4 = w.reshape(d_in, d_out // DIFF_HEAD, 2, DIFF_HEAD // 2)
    return jnp.concatenate([-w4[:, :, 1:2], w4[:, :, 0:1]], axis=2).reshape(d_in, d_out)


def _forward(x, meta, norm_g, mlp_w1, mlp_w2, ev, od):
    seq = x.shape[0]
    length = N_META + seq
    lp = -(-length // ROW_ALIGN) * ROW_ALIGN
    h = jnp.concatenate([meta.astype(x.dtype), x, jnp.zeros((lp - length, D_MODEL), x.dtype)], axis=0)

    pos = jnp.arange(lp, dtype=F32)
    inv = ROPE_THETA ** (-jnp.arange(0, DIFF_HEAD, 2, dtype=F32) / DIFF_HEAD)
    ang = pos[:, None] * inv[None, :]
    ang = jnp.concatenate([ang, ang, ang, ang], axis=-1)
    cos, sin = jnp.cos(ang), jnp.sin(ang)

    head_id = jnp.arange(RWKV_WIDTH) // RWKV_HEAD
    seg = (head_id[:, None] == head_id[None, :]).astype(BF16)
    row2 = lambda t: t.reshape(1, -1)

    depth = norm_g.shape[0]
    for i in range(depth):
        g = norm_g[i]
        g0, g1, g2, g3 = (row2(g[n]) for n in range(4))
        w1 = mlp_w1[i].astype(BF16)
        w2 = mlp_w2[i].astype(BF16)
        j = i // 2
        if i % 2 == 0:
            (w_in, mu, w0, w_up, a0, a_up, g_up, k_k, k_a, r_k, ln_w, ln_b, pool_w, pool_scale,
             w_out) = (t[j] for t in ev)
            zeros = jnp.zeros((DECAY_RANK, RWKV_WIDTH), F32)
            lora_w = jnp.concatenate([jnp.concatenate([w_up, zeros], axis=1),
                                      jnp.concatenate([zeros, a_up], axis=1)], axis=0).astype(BF16)
            r, k2, v, na, b, logw, gate, z = _even_in(
                h, g0, w_in.astype(BF16), row2(mu), row2(w0), lora_w, row2(a0), g_up.astype(BF16),
                row2(k_k), row2(k_a), seg, pool_w.astype(BF16), row2(pool_scale))
            o = _rwkv_scan(r, k2, v, na, b, logw)
            h = _even_out(h, o, r, k2, v, gate, z, row2(ln_w), row2(ln_b), row2(r_k), seg,
                          w_out.astype(BF16), g1, g2, g3, w1, w2)
        else:
            w_in, lam_vecs, subln_w, w_out = (t[j] for t in od)
            w_rot = _rotate_half_weights(w_in[:, :2 * D_MODEL])
            q, k, v = _odd_in(h, g0, w_in.astype(BF16), w_rot.astype(BF16), cos, sin)
            lam_init = 0.8 - 0.6 * math.exp(-0.3 * i)
            o = _diff_attn(q, k, v, lam_vecs, row2(subln_w), lam_init)
            h = _odd_out(h, o, w_out.astype(BF16), g1, g2, g3, w1, w2)
    return h[N_META:length]


def kernel(x, meta, norm_g, mlp_w1, mlp_w2, ev_w_in, ev_mu, ev_w0, ev_w_up, ev_a0, ev_a_up, ev_g_up, ev_k_k,
           ev_k_a, ev_r_k, ev_ln_w, ev_ln_b, ev_pool_w, ev_pool_scale, ev_w_out, od_w_in, od_lambda,
           od_subln_w, od_w_out):
    ev = (ev_w_in, ev_mu, ev_w0, ev_w_up, ev_a0, ev_a_up, ev_g_up, ev_k_k, ev_k_a, ev_r_k, ev_ln_w, ev_ln_b,
          ev_pool_w, ev_pool_scale, ev_w_out)
    od = (od_w_in, od_lambda, od_subln_w, od_w_out)
    outs = [_forward(x[bi], meta, norm_g, mlp_w1, mlp_w2, ev, od) for bi in range(x.shape[0])]
    return jnp.stack(outs, axis=0)
```

```python
import functools
import math

import jax
import jax.numpy as jnp
from jax import lax
from jax.experimental import pallas as pl
from jax.experimental.pallas import tpu as pltpu

F32, BF16 = jnp.float32, jnp.bfloat16

D_MODEL = 1024
N_META = 16
RMS_EPS = 1e-6
D_FF = 4 * D_MODEL
RWKV_HEAD = 64
RWKV_WIDTH = D_MODEL // 2
DECAY_RANK = 64
ICLR_RANK = 64
GATE_RANK = 128
GN_EPS = RWKV_HEAD * 1e-5
POOL_WIDTH = D_MODEL - RWKV_WIDTH
POOL_WINDOWS = (2, 4, 8, 16)
POOL_GROUP_W = POOL_WIDTH // len(POOL_WINDOWS)
POOL_CARRY = 16
SHIFT_WIDTH = 3 * RWKV_WIDTH + DECAY_RANK + ICLR_RANK + GATE_RANK
EVEN_IN = SHIFT_WIDTH + POOL_WIDTH
DIFF_HEADS = 8
DIFF_HEAD = D_MODEL // (2 * DIFF_HEADS)
SUBLN_EPS = 1e-5
ROPE_THETA = 10000.0

LANES = 128
HALF = LANES // 2
ROW_ALIGN = 256
CHUNK = 64
SOLVE_BLOCK = 8
ATT_Q_BLOCK = 256
ATT_K_BLOCK = 1024
FF_CHUNK = 1024
NEG_BIG = -1e30
VMEM_LIMIT = 56 * 1024 * 1024


def _pick_tile(n, candidates):
    for c in candidates:
        if n % c == 0:
            return c
    raise ValueError(f"no tile in {candidates} divides {n}")


def _rms(t, g, eps):
    return t * lax.rsqrt(jnp.mean(t * t, axis=-1, keepdims=True) + eps) * g


def _split2(x):
    hi = x.astype(BF16)
    lo = (x - hi.astype(F32)).astype(BF16)
    return hi, lo


def _dot(a, b):
    return jnp.dot(a.astype(BF16), b.astype(BF16), preferred_element_type=F32)


def _dot3(a, b):
    ah, al = _split2(a)
    bh, bl = _split2(b)
    d = functools.partial(jnp.dot, preferred_element_type=F32)
    return d(ah, bh) + (d(ah, bl) + d(al, bh))


def _dot_nt3(a, b):
    ah, al = _split2(a)
    bh, bl = _split2(b)
    d = functools.partial(lax.dot_general, dimension_numbers=(((1,), (1,)), ((), ())),
                          preferred_element_type=F32)
    return d(ah, bh) + (d(ah, bl) + d(al, bh))


def _dot_exact_lhs(a_bf16, b):
    bh, bl = _split2(b)
    d = functools.partial(jnp.dot, preferred_element_type=F32)
    return d(a_bf16, bh) + d(a_bf16, bl)


def _head_sum(x, seg_ref):
    return _dot_exact_lhs_rhs(x, seg_ref[...])


def _dot_exact_lhs_rhs(x, ones_bf16):
    xh, xl = _split2(x)
    d = functools.partial(jnp.dot, preferred_element_type=F32)
    return d(xh, ones_bf16) + d(xl, ones_bf16)


def _sigmoid(x):
    return 1.0 / (1.0 + jnp.exp(-x))


def _softplus(x):
    return jnp.maximum(x, 0.0) + jnp.log(1.0 + jnp.exp(-jnp.abs(x)))


def _mlp_residual(hm, g2, g3, w1_ref, w2_ref):
    n = _rms(hm, g2, RMS_EPS).astype(BF16)
    acc = jnp.zeros(hm.shape, F32)
    for c in range(D_FF // FF_CHUNK):
        cols = slice(c * FF_CHUNK, (c + 1) * FF_CHUNK)
        a = jnp.dot(n, w1_ref[:, cols], preferred_element_type=F32)
        a = jnp.square(jnp.maximum(a, 0.0)).astype(BF16)
        acc = acc + jnp.dot(a, w2_ref[cols, :], preferred_element_type=F32)
    return hm + _rms(acc, g3, RMS_EPS)


def _even_in_kernel(h_ref, g0_ref, win_ref, mu_ref, w0_ref, lora_ref, a0_ref, gup_ref,
                    kk_ref, ka_ref, seg_ref, poolw_ref, pscale_ref,
                    r_out, k_out, v_out, na_out, b_out, lw_out, g_out, z_out,
                    ycarry, ucarry, *, tm):
    i = pl.program_id(0)

    @pl.when(i == 0)
    def _():
        ycarry[...] = jnp.zeros(ycarry.shape, F32)
        ucarry[...] = jnp.zeros(ucarry.shape, F32)

    hn = _rms(h_ref[...], g0_ref[...], RMS_EPS).astype(BF16)
    y = jnp.dot(hn, win_ref[...], preferred_element_type=F32)

    ysh = y[:, :SHIFT_WIDTH]
    row = lax.broadcasted_iota(jnp.int32, (tm, 1), 0)
    prev = jnp.where(row == 0, ycarry[7:8, :], pltpu.roll(ysh, 1, axis=0))
    ycarry[...] = ysh[tm - 8:, :]
    ys = ysh + (prev - ysh) * mu_ref[...]

    rw = RWKV_WIDTH
    r = ys[:, 0:rw]
    k = ys[:, rw:2 * rw]
    v = ys[:, 2 * rw:3 * rw]
    wa = ys[:, 3 * rw:3 * rw + LANES]
    gd = ys[:, 3 * rw + LANES:SHIFT_WIDTH]

    lane = lax.broadcasted_iota(jnp.int32, (1, LANES), 1)
    lora_in = jnp.where(lane < DECAY_RANK, jnp.tanh(wa), wa)
    lora = _dot(lora_in, lora_ref[...])
    wlog = -_softplus(-(w0_ref[...] + lora[:, :rw])) - 0.5
    logw = -jnp.exp(wlog)
    a = _sigmoid(a0_ref[...] + lora[:, rw:])
    g = _dot(_sigmoid(gd), gup_ref[...])

    kk = k * kk_ref[...]
    kk = kk * lax.rsqrt(jnp.maximum(_head_sum(kk * kk, seg_ref), 1e-24))
    k2 = k * (1.0 + (a - 1.0) * ka_ref[...])

    r_out[...] = r
    k_out[...] = k2
    v_out[...] = v
    na_out[...] = -kk
    b_out[...] = kk * a
    lw_out[...] = logw
    g_out[...] = g

    u = y[:, SHIFT_WIDTH:]
    ext = jnp.concatenate([ucarry[...], u], axis=0)
    ucarry[...] = u[tm - POOL_CARRY:, :]
    t_idx = i * tm + row
    for gi, win in enumerate(POOL_WINDOWS):
        cols = slice(gi * POOL_GROUP_W, (gi + 1) * POOL_GROUP_W)
        s = ext[:, cols]
        span = 1
        while span < win:
            s = s + pltpu.roll(s, span, axis=0)
            span *= 2
        cnt = jnp.minimum(t_idx + 1, win).astype(F32)
        d = s[POOL_CARRY:, :] / cnt - u[:, cols]
        z_out[:, cols] = _dot(d, poolw_ref[gi]) * pscale_ref[:, cols]


def _even_in(h, g0, w_in, mu, w0, lora_w, a0, g_up, k_k, k_a, seg, pool_w, pool_scale):
    lp = h.shape[0]
    tm = _pick_tile(lp, (256,))
    rw = RWKV_WIDTH
    row_spec = lambda width: pl.BlockSpec((tm, width), lambda i: (i, 0))
    full = lambda arr: pl.BlockSpec(arr.shape, lambda i: (0,) * arr.ndim)
    out_sds = jax.ShapeDtypeStruct((lp, rw), F32)
    return pl.pallas_call(
        functools.partial(_even_in_kernel, tm=tm),
        grid=(lp // tm,),
        in_specs=[row_spec(D_MODEL), full(g0), full(w_in), full(mu), full(w0), full(lora_w), full(a0),
                  full(g_up), full(k_k), full(k_a), full(seg), full(pool_w), full(pool_scale)],
        out_specs=[row_spec(rw)] * 8,
        out_shape=[out_sds] * 8,
        scratch_shapes=[pltpu.VMEM((8, SHIFT_WIDTH), F32), pltpu.VMEM((POOL_CARRY, POOL_WIDTH), F32)],
        compiler_params=pltpu.CompilerParams(dimension_semantics=("arbitrary",),
                                             vmem_limit_bytes=VMEM_LIMIT),
        name="even_in",
    )(h, g0, w_in, mu, w0, lora_w, a0, g_up, k_k, k_a, seg, pool_w, pool_scale)


def _scan_kernel(r_ref, k_ref, v_ref, na_ref, b_ref, lw_ref, o_ref, h_scr):
    @pl.when(pl.program_id(0) == 0)
    def _():
        h_scr[...] = jnp.zeros(h_scr.shape, F32)

    c = CHUNK
    row = lax.broadcasted_iota(jnp.int32, (c, c), 0)
    col = lax.broadcasted_iota(jnp.int32, (c, c), 1)
    incl = col <= row
    tri = jnp.where(incl, 1.0, 0.0).astype(BF16)

    lw = lw_ref[...]
    lw_hi = lw.astype(BF16)
    lw_r = lw - lw_hi.astype(F32)
    lw_mid = lw_r.astype(BF16)
    lw_lo = (lw_r - lw_mid.astype(F32)).astype(BF16)
    d = functools.partial(jnp.dot, preferred_element_type=F32)
    cum = d(tri, lw_hi) + (d(tri, lw_mid) + d(tri, lw_lo))

    cum_end = cum[c - 1:c, :]
    e_pos = jnp.exp(cum)
    e_neg = jnp.exp(-cum)
    e_prev = jnp.exp(cum - lw)
    e_end = jnp.exp(cum_end - cum)
    p_end = jnp.exp(cum_end)

    r_t = r_ref[...] * e_pos
    a_t = na_ref[...] * e_prev
    b_all = b_ref[...]
    k_all = k_ref[...]
    b_t = b_all * e_neg
    k_t = k_all * e_neg
    b_h = b_all * e_end
    k_h = k_all * e_end
    v_all = v_ref[...]

    lane = lax.broadcasted_iota(jnp.int32, (1, LANES), 1)
    mlo = lane < HALF
    prow = lax.broadcasted_iota(jnp.int32, (LANES, LANES), 0)
    pcol = lax.broadcasted_iota(jnp.int32, (LANES, LANES), 1)
    same_head = (prow < HALF) == (pcol < HALF)
    diag = prow == pcol
    trow = lax.broadcasted_iota(jnp.int32, (c, LANES), 0)
    tcol = lax.broadcasted_iota(jnp.int32, (c, LANES), 1)
    tcol = jnp.where(tcol >= HALF, tcol - HALF, tcol)
    strict2 = tcol < trow
    incl2 = jnp.concatenate([tcol <= trow] * 2, axis=0)
    zeros_c = jnp.zeros((c, LANES), F32)
    n_pairs = RWKV_WIDTH // LANES
    pair_cols = [slice(p * LANES, (p + 1) * LANES) for p in range(n_pairs)]

    a_all = []
    for cols in pair_cols:
        rt, at = r_t[:, cols], a_t[:, cols]
        lhs4 = jnp.concatenate([jnp.where(mlo, at, zeros_c), jnp.where(mlo, zeros_c, at),
                                jnp.where(mlo, rt, zeros_c), jnp.where(mlo, zeros_c, rt)], axis=0)
        a_all.append(_dot_nt3(lhs4, jnp.concatenate([b_t[:, cols], k_t[:, cols]], axis=0)))

    same_blk = (tcol // SOLVE_BLOCK) == (trow // SOLVE_BLOCK)
    lane2 = lax.broadcasted_iota(jnp.int32, (1, 2 * LANES), 1)
    zeros_2c = jnp.zeros((c, 2 * LANES), F32)
    heads = []
    for p, cols in enumerate(pair_cols):
        at, vp = a_t[:, cols], v_all[:, cols]
        at_sw = pltpu.roll(at, HALF, axis=1)
        vp_sw = pltpu.roll(vp, HALF, axis=1)
        for hh in range(2):
            nk = jnp.where(strict2, a_all[p][hh * c:(hh + 1) * c], 0.0)
            av = _dot3(nk, jnp.concatenate([zeros_c, vp_sw if hh == 0 else vp], axis=0))
            x0 = jnp.where(mlo, at if hh == 0 else at_sw, av)
            nk_sw = pltpu.roll(nk, HALF, axis=1)
            n_split = jnp.where(mlo, jnp.where(same_blk, 0.0, nk), jnp.where(same_blk, nk_sw, 0.0))
            heads.append(jnp.concatenate([x0, n_split], axis=1))

    inner_levels = int(math.log2(SOLVE_BLOCK))
    for _ in range(inner_levels):
        nxt = []
        for y in heads:
            prod = _dot3(y[:, LANES:], jnp.concatenate([zeros_2c, y], axis=0))
            nxt.append(jnp.where(lane2 >= LANES + HALF, prod, y + prod))
        heads = nxt
    outer_levels = int(math.log2(c // SOLVE_BLOCK))
    for _ in range(outer_levels):
        nxt = []
        for y in heads:
            prod = _dot3(y[:, LANES:], jnp.concatenate([y, zeros_2c], axis=0))
            nxt.append(jnp.where(lane2 < LANES, y + prod, prod))
        heads = nxt

    big = []
    for p, cols in enumerate(pair_cols):
        x_lo, x_hi = heads[2 * p][:, :LANES], heads[2 * p + 1][:, :LANES]
        w_p = jnp.where(mlo, x_lo, pltpu.roll(x_hi, HALF, axis=1))
        u0_p = jnp.where(mlo, pltpu.roll(x_lo, HALF, axis=1), x_hi)
        rhs = jnp.concatenate([jnp.concatenate([w_p, u0_p], axis=1),
                               jnp.concatenate([zeros_c, v_all[:, cols]], axis=1)], axis=0)
        a_r = jnp.where(incl2, a_all[p][2 * c:], 0.0)
        bk_t = jnp.concatenate([b_h[:, cols], k_h[:, cols]], axis=0).T
        big.append(_dot3(jnp.concatenate([a_r, bk_t], axis=0), rhs))

    for p, cols in enumerate(pair_cols):
        res = big[p]
        q_hat = r_t[:, cols] + jnp.where(mlo, res[:c, :LANES], res[c:2 * c, :LANES])
        o_hat = jnp.where(mlo, res[:c, LANES:], res[c:2 * c, LANES:])
        g_mat = jnp.where(same_head, res[2 * c:, :LANES], 0.0) + jnp.where(diag, p_end[:, cols], 0.0)
        j_mat = jnp.where(same_head, res[2 * c:, LANES:], 0.0)
        st = _dot3(jnp.concatenate([q_hat, g_mat], axis=0), h_scr[p])
        o_ref[:, cols] = st[:c] + o_hat
        h_scr[p] = st[c:] + j_mat


def _rwkv_scan(r, k2, v, na, b, logw):
    lp, rw = r.shape
    spec = pl.BlockSpec((CHUNK, rw), lambda i: (i, 0))
    return pl.pallas_call(
        _scan_kernel,
        grid=(lp // CHUNK,),
        in_specs=[spec] * 6,
        out_specs=spec,
        out_shape=jax.ShapeDtypeStruct((lp, rw), F32),
        scratch_shapes=[pltpu.VMEM((rw // LANES, LANES, LANES), F32)],
        compiler_params=pltpu.CompilerParams(dimension_semantics=("arbitrary",),
                                             vmem_limit_bytes=VMEM_LIMIT),
        name="rwkv_scan",
    )(r, k2, v, na, b, logw)


def _even_out_kernel(h_ref, o_ref, r_ref, k_ref, v_ref, g_ref, z_ref, lnw_ref, lnb_ref, rk_ref, seg_ref,
                     wout_ref, g1_ref, g2_ref, g3_ref, w1_ref, w2_ref, out_ref):
    inv_n = 1.0 / RWKV_HEAD
    o = o_ref[...]
    mean = _head_sum(o, seg_ref) * inv_n
    dev = o - mean
    var = _head_sum(dev * dev, seg_ref) * inv_n
    on = dev * lax.rsqrt(var + GN_EPS) * lnw_ref[...] + lnb_ref[...]
    bonus = _head_sum(r_ref[...] * k_ref[...] * rk_ref[...], seg_ref) * v_ref[...]
    om = (on + bonus) * g_ref[...]
    rw = RWKV_WIDTH
    m = _dot(om, wout_ref[:rw, :]) + _dot(z_ref[...], wout_ref[rw:, :])
    hm = h_ref[...] + _rms(m, g1_ref[...], RMS_EPS)
    out_ref[...] = _mlp_residual(hm, g2_ref[...], g3_ref[...], w1_ref, w2_ref)


def _weight_spec(arr):
    return pl.BlockSpec(arr.shape, lambda i: (0,) * arr.ndim, pipeline_mode=pl.Buffered(1))


def _even_out(h, o, r, k2, v, g, z, ln_w, ln_b, r_k, seg, w_out, g1, g2, g3, w1, w2):
    lp = h.shape[0]
    tm = _pick_tile(lp, (256,))
    row_spec = lambda width: pl.BlockSpec((tm, width), lambda i: (i, 0))
    full = lambda arr: pl.BlockSpec(arr.shape, lambda i: (0,) * arr.ndim)
    rw = RWKV_WIDTH
    return pl.pallas_call(
        _even_out_kernel,
        grid=(lp // tm,),
        in_specs=[row_spec(D_MODEL)] + [row_spec(rw)] * 6 +
                 [full(ln_w), full(ln_b), full(r_k), full(seg), _weight_spec(w_out), full(g1), full(g2),
                  full(g3), _weight_spec(w1), _weight_spec(w2)],
        out_specs=row_spec(D_MODEL),
        out_shape=jax.ShapeDtypeStruct((lp, D_MODEL), F32),
        compiler_params=pltpu.CompilerParams(dimension_semantics=("parallel",),
                                             vmem_limit_bytes=VMEM_LIMIT),
        name="even_out",
    )(h, o, r, k2, v, g, z, ln_w, ln_b, r_k, seg, w_out, g1, g2, g3, w1, w2)


def _odd_in_kernel(h_ref, g0_ref, w_ref, wrot_ref, cos_ref, sin_ref, q_out, k_out, v_out):
    hn = _rms(h_ref[...], g0_ref[...], RMS_EPS).astype(BF16)
    y = jnp.dot(hn, w_ref[...], preferred_element_type=F32)
    yr = jnp.dot(hn, wrot_ref[...], preferred_element_type=F32)
    cos = cos_ref[...]
    sin = sin_ref[...]
    scale = DIFF_HEAD ** -0.5 * math.log2(math.e)
    for j in range(D_MODEL // LANES):
        cols = slice(j * LANES, (j + 1) * LANES)
        kcols = slice(D_MODEL + j * LANES, D_MODEL + (j + 1) * LANES)
        q_out[:, cols] = ((y[:, cols] * cos + yr[:, cols] * sin) * scale).astype(BF16)
        k_out[:, cols] = (y[:, kcols] * cos + yr[:, kcols] * sin).astype(BF16)
    v_out[...] = y[:, 2 * D_MODEL:].astype(BF16)


def _odd_in(h, g0, w, w_rot, cos, sin):
    lp = h.shape[0]
    tm = _pick_tile(lp, (256,))
    row_spec = lambda width: pl.BlockSpec((tm, width), lambda i: (i, 0))
    full = lambda arr: pl.BlockSpec(arr.shape, lambda i: (0,) * arr.ndim)
    sds = jax.ShapeDtypeStruct((lp, D_MODEL), BF16)
    return pl.pallas_call(
        _odd_in_kernel,
        grid=(lp // tm,),
        in_specs=[row_spec(D_MODEL), full(g0), _weight_spec(w), _weight_spec(w_rot),
                  row_spec(LANES), row_spec(LANES)],
        out_specs=[row_spec(D_MODEL)] * 3,
        out_shape=[sds] * 3,
        compiler_params=pltpu.CompilerParams(dimension_semantics=("parallel",),
                                             vmem_limit_bytes=VMEM_LIMIT),
        name="odd_in",
    )(h, g0, w, w_rot, cos, sin)


def _attn_kernel(lam_ref, sw_ref, q_ref, k_ref, v_ref, o_ref, *, lam_init, tq, tk):
    i = pl.program_id(1)
    lane = lax.broadcasted_iota(jnp.int32, (1, LANES), 1)
    mlo = lane < HALF
    q = q_ref[...]
    zq = jnp.zeros_like(q)
    q2 = jnp.concatenate([jnp.where(mlo, q, zq), jnp.where(mlo, zq, q)], axis=0)
    nt = (((1,), (1,)), ((), ()))

    def block(j, carry, masked):
        m, l, acc = carry
        off = pl.multiple_of(j * tk, tk)
        s = lax.dot_general(q2, k_ref[pl.ds(off, tk), :], nt, preferred_element_type=F32)
        if masked:
            qrow = lax.broadcasted_iota(jnp.int32, (2 * tq, tk), 0)
            qrow = jnp.where(qrow >= tq, qrow - tq, qrow)
            kcol = lax.broadcasted_iota(jnp.int32, (2 * tq, tk), 1)
            s = jnp.where(kcol - qrow <= i * tq - off, s, NEG_BIG)
        m_new = jnp.maximum(m, jnp.max(s, axis=-1, keepdims=True))
        alpha = jnp.exp2(m - m_new)
        p = jnp.exp2(s - m_new)
        l = alpha * l + jnp.sum(p, axis=-1, keepdims=True)
        acc = alpha * acc + jnp.dot(p.astype(BF16), v_ref[pl.ds(off, tk), :], preferred_element_type=F32)
        return m_new, l, acc

    init = (jnp.full((2 * tq, 1), NEG_BIG, F32), jnp.zeros((2 * tq, 1), F32),
            jnp.zeros((2 * tq, LANES), F32))
    n_full = (i * tq) // tk
    carry = lax.fori_loop(0, n_full, lambda j, c: block(j, c, False), init)
    _, l, acc = block(n_full, carry, True)

    lv = lam_ref[...]
    lam = (jnp.exp(jnp.sum(lv[0:1] * lv[1:2], axis=-1, keepdims=True))
           - jnp.exp(jnp.sum(lv[2:3] * lv[3:4], axis=-1, keepdims=True)) + lam_init)
    o = acc / l
    o = o[:tq] - lam * o[tq:]
    o = _rms(o, sw_ref[...], SUBLN_EPS) * (1.0 - lam_init)
    o_ref[...] = o.astype(BF16)


def _diff_attn(q, k, v, lam_vecs, subln_w, lam_init):
    lp = q.shape[0]
    tq, tk = ATT_Q_BLOCK, ATT_K_BLOCK
    lpk = -(-lp // tk) * tk
    k = jnp.pad(k, ((0, lpk - lp), (0, 0)))
    v = jnp.pad(v, ((0, lpk - lp), (0, 0)))
    blk = pl.BlockSpec((tq, LANES), lambda h, i: (i, h))
    head = pl.BlockSpec((lpk, LANES), lambda h, i: (0, h))
    full = lambda arr: pl.BlockSpec(arr.shape, lambda h, i: (0,) * arr.ndim)
    return pl.pallas_call(
        functools.partial(_attn_kernel, lam_init=lam_init, tq=tq, tk=tk),
        grid=(DIFF_HEADS, lp // tq),
        in_specs=[full(lam_vecs), full(subln_w), blk, head, head],
        out_specs=blk,
        out_shape=jax.ShapeDtypeStruct((lp, D_MODEL), BF16),
        compiler_params=pltpu.CompilerParams(dimension_semantics=("parallel", "parallel"),
                                             vmem_limit_bytes=VMEM_LIMIT),
        name="diff_attn",
    )(lam_vecs, subln_w, q, k, v)


def _odd_out_kernel(h_ref, o_ref, wout_ref, g1_ref, g2_ref, g3_ref, w1_ref, w2_ref, out_ref):
    m = jnp.dot(o_ref[...], wout_ref[...], preferred_element_type=F32)
    hm = h_ref[...] + _rms(m, g1_ref[...], RMS_EPS)
    out_ref[...] = _mlp_residual(hm, g2_ref[...], g3_ref[...], w1_ref, w2_ref)


def _odd_out(h, o, w_out, g1, g2, g3, w1, w2):
    lp = h.shape[0]
    tm = _pick_tile(lp, (256,))
    row_spec = lambda width: pl.BlockSpec((tm, width), lambda i: (i, 0))
    full = lambda arr: pl.BlockSpec(arr.shape, lambda i: (0,) * arr.ndim)
    return pl.pallas_call(
        _odd_out_kernel,
        grid=(lp // tm,),
        in_specs=[row_spec(D_MODEL), row_spec(D_MODEL), _weight_spec(w_out), full(g1), full(g2), full(g3),
                  _weight_spec(w1), _weight_spec(w2)],
        out_specs=row_spec(D_MODEL),
        out_shape=jax.ShapeDtypeStruct((lp, D_MODEL), F32),
        compiler_params=pltpu.CompilerParams(dimension_semantics=("parallel",),
                                             vmem_limit_bytes=VMEM_LIMIT),
        name="odd_out",
    )(h, o, w_out, g1, g2, g3, w1, w2)


def _rotate_half_weights(w):
    d_in, d_out = w.shape
    w4 = w.reshape(d_in, d_out // DIFF_HEAD, 2, DIFF_HEAD // 2)
    return jnp.concatenate([-w4[:, :, 1:2], w4[:, :, 0:1]], axis=2).reshape(d_in, d_out)


def _forward(x, meta, norm_g, mlp_w1, mlp_w2, ev, od):
    seq = x.shape[0]
    length = N_META + seq
    lp = -(-length // ROW_ALIGN) * ROW_ALIGN
    h = jnp.concatenate([meta.astype(x.dtype), x, jnp.zeros((lp - length, D_MODEL), x.dtype)], axis=0)

    pos = jnp.arange(lp, dtype=F32)
    inv = ROPE_THETA ** (-jnp.arange(0, DIFF_HEAD, 2, dtype=F32) / DIFF_HEAD)
    ang = pos[:, None] * inv[None, :]
    ang = jnp.concatenate([ang, ang, ang, ang], axis=-1)
    cos, sin = jnp.cos(ang), jnp.sin(ang)

    head_id = jnp.arange(RWKV_WIDTH) // RWKV_HEAD
    seg = (head_id[:, None] == head_id[None, :]).astype(BF16)
    row2 = lambda t: t.reshape(1, -1)

    depth = norm_g.shape[0]
    for i in range(depth):
        g = norm_g[i]
        g0, g1, g2, g3 = (row2(g[n]) for n in range(4))
        w1 = mlp_w1[i].astype(BF16)
        w2 = mlp_w2[i].astype(BF16)
        j = i // 2
        if i % 2 == 0:
            (w_in, mu, w0, w_up, a0, a_up, g_up, k_k, k_a, r_k, ln_w, ln_b, pool_w, pool_scale,
             w_out) = (t[j] for t in ev)
            zeros = jnp.zeros((DECAY_RANK, RWKV_WIDTH), F32)
            lora_w = jnp.concatenate([jnp.concatenate([w_up, zeros], axis=1),
                                      jnp.concatenate([zeros, a_up], axis=1)], axis=0).astype(BF16)
            r, k2, v, na, b, logw, gate, z = _even_in(
                h, g0, w_in.astype(BF16), row2(mu), row2(w0), lora_w, row2(a0), g_up.astype(BF16),
                row2(k_k), row2(k_a), seg, pool_w.astype(BF16), row2(pool_scale))
            o = _rwkv_scan(r, k2, v, na, b, logw)
            h = _even_out(h, o, r, k2, v, gate, z, row2(ln_w), row2(ln_b), row2(r_k), seg,
                          w_out.astype(BF16), g1, g2, g3, w1, w2)
        else:
            w_in, lam_vecs, subln_w, w_out = (t[j] for t in od)
            w_rot = _rotate_half_weights(w_in[:, :2 * D_MODEL])
            q, k, v = _odd_in(h, g0, w_in.astype(BF16), w_rot.astype(BF16), cos, sin)
            lam_init = 0.8 - 0.6 * math.exp(-0.3 * i)
            o = _diff_attn(q, k, v, lam_vecs, row2(subln_w), lam_init)
            h = _odd_out(h, o, w_out.astype(BF16), g1, g2, g3, w1, w2)
    return h[N_META:length]


def kernel(x, meta, norm_g, mlp_w1, mlp_w2, ev_w_in, ev_mu, ev_w0, ev_w_up, ev_a0, ev_a_up, ev_g_up, ev_k_k,
           ev_k_a, ev_r_k, ev_ln_w, ev_ln_b, ev_pool_w, ev_pool_scale, ev_w_out, od_w_in, od_lambda,
           od_subln_w, od_w_out):
    ev = (ev_w_in, ev_mu, ev_w0, ev_w_up, ev_a0, ev_a_up, ev_g_up, ev_k_k, ev_k_a, ev_r_k, ev_ln_w, ev_ln_b,
          ev_pool_w, ev_pool_scale, ev_w_out)
    od = (od_w_in, od_lambda, od_subln_w, od_w_out)
    outs = [_forward(x[bi], meta, norm_g, mlp_w1, mlp_w2, ev, od) for bi in range(x.shape[0])]
    return jnp.stack(outs, axis=0)
```

```python
import functools
import math

import jax
import jax.numpy as jnp
from jax import lax
from jax.experimental import pallas as pl
from jax.experimental.pallas import tpu as pltpu

F32, BF16 = jnp.float32, jnp.bfloat16

D_MODEL = 1024
N_META = 16
RMS_EPS = 1e-6
D_FF = 4 * D_MODEL
RWKV_HEAD = 64
RWKV_WIDTH = D_MODEL // 2
DECAY_RANK = 64
ICLR_RANK = 64
GATE_RANK = 128
GN_EPS = RWKV_HEAD * 1e-5
POOL_WIDTH = D_MODEL - RWKV_WIDTH
POOL_WINDOWS = (2, 4, 8, 16)
POOL_GROUP_W = POOL_WIDTH // len(POOL_WINDOWS)
POOL_CARRY = 16
SHIFT_WIDTH = 3 * RWKV_WIDTH + DECAY_RANK + ICLR_RANK + GATE_RANK
EVEN_IN = SHIFT_WIDTH + POOL_WIDTH
DIFF_HEADS = 8
DIFF_HEAD = D_MODEL // (2 * DIFF_HEADS)
SUBLN_EPS = 1e-5
ROPE_THETA = 10000.0

LANES = 128
HALF = LANES // 2
ROW_ALIGN = 256
CHUNK = 64
SOLVE_BLOCK = 8
ATT_Q_BLOCK = 256
ATT_K_BLOCK = 1024
FF_CHUNK = 1024
NEG_BIG = -1e30
VMEM_LIMIT = 56 * 1024 * 1024


def _pick_tile(n, candidates):
    for c in candidates:
        if n % c == 0:
            return c
    raise ValueError(f"no tile in {candidates} divides {n}")


def _rms(t, g, eps):
    return t * lax.rsqrt(jnp.mean(t * t, axis=-1, keepdims=True) + eps) * g


def _split2(x):
    hi = x.astype(BF16)
    lo = (x - hi.astype(F32)).astype(BF16)
    return hi, lo


def _dot(a, b):
    return jnp.dot(a.astype(BF16), b.astype(BF16), preferred_element_type=F32)


def _dot3(a, b):
    ah, al = _split2(a)
    bh, bl = _split2(b)
    d = functools.partial(jnp.dot, preferred_element_type=F32)
    return d(ah, bh) + (d(ah, bl) + d(al, bh))


def _dot_nt3(a, b):
    ah, al = _split2(a)
    bh, bl = _split2(b)
    d = functools.partial(lax.dot_general, dimension_numbers=(((1,), (1,)), ((), ())),
                          preferred_element_type=F32)
    return d(ah, bh) + (d(ah, bl) + d(al, bh))


def _dot_exact_lhs(a_bf16, b):
    bh, bl = _split2(b)
    d = functools.partial(jnp.dot, preferred_element_type=F32)
    return d(a_bf16, bh) + d(a_bf16, bl)


def _head_sum(x, seg_ref):
    return _dot_exact_lhs_rhs(x, seg_ref[...])


def _dot_exact_lhs_rhs(x, ones_bf16):
    xh, xl = _split2(x)
    d = functools.partial(jnp.dot, preferred_element_type=F32)
    return d(xh, ones_bf16) + d(xl, ones_bf16)


def _sigmoid(x):
    return 1.0 / (1.0 + jnp.exp(-x))


def _softplus(x):
    return jnp.maximum(x, 0.0) + jnp.log(1.0 + jnp.exp(-jnp.abs(x)))


def _mlp_residual(hm, g2, g3, w1_ref, w2_ref):
    n = _rms(hm, g2, RMS_EPS).astype(BF16)
    acc = jnp.zeros(hm.shape, F32)
    for c in range(D_FF // FF_CHUNK):
        cols = slice(c * FF_CHUNK, (c + 1) * FF_CHUNK)
        a = jnp.dot(n, w1_ref[:, cols], preferred_element_type=F32)
        a = jnp.square(jnp.maximum(a, 0.0)).astype(BF16)
        acc = acc + jnp.dot(a, w2_ref[cols, :], preferred_element_type=F32)
    return hm + _rms(acc, g3, RMS_EPS)


def _even_in_kernel(h_ref, g0_ref, win_ref, mu_ref, w0_ref, lora_ref, a0_ref, gup_ref,
                    kk_ref, ka_ref, seg_ref, poolw_ref, pscale_ref,
                    r_out, k_out, v_out, na_out, b_out, lw_out, g_out, z_out,
                    ycarry, ucarry, *, tm):
    i = pl.program_id(0)

    @pl.when(i == 0)
    def _():
        ycarry[...] = jnp.zeros(ycarry.shape, F32)
        ucarry[...] = jnp.zeros(ucarry.shape, F32)

    hn = _rms(h_ref[...], g0_ref[...], RMS_EPS).astype(BF16)
    y = jnp.dot(hn, win_ref[...], preferred_element_type=F32)

    ysh = y[:, :SHIFT_WIDTH]
    row = lax.broadcasted_iota(jnp.int32, (tm, 1), 0)
    prev = jnp.where(row == 0, ycarry[7:8, :], pltpu.roll(ysh, 1, axis=0))
    ycarry[...] = ysh[tm - 8:, :]
    ys = ysh + (prev - ysh) * mu_ref[...]

    rw = RWKV_WIDTH
    r = ys[:, 0:rw]
    k = ys[:, rw:2 * rw]
    v = ys[:, 2 * rw:3 * rw]
    wa = ys[:, 3 * rw:3 * rw + LANES]
    gd = ys[:, 3 * rw + LANES:SHIFT_WIDTH]

    lane = lax.broadcasted_iota(jnp.int32, (1, LANES), 1)
    lora_in = jnp.where(lane < DECAY_RANK, jnp.tanh(wa), wa)
    lora = _dot(lora_in, lora_ref[...])
    wlog = -_softplus(-(w0_ref[...] + lora[:, :rw])) - 0.5
    logw = -jnp.exp(wlog)
    a = _sigmoid(a0_ref[...] + lora[:, rw:])
    g = _dot(_sigmoid(gd), gup_ref[...])

    kk = k * kk_ref[...]
    kk = kk * lax.rsqrt(jnp.maximum(_head_sum(kk * kk, seg_ref), 1e-24))
    k2 = k * (1.0 + (a - 1.0) * ka_ref[...])

    r_out[...] = r
    k_out[...] = k2
    v_out[...] = v
    na_out[...] = -kk
    b_out[...] = kk * a
    lw_out[...] = logw
    g_out[...] = g

    u = y[:, SHIFT_WIDTH:]
    ext = jnp.concatenate([ucarry[...], u], axis=0)
    ucarry[...] = u[tm - POOL_CARRY:, :]
    t_idx = i * tm + row
    for gi, win in enumerate(POOL_WINDOWS):
        cols = slice(gi * POOL_GROUP_W, (gi + 1) * POOL_GROUP_W)
        s = ext[:, cols]
        span = 1
        while span < win:
            s = s + pltpu.roll(s, span, axis=0)
            span *= 2
        cnt = jnp.minimum(t_idx + 1, win).astype(F32)
        d = s[POOL_CARRY:, :] / cnt - u[:, cols]
        z_out[:, cols] = _dot(d, poolw_ref[gi]) * pscale_ref[:, cols]


def _even_in(h, g0, w_in, mu, w0, lora_w, a0, g_up, k_k, k_a, seg, pool_w, pool_scale):
    lp = h.shape[0]
    tm = _pick_tile(lp, (256,))
    rw = RWKV_WIDTH
    row_spec = lambda width: pl.BlockSpec((tm, width), lambda i: (i, 0))
    full = lambda arr: pl.BlockSpec(arr.shape, lambda i: (0,) * arr.ndim)
    out_sds = jax.ShapeDtypeStruct((lp, rw), F32)
    return pl.pallas_call(
        functools.partial(_even_in_kernel, tm=tm),
        grid=(lp // tm,),
        in_specs=[row_spec(D_MODEL), full(g0), full(w_in), full(mu), full(w0), full(lora_w), full(a0),
                  full(g_up), full(k_k), full(k_a), full(seg), full(pool_w), full(pool_scale)],
        out_specs=[row_spec(rw)] * 8,
        out_shape=[out_sds] * 8,
        scratch_shapes=[pltpu.VMEM((8, SHIFT_WIDTH), F32), pltpu.VMEM((POOL_CARRY, POOL_WIDTH), F32)],
        compiler_params=pltpu.CompilerParams(dimension_semantics=("arbitrary",),
                                             vmem_limit_bytes=VMEM_LIMIT),
        name="even_in",
    )(h, g0, w_in, mu, w0, lora_w, a0, g_up, k_k, k_a, seg, pool_w, pool_scale)


def _scan_kernel(r_ref, k_ref, v_ref, na_ref, b_ref, lw_ref, o_ref, h_scr):
    @pl.when(pl.program_id(0) == 0)
    def _():
        h_scr[...] = jnp.zeros(h_scr.shape, F32)

    c = CHUNK
    row = lax.broadcasted_iota(jnp.int32, (c, c), 0)
    col = lax.broadcasted_iota(jnp.int32, (c, c), 1)
    incl = col <= row
    tri = jnp.where(incl, 1.0, 0.0).astype(BF16)

    lw = lw_ref[...]
    lw_hi = lw.astype(BF16)
    lw_r = lw - lw_hi.astype(F32)
    lw_mid = lw_r.astype(BF16)
    lw_lo = (lw_r - lw_mid.astype(F32)).astype(BF16)
    d = functools.partial(jnp.dot, preferred_element_type=F32)
    cum = d(tri, lw_hi) + (d(tri, lw_mid) + d(tri, lw_lo))

    cum_end = cum[c - 1:c, :]
    e_pos = jnp.exp(cum)
    e_neg = jnp.exp(-cum)
    e_prev = jnp.exp(cum - lw)
    e_end = jnp.exp(cum_end - cum)
    p_end = jnp.exp(cum_end)

    r_t = r_ref[...] * e_pos
    a_t = na_ref[...] * e_prev
    b_all = b_ref[...]
    k_all = k_ref[...]
    b_t = b_all * e_neg
    k_t = k_all * e_neg
    b_h = b_all * e_end
    k_h = k_all * e_end
    v_all = v_ref[...]

    lane = lax.broadcasted_iota(jnp.int32, (1, LANES), 1)
    mlo = lane < HALF
    prow = lax.broadcasted_iota(jnp.int32, (LANES, LANES), 0)
    pcol = lax.broadcasted_iota(jnp.int32, (LANES, LANES), 1)
    same_head = (prow < HALF) == (pcol < HALF)
    diag = prow == pcol
    trow = lax.broadcasted_iota(jnp.int32, (c, LANES), 0)
    tcol = lax.broadcasted_iota(jnp.int32, (c, LANES), 1)
    tcol = jnp.where(tcol >= HALF, tcol - HALF, tcol)
    strict2 = tcol < trow
    incl2 = jnp.concatenate([tcol <= trow] * 2, axis=0)
    zeros_c = jnp.zeros((c, LANES), F32)
    n_pairs = RWKV_WIDTH // LANES
    pair_cols = [slice(p * LANES, (p + 1) * LANES) for p in range(n_pairs)]

    a_all = []
    for cols in pair_cols:
        rt, at = r_t[:, cols], a_t[:, cols]
        lhs4 = jnp.concatenate([jnp.where(mlo, at, zeros_c), jnp.where(mlo, zeros_c, at),
                                jnp.where(mlo, rt, zeros_c), jnp.where(mlo, zeros_c, rt)], axis=0)
        a_all.append(_dot_nt3(lhs4, jnp.concatenate([b_t[:, cols], k_t[:, cols]], axis=0)))

    same_blk = (tcol // SOLVE_BLOCK) == (trow // SOLVE_BLOCK)
    lane2 = lax.broadcasted_iota(jnp.int32, (1, 2 * LANES), 1)
    zeros_2c = jnp.zeros((c, 2 * LANES), F32)
    heads = []
    for p, cols in enumerate(pair_cols):
        at, vp = a_t[:, cols], v_all[:, cols]
        at_sw = pltpu.roll(at, HALF, axis=1)
        vp_sw = pltpu.roll(vp, HALF, axis=1)
        for hh in range(2):
            nk = jnp.where(strict2, a_all[p][hh * c:(hh + 1) * c], 0.0)
            av = _dot3(nk, jnp.concatenate([zeros_c, vp_sw if hh == 0 else vp], axis=0))
            x0 = jnp.where(mlo, at if hh == 0 else at_sw, av)
            nk_sw = pltpu.roll(nk, HALF, axis=1)
            n_split = jnp.where(mlo, jnp.where(same_blk, 0.0, nk), jnp.where(same_blk, nk_sw, 0.0))
            heads.append(jnp.concatenate([x0, n_split], axis=1))

    inner_levels = int(math.log2(SOLVE_BLOCK))
    for _ in range(inner_levels):
        nxt = []
        for y in heads:
            prod = _dot3(y[:, LANES:], jnp.concatenate([zeros_2c, y], axis=0))
            nxt.append(jnp.where(lane2 >= LANES + HALF, prod, y + prod))
        heads = nxt
    outer_levels = int(math.log2(c // SOLVE_BLOCK))
    for _ in range(outer_levels):
        nxt = []
        for y in heads:
            prod = _dot3(y[:, LANES:], jnp.concatenate([y, zeros_2c], axis=0))
            nxt.append(jnp.where(lane2 < LANES, y + prod, prod))
        heads = nxt

    big = []
    for p, cols in enumerate(pair_cols):
        x_lo, x_hi = heads[2 * p][:, :LANES], heads[2 * p + 1][:, :LANES]
        w_p = jnp.where(mlo, x_lo, pltpu.roll(x_hi, HALF, axis=1))
        u0_p = jnp.where(mlo, pltpu.roll(x_lo, HALF, axis=1), x_hi)
        rhs = jnp.concatenate([jnp.concatenate([w_p, u0_p], axis=1),
                               jnp.concatenate([zeros_c, v_all[:, cols]], axis=1)], axis=0)
        a_r = jnp.where(incl2, a_all[p][2 * c:], 0.0)
        bk_t = jnp.concatenate([b_h[:, cols], k_h[:, cols]], axis=0).T
        big.append(_dot3(jnp.concatenate([a_r, bk_t], axis=0), rhs))

    for p, cols in enumerate(pair_cols):
        res = big[p]
        q_hat = r_t[:, cols] + jnp.where(mlo, res[:c, :LANES], res[c:2 * c, :LANES])
        o_hat = jnp.where(mlo, res[:c, LANES:], res[c:2 * c, LANES:])
        g_mat = jnp.where(same_head, res[2 * c:, :LANES], 0.0) + jnp.where(diag, p_end[:, cols], 0.0)
        j_mat = jnp.where(same_head, res[2 * c:, LANES:], 0.0)
        st = _dot3(jnp.concatenate([q_hat, g_mat], axis=0), h_scr[p])
        o_ref[:, cols] = st[:c] + o_hat
        h_scr[p] = st[c:] + j_mat


def _rwkv_scan(r, k2, v, na, b, logw):
    lp, rw = r.shape
    spec = pl.BlockSpec((CHUNK, rw), lambda i: (i, 0))
    return pl.pallas_call(
        _scan_kernel,
        grid=(lp // CHUNK,),
        in_specs=[spec] * 6,
        out_specs=spec,
        out_shape=jax.ShapeDtypeStruct((lp, rw), F32),
        scratch_shapes=[pltpu.VMEM((rw // LANES, LANES, LANES), F32)],
        compiler_params=pltpu.CompilerParams(dimension_semantics=("arbitrary",),
                                             vmem_limit_bytes=VMEM_LIMIT),
        name="rwkv_scan",
    )(r, k2, v, na, b, logw)


def _even_out_kernel(h_ref, o_ref, r_ref, k_ref, v_ref, g_ref, z_ref, lnw_ref, lnb_ref, rk_ref, seg_ref,
                     wout_ref, g1_ref, g2_ref, g3_ref, w1_ref, w2_ref, out_ref):
    inv_n = 1.0 / RWKV_HEAD
    o = o_ref[...]
    mean = _head_sum(o, seg_ref) * inv_n
    dev = o - mean
    var = _head_sum(dev * dev, seg_ref) * inv_n
    on = dev * lax.rsqrt(var + GN_EPS) * lnw_ref[...] + lnb_ref[...]
    bonus = _head_sum(r_ref[...] * k_ref[...] * rk_ref[...], seg_ref) * v_ref[...]
    om = (on + bonus) * g_ref[...]
    rw = RWKV_WIDTH
    m = _dot(om, wout_ref[:rw, :]) + _dot(z_ref[...], wout_ref[rw:, :])
    hm = h_ref[...] + _rms(m, g1_ref[...], RMS_EPS)
    out_ref[...] = _mlp_residual(hm, g2_ref[...], g3_ref[...], w1_ref, w2_ref)


def _weight_spec(arr):
    return pl.BlockSpec(arr.shape, lambda i: (0,) * arr.ndim, pipeline_mode=pl.Buffered(1))


def _even_out(h, o, r, k2, v, g, z, ln_w, ln_b, r_k, seg, w_out, g1, g2, g3, w1, w2):
    lp = h.shape[0]
    tm = _pick_tile(lp, (256,))
    row_spec = lambda width: pl.BlockSpec((tm, width), lambda i: (i, 0))
    full = lambda arr: pl.BlockSpec(arr.shape, lambda i: (0,) * arr.ndim)
    rw = RWKV_WIDTH
    return pl.pallas_call(
        _even_out_kernel,
        grid=(lp // tm,),
        in_specs=[row_spec(D_MODEL)] + [row_spec(rw)] * 6 +
                 [full(ln_w), full(ln_b), full(r_k), full(seg), _weight_spec(w_out), full(g1), full(g2),
                  full(g3), _weight_spec(w1), _weight_spec(w2)],
        out_specs=row_spec(D_MODEL),
        out_shape=jax.ShapeDtypeStruct((lp, D_MODEL), F32),
        compiler_params=pltpu.CompilerParams(dimension_semantics=("parallel",),
                                             vmem_limit_bytes=VMEM_LIMIT),
        name="even_out",
    )(h, o, r, k2, v, g, z, ln_w, ln_b, r_k, seg, w_out, g1, g2, g3, w1, w2)


def _odd_in_kernel(h_ref, g0_ref, w_ref, wrot_ref, cos_ref, sin_ref, q_out, k_out, v_out):
    hn = _rms(h_ref[...], g0_ref[...], RMS_EPS).astype(BF16)
    y = jnp.dot(hn, w_ref[...], preferred_element_type=F32)
    yr = jnp.dot(hn, wrot_ref[...], preferred_element_type=F32)
    cos = cos_ref[...]
    sin = sin_ref[...]
    scale = DIFF_HEAD ** -0.5 * math.log2(math.e)
    for j in range(D_MODEL // LANES):
        cols = slice(j * LANES, (j + 1) * LANES)
        kcols = slice(D_MODEL + j * LANES, D_MODEL + (j + 1) * LANES)
        q_out[:, cols] = ((y[:, cols] * cos + yr[:, cols] * sin) * scale).astype(BF16)
        k_out[:, cols] = (y[:, kcols] * cos + yr[:, kcols] * sin).astype(BF16)
    v_out[...] = y[:, 2 * D_MODEL:].astype(BF16)


def _odd_in(h, g0, w, w_rot, cos, sin):
    lp = h.shape[0]
    tm = _pick_tile(lp, (256,))
    row_spec = lambda width: pl.BlockSpec((tm, width), lambda i: (i, 0))
    full = lambda arr: pl.BlockSpec(arr.shape, lambda i: (0,) * arr.ndim)
    sds = jax.ShapeDtypeStruct((lp, D_MODEL), BF16)
    return pl.pallas_call(
        _odd_in_kernel,
        grid=(lp // tm,),
        in_specs=[row_spec(D_MODEL), full(g0), _weight_spec(w), _weight_spec(w_rot),
                  row_spec(LANES), row_spec(LANES)],
        out_specs=[row_spec(D_MODEL)] * 3,
        out_shape=[sds] * 3,
        compiler_params=pltpu.CompilerParams(dimension_semantics=("parallel",),
                                             vmem_limit_bytes=VMEM_LIMIT),
        name="odd_in",
    )(h, g0, w, w_rot, cos, sin)


def _attn_kernel(lam_ref, sw_ref, q_ref, k_ref, v_ref, o_ref, q2_scr, s_scr, smax_scr, p_scr, m_scr, l_scr,
                 acc_scr, *, lam_init, tq, tk):
    i = pl.program_id(1)
    lane = lax.broadcasted_iota(jnp.int32, (1, LANES), 1)
    mlo = lane < HALF
    q = q_ref[...]
    zq = jnp.zeros_like(q)
    q2_scr[...] = jnp.concatenate([jnp.where(mlo, q, zq), jnp.where(mlo, zq, q)], axis=0)
    nt = (((1,), (1,)), ((), ()))

    n_blocks = (i * tq) // tk + 1

    def scores_into(slot, t, diagonal):
        off = pl.multiple_of(t * tk, tk)
        s = lax.dot_general(q2_scr[...], k_ref[pl.ds(off, tk), :], nt, preferred_element_type=F32)
        if diagonal:
            qrow = lax.broadcasted_iota(jnp.int32, (2 * tq, tk), 0)
            qrow = jnp.where(qrow >= tq, qrow - tq, qrow)
            kcol = lax.broadcasted_iota(jnp.int32, (2 * tq, tk), 1)
            s = jnp.where(kcol - qrow <= i * tq - off, s, NEG_BIG)
        s_scr[slot] = s
        smax_scr[slot] = jnp.max(s, axis=-1, keepdims=True)

    def add_weighted_values(slot, t):
        off = pl.multiple_of(t * tk, tk)
        acc_scr[...] += jnp.dot(p_scr[slot], v_ref[pl.ds(off, tk), :], preferred_element_type=F32)

    def tick(t, slot, next_kind):
        if next_kind is not None:
            scores_into(1 - slot, t + 1, next_kind == "diagonal")
        add_weighted_values(1 - slot, jnp.maximum(t - 1, 0))
        m_old = m_scr[...]
        m_new = jnp.maximum(m_old, smax_scr[slot])
        alpha = jnp.exp2(m_old - m_new)
        p = jnp.exp2(s_scr[slot] - m_new)
        l_scr[...] = alpha * l_scr[...] + jnp.sum(p, axis=-1, keepdims=True)
        m_scr[...] = m_new
        p_scr[slot] = p.astype(BF16)
        acc_scr[...] = alpha * acc_scr[...]

    m_scr[...] = jnp.full(m_scr.shape, NEG_BIG, F32)
    l_scr[...] = jnp.zeros(l_scr.shape, F32)
    acc_scr[...] = jnp.zeros(acc_scr.shape, F32)
    p_scr[1] = jnp.zeros(p_scr.shape[1:], BF16)

    @pl.when(n_blocks == 1)
    def _():
        scores_into(0, 0, True)

    @pl.when(n_blocks > 1)
    def _():
        scores_into(0, 0, False)

    def pair(u, carry):
        tick(2 * u, 0, "full")
        tick(2 * u + 1, 1, "full")
        return carry

    n_pairs = jnp.maximum(n_blocks - 2, 0) // 2
    lax.fori_loop(0, n_pairs, pair, 0)
    t0 = 2 * n_pairs

    @pl.when(n_blocks == 1)
    def _():
        tick(t0, 0, None)
        add_weighted_values(0, t0)

    @pl.when(jnp.logical_and(n_blocks > 1, n_blocks % 2 == 0))
    def _():
        tick(t0, 0, "diagonal")
        tick(t0 + 1, 1, None)
        add_weighted_values(1, t0 + 1)

    @pl.when(jnp.logical_and(n_blocks > 1, n_blocks % 2 == 1))
    def _():
        tick(t0, 0, "full")
        tick(t0 + 1, 1, "diagonal")
        tick(t0 + 2, 0, None)
        add_weighted_values(0, t0 + 2)

    lv = lam_ref[...]
    lam = (jnp.exp(jnp.sum(lv[0:1] * lv[1:2], axis=-1, keepdims=True))
           - jnp.exp(jnp.sum(lv[2:3] * lv[3:4], axis=-1, keepdims=True)) + lam_init)
    o = acc_scr[...] / l_scr[...]
    o = o[:tq] - lam * o[tq:]
    o = _rms(o, sw_ref[...], SUBLN_EPS) * (1.0 - lam_init)
    o_ref[...] = o.astype(BF16)


def _diff_attn(q, k, v, lam_vecs, subln_w, lam_init):
    lp = q.shape[0]
    tq, tk = ATT_Q_BLOCK, ATT_K_BLOCK
    lpk = -(-lp // tk) * tk
    k = jnp.pad(k, ((0, lpk - lp), (0, 0)))
    v = jnp.pad(v, ((0, lpk - lp), (0, 0)))
    blk = pl.BlockSpec((tq, LANES), lambda h, i: (i, h))
    head = pl.BlockSpec((lpk, LANES), lambda h, i: (0, h))
    full = lambda arr: pl.BlockSpec(arr.shape, lambda h, i: (0,) * arr.ndim)
    return pl.pallas_call(
        functools.partial(_attn_kernel, lam_init=lam_init, tq=tq, tk=tk),
        grid=(DIFF_HEADS, lp // tq),
        in_specs=[full(lam_vecs), full(subln_w), blk, head, head],
        out_specs=blk,
        out_shape=jax.ShapeDtypeStruct((lp, D_MODEL), BF16),
        scratch_shapes=[pltpu.VMEM((2 * tq, LANES), BF16), pltpu.VMEM((2, 2 * tq, tk), F32),
                        pltpu.VMEM((2, 2 * tq, 1), F32), pltpu.VMEM((2, 2 * tq, tk), BF16),
                        pltpu.VMEM((2 * tq, 1), F32),
                        pltpu.VMEM((2 * tq, 1), F32), pltpu.VMEM((2 * tq, LANES), F32)],
        compiler_params=pltpu.CompilerParams(dimension_semantics=("parallel", "parallel"),
                                             vmem_limit_bytes=VMEM_LIMIT),
        name="diff_attn",
    )(lam_vecs, subln_w, q, k, v)


def _odd_out_kernel(h_ref, o_ref, wout_ref, g1_ref, g2_ref, g3_ref, w1_ref, w2_ref, out_ref):
    m = jnp.dot(o_ref[...], wout_ref[...], preferred_element_type=F32)
    hm = h_ref[...] + _rms(m, g1_ref[...], RMS_EPS)
    out_ref[...] = _mlp_residual(hm, g2_ref[...], g3_ref[...], w1_ref, w2_ref)


def _odd_out(h, o, w_out, g1, g2, g3, w1, w2):
    lp = h.shape[0]
    tm = _pick_tile(lp, (256,))
    row_spec = lambda width: pl.BlockSpec((tm, width), lambda i: (i, 0))
    full = lambda arr: pl.BlockSpec(arr.shape, lambda i: (0,) * arr.ndim)
    return pl.pallas_call(
        _odd_out_kernel,
        grid=(lp // tm,),
        in_specs=[row_spec(D_MODEL), row_spec(D_MODEL), _weight_spec(w_out), full(g1), full(g2), full(g3),
                  _weight_spec(w1), _weight_spec(w2)],
        out_specs=row_spec(D_MODEL),
        out_shape=jax.ShapeDtypeStruct((lp, D_MODEL), F32),
        compiler_params=pltpu.CompilerParams(dimension_semantics=("parallel",),
                                             vmem_limit_bytes=VMEM_LIMIT),
        name="odd_out",
    )(h, o, w_out, g1, g2, g3, w1, w2)


def _rotate_half_weights(w):
    d_in, d_out = w.shape
    w4 = w.reshape(d_in, d_out // DIFF_HEAD, 2, DIFF_HEAD // 2)
    return jnp.concatenate([-w4[:, :, 1:2], w4[:, :, 0:1]], axis=2).reshape(d_in, d_out)


def _forward(x, meta, norm_g, mlp_w1, mlp_w2, ev, od):
    seq = x.shape[0]
    length = N_META + seq
    lp = -(-length // ROW_ALIGN) * ROW_ALIGN
    h = jnp.concatenate([meta.astype(x.dtype), x, jnp.zeros((lp - length, D_MODEL), x.dtype)], axis=0)

    pos = jnp.arange(lp, dtype=F32)
    inv = ROPE_THETA ** (-jnp.arange(0, DIFF_HEAD, 2, dtype=F32) / DIFF_HEAD)
    ang = pos[:, None] * inv[None, :]
    ang = jnp.concatenate([ang, ang, ang, ang], axis=-1)
    cos, sin = jnp.cos(ang), jnp.sin(ang)

    head_id = jnp.arange(RWKV_WIDTH) // RWKV_HEAD
    seg = (head_id[:, None] == head_id[None, :]).astype(BF16)
    row2 = lambda t: t.reshape(1, -1)

    depth = norm_g.shape[0]
    for i in range(depth):
        g = norm_g[i]
        g0, g1, g2, g3 = (row2(g[n]) for n in range(4))
        w1 = mlp_w1[i].astype(BF16)
        w2 = mlp_w2[i].astype(BF16)
        j = i // 2
        if i % 2 == 0:
            (w_in, mu, w0, w_up, a0, a_up, g_up, k_k, k_a, r_k, ln_w, ln_b, pool_w, pool_scale,
             w_out) = (t[j] for t in ev)
            zeros = jnp.zeros((DECAY_RANK, RWKV_WIDTH), F32)
            lora_w = jnp.concatenate([jnp.concatenate([w_up, zeros], axis=1),
                                      jnp.concatenate([zeros, a_up], axis=1)], axis=0).astype(BF16)
            r, k2, v, na, b, logw, gate, z = _even_in(
                h, g0, w_in.astype(BF16), row2(mu), row2(w0), lora_w, row2(a0), g_up.astype(BF16),
                row2(k_k), row2(k_a), seg, pool_w.astype(BF16), row2(pool_scale))
            o = _rwkv_scan(r, k2, v, na, b, logw)
            h = _even_out(h, o, r, k2, v, gate, z, row2(ln_w), row2(ln_b), row2(r_k), seg,
                          w_out.astype(BF16), g1, g2, g3, w1, w2)
        else:
            w_in, lam_vecs, subln_w, w_out = (t[j] for t in od)
            w_rot = _rotate_half_weights(w_in[:, :2 * D_MODEL])
            q, k, v = _odd_in(h, g0, w_in.astype(BF16), w_rot.astype(BF16), cos, sin)
            lam_init = 0.8 - 0.6 * math.exp(-0.3 * i)
            o = _diff_attn(q, k, v, lam_vecs, row2(subln_w), lam_init)
            h = _odd_out(h, o, w_out.astype(BF16), g1, g2, g3, w1, w2)
    return h[N_META:length]


def kernel(x, meta, norm_g, mlp_w1, mlp_w2, ev_w_in, ev_mu, ev_w0, ev_w_up, ev_a0, ev_a_up, ev_g_up, ev_k_k,
           ev_k_a, ev_r_k, ev_ln_w, ev_ln_b, ev_pool_w, ev_pool_scale, ev_w_out, od_w_in, od_lambda,
           od_subln_w, od_w_out):
    ev = (ev_w_in, ev_mu, ev_w0, ev_w_up, ev_a0, ev_a_up, ev_g_up, ev_k_k, ev_k_a, ev_r_k, ev_ln_w, ev_ln_b,
          ev_pool_w, ev_pool_scale, ev_w_out)
    od = (od_w_in, od_lambda, od_subln_w, od_w_out)
    outs = [_forward(x[bi], meta, norm_g, mlp_w1, mlp_w2, ev, od) for bi in range(x.shape[0])]
    return jnp.stack(outs, axis=0)
```

```python
import functools
import math

import jax
import jax.numpy as jnp
from jax import lax
from jax.experimental import pallas as pl
from jax.experimental.pallas import tpu as pltpu

F32, BF16 = jnp.float32, jnp.bfloat16

D_MODEL = 1024
N_META = 16
RMS_EPS = 1e-6
D_FF = 4 * D_MODEL
RWKV_HEAD = 64
RWKV_WIDTH = D_MODEL // 2
DECAY_RANK = 64
ICLR_RANK = 64
GATE_RANK = 128
GN_EPS = RWKV_HEAD * 1e-5
POOL_WIDTH = D_MODEL - RWKV_WIDTH
POOL_WINDOWS = (2, 4, 8, 16)
POOL_GROUP_W = POOL_WIDTH // len(POOL_WINDOWS)
POOL_CARRY = 16
SHIFT_WIDTH = 3 * RWKV_WIDTH + DECAY_RANK + ICLR_RANK + GATE_RANK
EVEN_IN = SHIFT_WIDTH + POOL_WIDTH
DIFF_HEADS = 8
DIFF_HEAD = D_MODEL // (2 * DIFF_HEADS)
SUBLN_EPS = 1e-5
ROPE_THETA = 10000.0

LANES = 128
HALF = LANES // 2
ROW_ALIGN = 256
CHUNK = 64
SOLVE_BLOCK = 8
ATT_Q_BLOCK = 256
ATT_K_BLOCK = 1024
FF_CHUNK = 1024
NEG_BIG = -1e30
VMEM_LIMIT = 56 * 1024 * 1024


def _pick_tile(n, candidates):
    for c in candidates:
        if n % c == 0:
            return c
    raise ValueError(f"no tile in {candidates} divides {n}")


def _rms(t, g, eps):
    return t * lax.rsqrt(jnp.mean(t * t, axis=-1, keepdims=True) + eps) * g


def _split2(x):
    hi = x.astype(BF16)
    lo = (x - hi.astype(F32)).astype(BF16)
    return hi, lo


def _dot(a, b):
    return jnp.dot(a.astype(BF16), b.astype(BF16), preferred_element_type=F32)


def _dot3(a, b):
    ah, al = _split2(a)
    bh, bl = _split2(b)
    d = functools.partial(jnp.dot, preferred_element_type=F32)
    return d(ah, bh) + (d(ah, bl) + d(al, bh))


def _dot_nt3(a, b):
    ah, al = _split2(a)
    bh, bl = _split2(b)
    d = functools.partial(lax.dot_general, dimension_numbers=(((1,), (1,)), ((), ())),
                          preferred_element_type=F32)
    return d(ah, bh) + (d(ah, bl) + d(al, bh))


def _dot_exact_lhs(a_bf16, b):
    bh, bl = _split2(b)
    d = functools.partial(jnp.dot, preferred_element_type=F32)
    return d(a_bf16, bh) + d(a_bf16, bl)


def _head_sum(x, seg_ref):
    return _dot_exact_lhs_rhs(x, seg_ref[...])


def _dot_exact_lhs_rhs(x, ones_bf16):
    xh, xl = _split2(x)
    d = functools.partial(jnp.dot, preferred_element_type=F32)
    return d(xh, ones_bf16) + d(xl, ones_bf16)


def _sigmoid(x):
    return 1.0 / (1.0 + jnp.exp(-x))


def _softplus(x):
    return jnp.maximum(x, 0.0) + jnp.log(1.0 + jnp.exp(-jnp.abs(x)))


def _mlp_residual(hm, g2, g3, w1_ref, w2_ref):
    n = _rms(hm, g2, RMS_EPS).astype(BF16)
    acc = jnp.zeros(hm.shape, F32)
    for c in range(D_FF // FF_CHUNK):
        cols = slice(c * FF_CHUNK, (c + 1) * FF_CHUNK)
        a = jnp.dot(n, w1_ref[:, cols], preferred_element_type=F32)
        a = jnp.square(jnp.maximum(a, 0.0)).astype(BF16)
        acc = acc + jnp.dot(a, w2_ref[cols, :], preferred_element_type=F32)
    return hm + _rms(acc, g3, RMS_EPS)


def _even_in_kernel(h_ref, g0_ref, win_ref, mu_ref, w0_ref, lora_ref, a0_ref, gup_ref,
                    kk_ref, ka_ref, seg_ref, poolw_ref, pscale_ref,
                    r_out, k_out, v_out, na_out, b_out, lw_out, g_out, z_out,
                    ycarry, ucarry, *, tm):
    i = pl.program_id(0)

    @pl.when(i == 0)
    def _():
        ycarry[...] = jnp.zeros(ycarry.shape, F32)
        ucarry[...] = jnp.zeros(ucarry.shape, F32)

    hn = _rms(h_ref[...], g0_ref[...], RMS_EPS).astype(BF16)
    y = jnp.dot(hn, win_ref[...], preferred_element_type=F32)

    ysh = y[:, :SHIFT_WIDTH]
    row = lax.broadcasted_iota(jnp.int32, (tm, 1), 0)
    prev = jnp.where(row == 0, ycarry[7:8, :], pltpu.roll(ysh, 1, axis=0))
    ycarry[...] = ysh[tm - 8:, :]
    ys = ysh + (prev - ysh) * mu_ref[...]

    rw = RWKV_WIDTH
    r = ys[:, 0:rw]
    k = ys[:, rw:2 * rw]
    v = ys[:, 2 * rw:3 * rw]
    wa = ys[:, 3 * rw:3 * rw + LANES]
    gd = ys[:, 3 * rw + LANES:SHIFT_WIDTH]

    lane = lax.broadcasted_iota(jnp.int32, (1, LANES), 1)
    lora_in = jnp.where(lane < DECAY_RANK, jnp.tanh(wa), wa)
    lora = _dot(lora_in, lora_ref[...])
    wlog = -_softplus(-(w0_ref[...] + lora[:, :rw])) - 0.5
    logw = -jnp.exp(wlog)
    a = _sigmoid(a0_ref[...] + lora[:, rw:])
    g = _dot(_sigmoid(gd), gup_ref[...])

    kk = k * kk_ref[...]
    kk = kk * lax.rsqrt(jnp.maximum(_head_sum(kk * kk, seg_ref), 1e-24))
    k2 = k * (1.0 + (a - 1.0) * ka_ref[...])

    r_out[...] = r
    k_out[...] = k2
    v_out[...] = v
    na_out[...] = -kk
    b_out[...] = kk * a
    lw_out[...] = logw
    g_out[...] = g

    u = y[:, SHIFT_WIDTH:]
    ext = jnp.concatenate([ucarry[...], u], axis=0)
    ucarry[...] = u[tm - POOL_CARRY:, :]
    t_idx = i * tm + row
    for gi, win in enumerate(POOL_WINDOWS):
        cols = slice(gi * POOL_GROUP_W, (gi + 1) * POOL_GROUP_W)
        s = ext[:, cols]
        span = 1
        while span < win:
            s = s + pltpu.roll(s, span, axis=0)
            span *= 2
        cnt = jnp.minimum(t_idx + 1, win).astype(F32)
        d = s[POOL_CARRY:, :] / cnt - u[:, cols]
        z_out[:, cols] = _dot(d, poolw_ref[gi]) * pscale_ref[:, cols]


def _even_in(h, g0, w_in, mu, w0, lora_w, a0, g_up, k_k, k_a, seg, pool_w, pool_scale):
    lp = h.shape[0]
    tm = _pick_tile(lp, (256,))
    rw = RWKV_WIDTH
    row_spec = lambda width: pl.BlockSpec((tm, width), lambda i: (i, 0))
    full = lambda arr: pl.BlockSpec(arr.shape, lambda i: (0,) * arr.ndim)
    out_sds = jax.ShapeDtypeStruct((lp, rw), F32)
    return pl.pallas_call(
        functools.partial(_even_in_kernel, tm=tm),
        grid=(lp // tm,),
        in_specs=[row_spec(D_MODEL), full(g0), full(w_in), full(mu), full(w0), full(lora_w), full(a0),
                  full(g_up), full(k_k), full(k_a), full(seg), full(pool_w), full(pool_scale)],
        out_specs=[row_spec(rw)] * 8,
        out_shape=[out_sds] * 8,
        scratch_shapes=[pltpu.VMEM((8, SHIFT_WIDTH), F32), pltpu.VMEM((POOL_CARRY, POOL_WIDTH), F32)],
        compiler_params=pltpu.CompilerParams(dimension_semantics=("arbitrary",),
                                             vmem_limit_bytes=VMEM_LIMIT),
        name="even_in",
    )(h, g0, w_in, mu, w0, lora_w, a0, g_up, k_k, k_a, seg, pool_w, pool_scale)


def _scan_kernel(r_ref, k_ref, v_ref, na_ref, b_ref, lw_ref, o_ref, h_scr):
    @pl.when(pl.program_id(0) == 0)
    def _():
        h_scr[...] = jnp.zeros(h_scr.shape, F32)

    c = CHUNK
    row = lax.broadcasted_iota(jnp.int32, (c, c), 0)
    col = lax.broadcasted_iota(jnp.int32, (c, c), 1)
    incl = col <= row
    tri = jnp.where(incl, 1.0, 0.0).astype(BF16)

    lw = lw_ref[...]
    lw_hi = lw.astype(BF16)
    lw_r = lw - lw_hi.astype(F32)
    lw_mid = lw_r.astype(BF16)
    lw_lo = (lw_r - lw_mid.astype(F32)).astype(BF16)
    d = functools.partial(jnp.dot, preferred_element_type=F32)
    cum = d(tri, lw_hi) + (d(tri, lw_mid) + d(tri, lw_lo))

    cum_end = cum[c - 1:c, :]
    e_pos = jnp.exp(cum)
    e_neg = jnp.exp(-cum)
    e_prev = jnp.exp(cum - lw)
    e_end = jnp.exp(cum_end - cum)
    p_end = jnp.exp(cum_end)

    r_t = r_ref[...] * e_pos
    a_t = na_ref[...] * e_prev
    b_all = b_ref[...]
    k_all = k_ref[...]
    b_t = b_all * e_neg
    k_t = k_all * e_neg
    b_h = b_all * e_end
    k_h = k_all * e_end
    v_all = v_ref[...]

    lane = lax.broadcasted_iota(jnp.int32, (1, LANES), 1)
    mlo = lane < HALF
    prow = lax.broadcasted_iota(jnp.int32, (LANES, LANES), 0)
    pcol = lax.broadcasted_iota(jnp.int32, (LANES, LANES), 1)
    same_head = (prow < HALF) == (pcol < HALF)
    diag = prow == pcol
    trow = lax.broadcasted_iota(jnp.int32, (c, LANES), 0)
    tcol = lax.broadcasted_iota(jnp.int32, (c, LANES), 1)
    tcol = jnp.where(tcol >= HALF, tcol - HALF, tcol)
    strict2 = tcol < trow
    incl2 = jnp.concatenate([tcol <= trow] * 2, axis=0)
    zeros_c = jnp.zeros((c, LANES), F32)
    n_pairs = RWKV_WIDTH // LANES
    pair_cols = [slice(p * LANES, (p + 1) * LANES) for p in range(n_pairs)]

    a_all = []
    for cols in pair_cols:
        rt, at = r_t[:, cols], a_t[:, cols]
        lhs4 = jnp.concatenate([jnp.where(mlo, at, zeros_c), jnp.where(mlo, zeros_c, at),
                                jnp.where(mlo, rt, zeros_c), jnp.where(mlo, zeros_c, rt)], axis=0)
        a_all.append(_dot_nt3(lhs4, jnp.concatenate([b_t[:, cols], k_t[:, cols]], axis=0)))

    same_blk = (tcol // SOLVE_BLOCK) == (trow // SOLVE_BLOCK)
    lane2 = lax.broadcasted_iota(jnp.int32, (1, 2 * LANES), 1)
    zeros_2c = jnp.zeros((c, 2 * LANES), F32)
    heads = []
    for p, cols in enumerate(pair_cols):
        at, vp = a_t[:, cols], v_all[:, cols]
        at_sw = pltpu.roll(at, HALF, axis=1)
        vp_sw = pltpu.roll(vp, HALF, axis=1)
        for hh in range(2):
            nk = jnp.where(strict2, a_all[p][hh * c:(hh + 1) * c], 0.0)
            av = _dot3(nk, jnp.concatenate([zeros_c, vp_sw if hh == 0 else vp], axis=0))
            x0 = jnp.where(mlo, at if hh == 0 else at_sw, av)
            nk_sw = pltpu.roll(nk, HALF, axis=1)
            n_split = jnp.where(mlo, jnp.where(same_blk, 0.0, nk), jnp.where(same_blk, nk_sw, 0.0))
            heads.append(jnp.concatenate([x0, n_split], axis=1))

    inner_levels = int(math.log2(SOLVE_BLOCK))
    for _ in range(inner_levels):
        nxt = []
        for y in heads:
            prod = _dot3(y[:, LANES:], jnp.concatenate([zeros_2c, y], axis=0))
            nxt.append(jnp.where(lane2 >= LANES + HALF, prod, y + prod))
        heads = nxt
    outer_levels = int(math.log2(c // SOLVE_BLOCK))
    for _ in range(outer_levels):
        nxt = []
        for y in heads:
            prod = _dot3(y[:, LANES:], jnp.concatenate([y, zeros_2c], axis=0))
            nxt.append(jnp.where(lane2 < LANES, y + prod, prod))
        heads = nxt

    big = []
    for p, cols in enumerate(pair_cols):
        x_lo, x_hi = heads[2 * p][:, :LANES], heads[2 * p + 1][:, :LANES]
        w_p = jnp.where(mlo, x_lo, pltpu.roll(x_hi, HALF, axis=1))
        u0_p = jnp.where(mlo, pltpu.roll(x_lo, HALF, axis=1), x_hi)
        rhs = jnp.concatenate([jnp.concatenate([w_p, u0_p], axis=1),
                               jnp.concatenate([zeros_c, v_all[:, cols]], axis=1)], axis=0)
        a_r = jnp.where(incl2, a_all[p][2 * c:], 0.0)
        bk_t = jnp.concatenate([b_h[:, cols], k_h[:, cols]], axis=0).T
        big.append(_dot3(jnp.concatenate([a_r, bk_t], axis=0), rhs))

    for p, cols in enumerate(pair_cols):
        res = big[p]
        q_hat = r_t[:, cols] + jnp.where(mlo, res[:c, :LANES], res[c:2 * c, :LANES])
        o_hat = jnp.where(mlo, res[:c, LANES:], res[c:2 * c, LANES:])
        g_mat = jnp.where(same_head, res[2 * c:, :LANES], 0.0) + jnp.where(diag, p_end[:, cols], 0.0)
        j_mat = jnp.where(same_head, res[2 * c:, LANES:], 0.0)
        st = _dot3(jnp.concatenate([q_hat, g_mat], axis=0), h_scr[p])
        o_ref[:, cols] = st[:c] + o_hat
        h_scr[p] = st[c:] + j_mat


def _rwkv_scan(r, k2, v, na, b, logw):
    lp, rw = r.shape
    spec = pl.BlockSpec((CHUNK, rw), lambda i: (i, 0))
    return pl.pallas_call(
        _scan_kernel,
        grid=(lp // CHUNK,),
        in_specs=[spec] * 6,
        out_specs=spec,
        out_shape=jax.ShapeDtypeStruct((lp, rw), F32),
        scratch_shapes=[pltpu.VMEM((rw // LANES, LANES, LANES), F32)],
        compiler_params=pltpu.CompilerParams(dimension_semantics=("arbitrary",),
                                             vmem_limit_bytes=VMEM_LIMIT),
        name="rwkv_scan",
    )(r, k2, v, na, b, logw)


def _even_out_kernel(h_ref, o_ref, r_ref, k_ref, v_ref, g_ref, z_ref, lnw_ref, lnb_ref, rk_ref, seg_ref,
                     wout_ref, g1_ref, g2_ref, g3_ref, w1_ref, w2_ref, out_ref):
    inv_n = 1.0 / RWKV_HEAD
    o = o_ref[...]
    mean = _head_sum(o, seg_ref) * inv_n
    dev = o - mean
    var = _head_sum(dev * dev, seg_ref) * inv_n
    on = dev * lax.rsqrt(var + GN_EPS) * lnw_ref[...] + lnb_ref[...]
    bonus = _head_sum(r_ref[...] * k_ref[...] * rk_ref[...], seg_ref) * v_ref[...]
    om = (on + bonus) * g_ref[...]
    rw = RWKV_WIDTH
    m = _dot(om, wout_ref[:rw, :]) + _dot(z_ref[...], wout_ref[rw:, :])
    hm = h_ref[...] + _rms(m, g1_ref[...], RMS_EPS)
    out_ref[...] = _mlp_residual(hm, g2_ref[...], g3_ref[...], w1_ref, w2_ref)


def _weight_spec(arr):
    return pl.BlockSpec(arr.shape, lambda i: (0,) * arr.ndim, pipeline_mode=pl.Buffered(1))


def _even_out(h, o, r, k2, v, g, z, ln_w, ln_b, r_k, seg, w_out, g1, g2, g3, w1, w2):
    lp = h.shape[0]
    tm = _pick_tile(lp, (256,))
    row_spec = lambda width: pl.BlockSpec((tm, width), lambda i: (i, 0))
    full = lambda arr: pl.BlockSpec(arr.shape, lambda i: (0,) * arr.ndim)
    rw = RWKV_WIDTH
    return pl.pallas_call(
        _even_out_kernel,
        grid=(lp // tm,),
        in_specs=[row_spec(D_MODEL)] + [row_spec(rw)] * 6 +
                 [full(ln_w), full(ln_b), full(r_k), full(seg), _weight_spec(w_out), full(g1), full(g2),
                  full(g3), _weight_spec(w1), _weight_spec(w2)],
        out_specs=row_spec(D_MODEL),
        out_shape=jax.ShapeDtypeStruct((lp, D_MODEL), F32),
        compiler_params=pltpu.CompilerParams(dimension_semantics=("parallel",),
                                             vmem_limit_bytes=VMEM_LIMIT),
        name="even_out",
    )(h, o, r, k2, v, g, z, ln_w, ln_b, r_k, seg, w_out, g1, g2, g3, w1, w2)


def _odd_in_kernel(h_ref, g0_ref, w_ref, wrot_ref, cos_ref, sin_ref, q_out, k_out, v_out):
    hn = _rms(h_ref[...], g0_ref[...], RMS_EPS).astype(BF16)
    y = jnp.dot(hn, w_ref[...], preferred_element_type=F32)
    yr = jnp.dot(hn, wrot_ref[...], preferred_element_type=F32)
    cos = cos_ref[...]
    sin = sin_ref[...]
    scale = DIFF_HEAD ** -0.5 * math.log2(math.e)
    for j in range(D_MODEL // LANES):
        cols = slice(j * LANES, (j + 1) * LANES)
        kcols = slice(D_MODEL + j * LANES, D_MODEL + (j + 1) * LANES)
        q_out[:, cols] = ((y[:, cols] * cos + yr[:, cols] * sin) * scale).astype(BF16)
        k_out[:, cols] = (y[:, kcols] * cos + yr[:, kcols] * sin).astype(BF16)
    ones = jnp.ones((y.shape[0], LANES), BF16)
    for j in range(D_MODEL // LANES):
        v_out[:, 2 * j * LANES:(2 * j + 1) * LANES] = y[:, 2 * D_MODEL + j * LANES:
                                                        2 * D_MODEL + (j + 1) * LANES].astype(BF16)
        v_out[:, (2 * j + 1) * LANES:(2 * j + 2) * LANES] = ones


def _odd_in(h, g0, w, w_rot, cos, sin):
    lp, lpk = h.shape[0], cos.shape[0]
    tm = _pick_tile(lp, (256,))
    last = lp // tm - 1
    row_spec = lambda width: pl.BlockSpec((tm, width), lambda i: (i, 0))
    full = lambda arr: pl.BlockSpec(arr.shape, lambda i: (0,) * arr.ndim)
    sds = jax.ShapeDtypeStruct((lpk, D_MODEL), BF16)
    return pl.pallas_call(
        _odd_in_kernel,
        grid=(lpk // tm,),
        in_specs=[pl.BlockSpec((tm, D_MODEL), lambda i: (jnp.minimum(i, last), 0)), full(g0),
                  _weight_spec(w), _weight_spec(w_rot), row_spec(LANES), row_spec(LANES)],
        out_specs=[row_spec(D_MODEL), row_spec(D_MODEL), row_spec(2 * D_MODEL)],
        out_shape=[sds, sds, jax.ShapeDtypeStruct((lpk, 2 * D_MODEL), BF16)],
        compiler_params=pltpu.CompilerParams(dimension_semantics=("parallel",),
                                             vmem_limit_bytes=VMEM_LIMIT),
        name="odd_in",
    )(h, g0, w, w_rot, cos, sin)


def _attn_kernel(lam_ref, sw_ref, q_ref, k_ref, v_ref, o_ref, q2_scr, s_scr, smax_scr, p_scr, m_scr, acc_scr,
                 *, lam_init, tq, tk):
    i = pl.program_id(1)
    lane = lax.broadcasted_iota(jnp.int32, (1, LANES), 1)
    mlo = lane < HALF
    q = q_ref[...]
    zq = jnp.zeros_like(q)
    q2_scr[...] = jnp.concatenate([jnp.where(mlo, q, zq), jnp.where(mlo, zq, q)], axis=0)
    nt = (((1,), (1,)), ((), ()))

    n_blocks = (i * tq) // tk + 1

    def scores_into(slot, t, diagonal):
        off = pl.multiple_of(t * tk, tk)
        s = lax.dot_general(q2_scr[...], k_ref[pl.ds(off, tk), :], nt, preferred_element_type=F32)
        if diagonal:
            qrow = lax.broadcasted_iota(jnp.int32, (2 * tq, tk), 0)
            qrow = jnp.where(qrow >= tq, qrow - tq, qrow)
            kcol = lax.broadcasted_iota(jnp.int32, (2 * tq, tk), 1)
            s = jnp.where(kcol - qrow <= i * tq - off, s, NEG_BIG)
        s_scr[slot] = s
        smax_scr[slot] = jnp.max(s_scr[slot], axis=-1, keepdims=True)

    def add_weighted_values(slot, t):
        off = pl.multiple_of(t * tk, tk)
        acc_scr[...] += jnp.dot(p_scr[slot], v_ref[pl.ds(off, tk), :], preferred_element_type=F32)

    def tick(t, slot, next_kind):
        if next_kind is not None:
            scores_into(1 - slot, t + 1, next_kind == "diagonal")
        add_weighted_values(1 - slot, jnp.maximum(t - 1, 0))
        m_old = m_scr[...]
        m_new = jnp.maximum(m_old, smax_scr[slot])
        m_scr[...] = m_new
        p_scr[slot] = jnp.exp2((s_scr[slot] - m_new).astype(BF16))
        acc_scr[...] = jnp.exp2(m_old - m_new) * acc_scr[...]

    m_scr[...] = jnp.full(m_scr.shape, NEG_BIG, F32)
    acc_scr[...] = jnp.zeros(acc_scr.shape, F32)
    p_scr[1] = jnp.zeros(p_scr.shape[1:], BF16)

    @pl.when(n_blocks == 1)
    def _():
        scores_into(0, 0, True)

    @pl.when(n_blocks > 1)
    def _():
        scores_into(0, 0, False)

    def pair(u, carry):
        tick(2 * u, 0, "full")
        tick(2 * u + 1, 1, "full")
        return carry

    n_pairs = jnp.maximum(n_blocks - 2, 0) // 2
    lax.fori_loop(0, n_pairs, pair, 0)
    t0 = 2 * n_pairs

    @pl.when(n_blocks == 1)
    def _():
        tick(t0, 0, None)
        add_weighted_values(0, t0)

    @pl.when(jnp.logical_and(n_blocks > 1, n_blocks % 2 == 0))
    def _():
        tick(t0, 0, "diagonal")
        tick(t0 + 1, 1, None)
        add_weighted_values(1, t0 + 1)

    @pl.when(jnp.logical_and(n_blocks > 1, n_blocks % 2 == 1))
    def _():
        tick(t0, 0, "full")
        tick(t0 + 1, 1, "diagonal")
        tick(t0 + 2, 0, None)
        add_weighted_values(0, t0 + 2)

    lv = lam_ref[...]
    lam = (jnp.exp(jnp.sum(lv[0:1] * lv[1:2], axis=-1, keepdims=True))
           - jnp.exp(jnp.sum(lv[2:3] * lv[3:4], axis=-1, keepdims=True)) + lam_init)
    o = acc_scr[:, :LANES] / acc_scr[:, LANES:]
    o = o[:tq] - lam * o[tq:]
    o = _rms(o, sw_ref[...], SUBLN_EPS) * (1.0 - lam_init)
    o_ref[...] = o.astype(BF16)


def _diff_attn(q, k, v, lam_vecs, subln_w, lam_init, lp):
    tq, tk = ATT_Q_BLOCK, ATT_K_BLOCK
    lpk = k.shape[0]
    blk = pl.BlockSpec((tq, LANES), lambda h, i: (i, h))
    head = pl.BlockSpec((lpk, LANES), lambda h, i: (0, h))
    head_v = pl.BlockSpec((lpk, 2 * LANES), lambda h, i: (0, h))
    full = lambda arr: pl.BlockSpec(arr.shape, lambda h, i: (0,) * arr.ndim)
    return pl.pallas_call(
        functools.partial(_attn_kernel, lam_init=lam_init, tq=tq, tk=tk),
        grid=(DIFF_HEADS, lp // tq),
        in_specs=[full(lam_vecs), full(subln_w), blk, head, head_v],
        out_specs=blk,
        out_shape=jax.ShapeDtypeStruct((lp, D_MODEL), BF16),
        scratch_shapes=[pltpu.VMEM((2 * tq, LANES), BF16), pltpu.VMEM((2, 2 * tq, tk), F32),
                        pltpu.VMEM((2, 2 * tq, 1), F32), pltpu.VMEM((2, 2 * tq, tk), BF16),
                        pltpu.VMEM((2 * tq, 1), F32), pltpu.VMEM((2 * tq, 2 * LANES), F32)],
        compiler_params=pltpu.CompilerParams(dimension_semantics=("parallel", "parallel"),
                                             vmem_limit_bytes=VMEM_LIMIT),
        name="diff_attn",
    )(lam_vecs, subln_w, q, k, v)


def _odd_out_kernel(h_ref, o_ref, wout_ref, g1_ref, g2_ref, g3_ref, w1_ref, w2_ref, out_ref):
    m = jnp.dot(o_ref[...], wout_ref[...], preferred_element_type=F32)
    hm = h_ref[...] + _rms(m, g1_ref[...], RMS_EPS)
    out_ref[...] = _mlp_residual(hm, g2_ref[...], g3_ref[...], w1_ref, w2_ref)


def _odd_out(h, o, w_out, g1, g2, g3, w1, w2):
    lp = h.shape[0]
    tm = _pick_tile(lp, (256,))
    row_spec = lambda width: pl.BlockSpec((tm, width), lambda i: (i, 0))
    full = lambda arr: pl.BlockSpec(arr.shape, lambda i: (0,) * arr.ndim)
    return pl.pallas_call(
        _odd_out_kernel,
        grid=(lp // tm,),
        in_specs=[row_spec(D_MODEL), row_spec(D_MODEL), _weight_spec(w_out), full(g1), full(g2), full(g3),
                  _weight_spec(w1), _weight_spec(w2)],
        out_specs=row_spec(D_MODEL),
        out_shape=jax.ShapeDtypeStruct((lp, D_MODEL), F32),
        compiler_params=pltpu.CompilerParams(dimension_semantics=("parallel",),
                                             vmem_limit_bytes=VMEM_LIMIT),
        name="odd_out",
    )(h, o, w_out, g1, g2, g3, w1, w2)


def _rotate_half_weights(w):
    d_in, d_out = w.shape
    w4 = w.reshape(d_in, d_out // DIFF_HEAD, 2, DIFF_HEAD // 2)
    return jnp.concatenate([-w4[:, :, 1:2], w4[:, :, 0:1]], axis=2).reshape(d_in, d_out)


def _forward(x, meta, norm_g, mlp_w1, mlp_w2, ev, od):
    seq = x.shape[0]
    length = N_META + seq
    lp = -(-length // ROW_ALIGN) * ROW_ALIGN
    h = jnp.concatenate([meta.astype(x.dtype), x, jnp.zeros((lp - length, D_MODEL), x.dtype)], axis=0)

    lpk = -(-lp // ATT_K_BLOCK) * ATT_K_BLOCK
    pos = jnp.arange(lpk, dtype=F32)
    inv = ROPE_THETA ** (-jnp.arange(0, DIFF_HEAD, 2, dtype=F32) / DIFF_HEAD)
    ang = pos[:, None] * inv[None, :]
    ang = jnp.concatenate([ang, ang, ang, ang], axis=-1)
    cos, sin = jnp.cos(ang), jnp.sin(ang)

    head_id = jnp.arange(RWKV_WIDTH) // RWKV_HEAD
    seg = (head_id[:, None] == head_id[None, :]).astype(BF16)
    row2 = lambda t: t.reshape(1, -1)

    depth = norm_g.shape[0]
    for i in range(depth):
        g = norm_g[i]
        g0, g1, g2, g3 = (row2(g[n]) for n in range(4))
        w1 = mlp_w1[i].astype(BF16)
        w2 = mlp_w2[i].astype(BF16)
        j = i // 2
        if i % 2 == 0:
            (w_in, mu, w0, w_up, a0, a_up, g_up, k_k, k_a, r_k, ln_w, ln_b, pool_w, pool_scale,
             w_out) = (t[j] for t in ev)
            zeros = jnp.zeros((DECAY_RANK, RWKV_WIDTH), F32)
            lora_w = jnp.concatenate([jnp.concatenate([w_up, zeros], axis=1),
                                      jnp.concatenate([zeros, a_up], axis=1)], axis=0).astype(BF16)
            r, k2, v, na, b, logw, gate, z = _even_in(
                h, g0, w_in.astype(BF16), row2(mu), row2(w0), lora_w, row2(a0), g_up.astype(BF16),
                row2(k_k), row2(k_a), seg, pool_w.astype(BF16), row2(pool_scale))
            o = _rwkv_scan(r, k2, v, na, b, logw)
            h = _even_out(h, o, r, k2, v, gate, z, row2(ln_w), row2(ln_b), row2(r_k), seg,
                          w_out.astype(BF16), g1, g2, g3, w1, w2)
        else:
            w_in, lam_vecs, subln_w, w_out = (t[j] for t in od)
            w_rot = _rotate_half_weights(w_in[:, :2 * D_MODEL])
            q, k, v = _odd_in(h, g0, w_in.astype(BF16), w_rot.astype(BF16), cos, sin)
            lam_init = 0.8 - 0.6 * math.exp(-0.3 * i)
            o = _diff_attn(q, k, v, lam_vecs, row2(subln_w), lam_init, lp)
            h = _odd_out(h, o, w_out.astype(BF16), g1, g2, g3, w1, w2)
    return h[N_META:length]


def kernel(x, meta, norm_g, mlp_w1, mlp_w2, ev_w_in, ev_mu, ev_w0, ev_w_up, ev_a0, ev_a_up, ev_g_up, ev_k_k,
           ev_k_a, ev_r_k, ev_ln_w, ev_ln_b, ev_pool_w, ev_pool_scale, ev_w_out, od_w_in, od_lambda,
           od_subln_w, od_w_out):
    ev = (ev_w_in, ev_mu, ev_w0, ev_w_up, ev_a0, ev_a_up, ev_g_up, ev_k_k, ev_k_a, ev_r_k, ev_ln_w, ev_ln_b,
          ev_pool_w, ev_pool_scale, ev_w_out)
    od = (od_w_in, od_lambda, od_subln_w, od_w_out)
    outs = [_forward(x[bi], meta, norm_g, mlp_w1, mlp_w2, ev, od) for bi in range(x.shape[0])]
    return jnp.stack(outs, axis=0)
```

```python
import functools
import math

import jax
import jax.numpy as jnp
from jax import lax
from jax.experimental import pallas as pl
from jax.experimental.pallas import tpu as pltpu

F32, BF16 = jnp.float32, jnp.bfloat16

D_MODEL = 1024
N_META = 16
RMS_EPS = 1e-6
D_FF = 4 * D_MODEL
RWKV_HEAD = 64
RWKV_WIDTH = D_MODEL // 2
DECAY_RANK = 64
ICLR_RANK = 64
GATE_RANK = 128
GN_EPS = RWKV_HEAD * 1e-5
POOL_WIDTH = D_MODEL - RWKV_WIDTH
POOL_WINDOWS = (2, 4, 8, 16)
POOL_GROUP_W = POOL_WIDTH // len(POOL_WINDOWS)
POOL_CARRY = 16
SHIFT_WIDTH = 3 * RWKV_WIDTH + DECAY_RANK + ICLR_RANK + GATE_RANK
EVEN_IN = SHIFT_WIDTH + POOL_WIDTH
DIFF_HEADS = 8
DIFF_HEAD = D_MODEL // (2 * DIFF_HEADS)
SUBLN_EPS = 1e-5
ROPE_THETA = 10000.0

LANES = 128
HALF = LANES // 2
ROW_ALIGN = 256
CHUNK = 64
SOLVE_BLOCK = 8
ATT_Q_BLOCK = 256
ATT_K_BLOCK = 1024
ATT_HEADS = 2
FF_CHUNK = 1024
NEG_BIG = -1e30
VMEM_LIMIT = 56 * 1024 * 1024


def _pick_tile(n, candidates):
    for c in candidates:
        if n % c == 0:
            return c
    raise ValueError(f"no tile in {candidates} divides {n}")


def _rms(t, g, eps):
    return t * lax.rsqrt(jnp.mean(t * t, axis=-1, keepdims=True) + eps) * g


def _split2(x):
    hi = x.astype(BF16)
    lo = (x - hi.astype(F32)).astype(BF16)
    return hi, lo


def _dot(a, b):
    return jnp.dot(a.astype(BF16), b.astype(BF16), preferred_element_type=F32)


def _dot3(a, b):
    ah, al = _split2(a)
    bh, bl = _split2(b)
    d = functools.partial(jnp.dot, preferred_element_type=F32)
    return d(ah, bh) + (d(ah, bl) + d(al, bh))


def _dot_nt3(a, b):
    ah, al = _split2(a)
    bh, bl = _split2(b)
    d = functools.partial(lax.dot_general, dimension_numbers=(((1,), (1,)), ((), ())),
                          preferred_element_type=F32)
    return d(ah, bh) + (d(ah, bl) + d(al, bh))


def _dot_exact_lhs(a_bf16, b):
    bh, bl = _split2(b)
    d = functools.partial(jnp.dot, preferred_element_type=F32)
    return d(a_bf16, bh) + d(a_bf16, bl)


def _head_sum(x, seg_ref):
    return _dot_exact_lhs_rhs(x, seg_ref[...])


def _dot_exact_lhs_rhs(x, ones_bf16):
    xh, xl = _split2(x)
    d = functools.partial(jnp.dot, preferred_element_type=F32)
    return d(xh, ones_bf16) + d(xl, ones_bf16)


def _sigmoid(x):
    return 1.0 / (1.0 + jnp.exp(-x))


def _softplus(x):
    return jnp.maximum(x, 0.0) + jnp.log(1.0 + jnp.exp(-jnp.abs(x)))


def _mlp_residual(hm, g2, g3, w1_ref, w2_ref):
    n = _rms(hm, g2, RMS_EPS).astype(BF16)
    acc = jnp.zeros(hm.shape, F32)
    for c in range(D_FF // FF_CHUNK):
        cols = slice(c * FF_CHUNK, (c + 1) * FF_CHUNK)
        a = jnp.dot(n, w1_ref[:, cols], preferred_element_type=F32)
        a = jnp.square(jnp.maximum(a, 0.0)).astype(BF16)
        acc = acc + jnp.dot(a, w2_ref[cols, :], preferred_element_type=F32)
    return hm + _rms(acc, g3, RMS_EPS)


def _even_in_kernel(h_ref, g0_ref, win_ref, mu_ref, w0_ref, lora_ref, a0_ref, gup_ref,
                    kk_ref, ka_ref, seg_ref, poolw_ref, pscale_ref,
                    r_out, k_out, v_out, na_out, b_out, lw_out, g_out, z_out,
                    ycarry, ucarry, *, tm):
    i = pl.program_id(0)

    @pl.when(i == 0)
    def _():
        ycarry[...] = jnp.zeros(ycarry.shape, F32)
        ucarry[...] = jnp.zeros(ucarry.shape, F32)

    hn = _rms(h_ref[...], g0_ref[...], RMS_EPS).astype(BF16)
    y = jnp.dot(hn, win_ref[...], preferred_element_type=F32)

    ysh = y[:, :SHIFT_WIDTH]
    row = lax.broadcasted_iota(jnp.int32, (tm, 1), 0)
    prev = jnp.where(row == 0, ycarry[7:8, :], pltpu.roll(ysh, 1, axis=0))
    ycarry[...] = ysh[tm - 8:, :]
    ys = ysh + (prev - ysh) * mu_ref[...]

    rw = RWKV_WIDTH
    r = ys[:, 0:rw]
    k = ys[:, rw:2 * rw]
    v = ys[:, 2 * rw:3 * rw]
    wa = ys[:, 3 * rw:3 * rw + LANES]
    gd = ys[:, 3 * rw + LANES:SHIFT_WIDTH]

    lane = lax.broadcasted_iota(jnp.int32, (1, LANES), 1)
    lora_in = jnp.where(lane < DECAY_RANK, jnp.tanh(wa), wa)
    lora = _dot(lora_in, lora_ref[...])
    wlog = -_softplus(-(w0_ref[...] + lora[:, :rw])) - 0.5
    logw = -jnp.exp(wlog)
    a = _sigmoid(a0_ref[...] + lora[:, rw:])
    g = _dot(_sigmoid(gd), gup_ref[...])

    kk = k * kk_ref[...]
    kk = kk * lax.rsqrt(jnp.maximum(_head_sum(kk * kk, seg_ref), 1e-24))
    k2 = k * (1.0 + (a - 1.0) * ka_ref[...])

    r_out[...] = r
    k_out[...] = k2
    v_out[...] = v
    na_out[...] = -kk
    b_out[...] = kk * a
    lw_out[...] = logw
    g_out[...] = g

    u = y[:, SHIFT_WIDTH:]
    ext = jnp.concatenate([ucarry[...], u], axis=0)
    ucarry[...] = u[tm - POOL_CARRY:, :]
    t_idx = i * tm + row
    for gi, win in enumerate(POOL_WINDOWS):
        cols = slice(gi * POOL_GROUP_W, (gi + 1) * POOL_GROUP_W)
        s = ext[:, cols]
        span = 1
        while span < win:
            s = s + pltpu.roll(s, span, axis=0)
            span *= 2
        cnt = jnp.minimum(t_idx + 1, win).astype(F32)
        d = s[POOL_CARRY:, :] / cnt - u[:, cols]
        z_out[:, cols] = _dot(d, poolw_ref[gi]) * pscale_ref[:, cols]


def _even_in(h, g0, w_in, mu, w0, lora_w, a0, g_up, k_k, k_a, seg, pool_w, pool_scale):
    lp = h.shape[0]
    tm = _pick_tile(lp, (256,))
    rw = RWKV_WIDTH
    row_spec = lambda width: pl.BlockSpec((tm, width), lambda i: (i, 0))
    full = lambda arr: pl.BlockSpec(arr.shape, lambda i: (0,) * arr.ndim)
    out_sds = jax.ShapeDtypeStruct((lp, rw), F32)
    return pl.pallas_call(
        functools.partial(_even_in_kernel, tm=tm),
        grid=(lp // tm,),
        in_specs=[row_spec(D_MODEL), full(g0), full(w_in), full(mu), full(w0), full(lora_w), full(a0),
                  full(g_up), full(k_k), full(k_a), full(seg), full(pool_w), full(pool_scale)],
        out_specs=[row_spec(rw)] * 8,
        out_shape=[out_sds] * 8,
        scratch_shapes=[pltpu.VMEM((8, SHIFT_WIDTH), F32), pltpu.VMEM((POOL_CARRY, POOL_WIDTH), F32)],
        compiler_params=pltpu.CompilerParams(dimension_semantics=("arbitrary",),
                                             vmem_limit_bytes=VMEM_LIMIT),
        name="even_in",
    )(h, g0, w_in, mu, w0, lora_w, a0, g_up, k_k, k_a, seg, pool_w, pool_scale)


def _scan_kernel(r_ref, k_ref, v_ref, na_ref, b_ref, lw_ref, o_ref, h_scr):
    @pl.when(pl.program_id(0) == 0)
    def _():
        h_scr[...] = jnp.zeros(h_scr.shape, F32)

    c = CHUNK
    row = lax.broadcasted_iota(jnp.int32, (c, c), 0)
    col = lax.broadcasted_iota(jnp.int32, (c, c), 1)
    incl = col <= row
    tri = jnp.where(incl, 1.0, 0.0).astype(BF16)

    lw = lw_ref[...]
    lw_hi = lw.astype(BF16)
    lw_r = lw - lw_hi.astype(F32)
    lw_mid = lw_r.astype(BF16)
    lw_lo = (lw_r - lw_mid.astype(F32)).astype(BF16)
    d = functools.partial(jnp.dot, preferred_element_type=F32)
    cum = d(tri, lw_hi) + (d(tri, lw_mid) + d(tri, lw_lo))

    cum_end = cum[c - 1:c, :]
    e_pos = jnp.exp(cum)
    e_neg = jnp.exp(-cum)
    e_prev = jnp.exp(cum - lw)
    e_end = jnp.exp(cum_end - cum)
    p_end = jnp.exp(cum_end)

    r_t = r_ref[...] * e_pos
    a_t = na_ref[...] * e_prev
    b_all = b_ref[...]
    k_all = k_ref[...]
    b_t = b_all * e_neg
    k_t = k_all * e_neg
    b_h = b_all * e_end
    k_h = k_all * e_end
    v_all = v_ref[...]

    lane = lax.broadcasted_iota(jnp.int32, (1, LANES), 1)
    mlo = lane < HALF
    prow = lax.broadcasted_iota(jnp.int32, (LANES, LANES), 0)
    pcol = lax.broadcasted_iota(jnp.int32, (LANES, LANES), 1)
    same_head = (prow < HALF) == (pcol < HALF)
    diag = prow == pcol
    trow = lax.broadcasted_iota(jnp.int32, (c, LANES), 0)
    tcol = lax.broadcasted_iota(jnp.int32, (c, LANES), 1)
    tcol = jnp.where(tcol >= HALF, tcol - HALF, tcol)
    strict2 = tcol < trow
    incl2 = jnp.concatenate([tcol <= trow] * 2, axis=0)
    zeros_c = jnp.zeros((c, LANES), F32)
    n_pairs = RWKV_WIDTH // LANES
    pair_cols = [slice(p * LANES, (p + 1) * LANES) for p in range(n_pairs)]

    a_all = []
    for cols in pair_cols:
        rt, at = r_t[:, cols], a_t[:, cols]
        lhs4 = jnp.concatenate([jnp.where(mlo, at, zeros_c), jnp.where(mlo, zeros_c, at),
                                jnp.where(mlo, rt, zeros_c), jnp.where(mlo, zeros_c, rt)], axis=0)
        a_all.append(_dot_nt3(lhs4, jnp.concatenate([b_t[:, cols], k_t[:, cols]], axis=0)))

    same_blk = (tcol // SOLVE_BLOCK) == (trow // SOLVE_BLOCK)
    lane2 = lax.broadcasted_iota(jnp.int32, (1, 2 * LANES), 1)
    zeros_2c = jnp.zeros((c, 2 * LANES), F32)
    heads = []
    for p, cols in enumerate(pair_cols):
        at, vp = a_t[:, cols], v_all[:, cols]
        at_sw = pltpu.roll(at, HALF, axis=1)
        vp_sw = pltpu.roll(vp, HALF, axis=1)
        for hh in range(2):
            nk = jnp.where(strict2, a_all[p][hh * c:(hh + 1) * c], 0.0)
            av = _dot3(nk, jnp.concatenate([zeros_c, vp_sw if hh == 0 else vp], axis=0))
            x0 = jnp.where(mlo, at if hh == 0 else at_sw, av)
            nk_sw = pltpu.roll(nk, HALF, axis=1)
            n_split = jnp.where(mlo, jnp.where(same_blk, 0.0, nk), jnp.where(same_blk, nk_sw, 0.0))
            heads.append(jnp.concatenate([x0, n_split], axis=1))

    inner_levels = int(math.log2(SOLVE_BLOCK))
    for _ in range(inner_levels):
        nxt = []
        for y in heads:
            prod = _dot3(y[:, LANES:], jnp.concatenate([zeros_2c, y], axis=0))
            nxt.append(jnp.where(lane2 >= LANES + HALF, prod, y + prod))
        heads = nxt
    outer_levels = int(math.log2(c // SOLVE_BLOCK))
    for _ in range(outer_levels):
        nxt = []
        for y in heads:
            prod = _dot3(y[:, LANES:], jnp.concatenate([y, zeros_2c], axis=0))
            nxt.append(jnp.where(lane2 < LANES, y + prod, prod))
        heads = nxt

    big = []
    for p, cols in enumerate(pair_cols):
        x_lo, x_hi = heads[2 * p][:, :LANES], heads[2 * p + 1][:, :LANES]
        w_p = jnp.where(mlo, x_lo, pltpu.roll(x_hi, HALF, axis=1))
        u0_p = jnp.where(mlo, pltpu.roll(x_lo, HALF, axis=1), x_hi)
        rhs = jnp.concatenate([jnp.concatenate([w_p, u0_p], axis=1),
                               jnp.concatenate([zeros_c, v_all[:, cols]], axis=1)], axis=0)
        a_r = jnp.where(incl2, a_all[p][2 * c:], 0.0)
        bk_t = jnp.concatenate([b_h[:, cols], k_h[:, cols]], axis=0).T
        big.append(_dot3(jnp.concatenate([a_r, bk_t], axis=0), rhs))

    for p, cols in enumerate(pair_cols):
        res = big[p]
        q_hat = r_t[:, cols] + jnp.where(mlo, res[:c, :LANES], res[c:2 * c, :LANES])
        o_hat = jnp.where(mlo, res[:c, LANES:], res[c:2 * c, LANES:])
        g_mat = jnp.where(same_head, res[2 * c:, :LANES], 0.0) + jnp.where(diag, p_end[:, cols], 0.0)
        j_mat = jnp.where(same_head, res[2 * c:, LANES:], 0.0)
        st = _dot3(jnp.concatenate([q_hat, g_mat], axis=0), h_scr[p])
        o_ref[:, cols] = st[:c] + o_hat
        h_scr[p] = st[c:] + j_mat


def _rwkv_scan(r, k2, v, na, b, logw):
    lp, rw = r.shape
    spec = pl.BlockSpec((CHUNK, rw), lambda i: (i, 0))
    return pl.pallas_call(
        _scan_kernel,
        grid=(lp // CHUNK,),
        in_specs=[spec] * 6,
        out_specs=spec,
        out_shape=jax.ShapeDtypeStruct((lp, rw), F32),
        scratch_shapes=[pltpu.VMEM((rw // LANES, LANES, LANES), F32)],
        compiler_params=pltpu.CompilerParams(dimension_semantics=("arbitrary",),
                                             vmem_limit_bytes=VMEM_LIMIT),
        name="rwkv_scan",
    )(r, k2, v, na, b, logw)


def _even_out_kernel(h_ref, o_ref, r_ref, k_ref, v_ref, g_ref, z_ref, lnw_ref, lnb_ref, rk_ref, seg_ref,
                     wout_ref, g1_ref, g2_ref, g3_ref, w1_ref, w2_ref, out_ref):
    inv_n = 1.0 / RWKV_HEAD
    o = o_ref[...]
    mean = _head_sum(o, seg_ref) * inv_n
    dev = o - mean
    var = _head_sum(dev * dev, seg_ref) * inv_n
    on = dev * lax.rsqrt(var + GN_EPS) * lnw_ref[...] + lnb_ref[...]
    bonus = _head_sum(r_ref[...] * k_ref[...] * rk_ref[...], seg_ref) * v_ref[...]
    om = (on + bonus) * g_ref[...]
    rw = RWKV_WIDTH
    m = _dot(om, wout_ref[:rw, :]) + _dot(z_ref[...], wout_ref[rw:, :])
    hm = h_ref[...] + _rms(m, g1_ref[...], RMS_EPS)
    out_ref[...] = _mlp_residual(hm, g2_ref[...], g3_ref[...], w1_ref, w2_ref)


def _weight_spec(arr):
    return pl.BlockSpec(arr.shape, lambda i: (0,) * arr.ndim, pipeline_mode=pl.Buffered(1))


def _even_out(h, o, r, k2, v, g, z, ln_w, ln_b, r_k, seg, w_out, g1, g2, g3, w1, w2):
    lp = h.shape[0]
    tm = _pick_tile(lp, (256,))
    row_spec = lambda width: pl.BlockSpec((tm, width), lambda i: (i, 0))
    full = lambda arr: pl.BlockSpec(arr.shape, lambda i: (0,) * arr.ndim)
    rw = RWKV_WIDTH
    return pl.pallas_call(
        _even_out_kernel,
        grid=(lp // tm,),
        in_specs=[row_spec(D_MODEL)] + [row_spec(rw)] * 6 +
                 [full(ln_w), full(ln_b), full(r_k), full(seg), _weight_spec(w_out), full(g1), full(g2),
                  full(g3), _weight_spec(w1), _weight_spec(w2)],
        out_specs=row_spec(D_MODEL),
        out_shape=jax.ShapeDtypeStruct((lp, D_MODEL), F32),
        compiler_params=pltpu.CompilerParams(dimension_semantics=("parallel",),
                                             vmem_limit_bytes=VMEM_LIMIT),
        name="even_out",
    )(h, o, r, k2, v, g, z, ln_w, ln_b, r_k, seg, w_out, g1, g2, g3, w1, w2)


def _odd_in_kernel(h_ref, g0_ref, w_ref, wrot_ref, cos_ref, sin_ref, q_out, k_out, v_out):
    hn = _rms(h_ref[...], g0_ref[...], RMS_EPS).astype(BF16)
    y = jnp.dot(hn, w_ref[...], preferred_element_type=F32)
    yr = jnp.dot(hn, wrot_ref[...], preferred_element_type=F32)
    cos = cos_ref[...]
    sin = sin_ref[...]
    scale = DIFF_HEAD ** -0.5 * math.log2(math.e)
    for j in range(D_MODEL // LANES):
        cols = slice(j * LANES, (j + 1) * LANES)
        kcols = slice(D_MODEL + j * LANES, D_MODEL + (j + 1) * LANES)
        q_out[:, cols] = ((y[:, cols] * cos + yr[:, cols] * sin) * scale).astype(BF16)
        k_out[:, cols] = (y[:, kcols] * cos + yr[:, kcols] * sin).astype(BF16)
    ones = jnp.ones((y.shape[0], LANES), BF16)
    for j in range(D_MODEL // LANES):
        v_out[:, 2 * j * LANES:(2 * j + 1) * LANES] = y[:, 2 * D_MODEL + j * LANES:
                                                        2 * D_MODEL + (j + 1) * LANES].astype(BF16)
        v_out[:, (2 * j + 1) * LANES:(2 * j + 2) * LANES] = ones


def _odd_in(h, g0, w, w_rot, cos, sin):
    lp, lpk = h.shape[0], cos.shape[0]
    tm = _pick_tile(lp, (256,))
    last = lp // tm - 1
    row_spec = lambda width: pl.BlockSpec((tm, width), lambda i: (i, 0))
    full = lambda arr: pl.BlockSpec(arr.shape, lambda i: (0,) * arr.ndim)
    sds = jax.ShapeDtypeStruct((lpk, D_MODEL), BF16)
    return pl.pallas_call(
        _odd_in_kernel,
        grid=(lpk // tm,),
        in_specs=[pl.BlockSpec((tm, D_MODEL), lambda i: (jnp.minimum(i, last), 0)), full(g0),
                  _weight_spec(w), _weight_spec(w_rot), row_spec(LANES), row_spec(LANES)],
        out_specs=[row_spec(D_MODEL), row_spec(D_MODEL), row_spec(2 * D_MODEL)],
        out_shape=[sds, sds, jax.ShapeDtypeStruct((lpk, 2 * D_MODEL), BF16)],
        compiler_params=pltpu.CompilerParams(dimension_semantics=("parallel",),
                                             vmem_limit_bytes=VMEM_LIMIT),
        name="odd_in",
    )(h, g0, w, w_rot, cos, sin)


def _attn_kernel(lam_ref, sw_ref, q_ref, k_ref, v_ref, o_ref, q2_scr, s_scr, smax_scr, p_scr, m_scr, acc_scr,
                 *, lam_init, tq, tk, nh):
    i = pl.program_id(1)
    lane = lax.broadcasted_iota(jnp.int32, (1, LANES), 1)
    mlo = lane < HALF
    for hd in range(nh):
        q = q_ref[:, hd * LANES:(hd + 1) * LANES]
        zq = jnp.zeros_like(q)
        q2_scr[hd] = jnp.concatenate([jnp.where(mlo, q, zq), jnp.where(mlo, zq, q)], axis=0)
    nt = (((1,), (1,)), ((), ()))

    n_blocks = (i * tq) // tk + 1

    def scores_into(hd, slot, t, diagonal):
        off = pl.multiple_of(t * tk, tk)
        s = lax.dot_general(q2_scr[hd], k_ref[pl.ds(off, tk), hd * LANES:(hd + 1) * LANES], nt,
                            preferred_element_type=F32)
        if diagonal:
            qrow = lax.broadcasted_iota(jnp.int32, (2 * tq, tk), 0)
            qrow = jnp.where(qrow >= tq, qrow - tq, qrow)
            kcol = lax.broadcasted_iota(jnp.int32, (2 * tq, tk), 1)
            s = jnp.where(kcol - qrow <= i * tq - off, s, NEG_BIG)
        s_scr[hd, slot] = s
        smax_scr[hd, slot] = jnp.max(s_scr[hd, slot], axis=-1, keepdims=True)

    def add_weighted_values(hd, slot, t):
        off = pl.multiple_of(t * tk, tk)
        acc_scr[hd] += jnp.dot(p_scr[hd, slot], v_ref[pl.ds(off, tk), 2 * hd * LANES:2 * (hd + 1) * LANES],
                               preferred_element_type=F32)

    def tick(t, slot, next_kind):
        for hd in range(nh):
            if next_kind is not None:
                scores_into(hd, 1 - slot, t + 1, next_kind == "diagonal")
            add_weighted_values(hd, 1 - slot, jnp.maximum(t - 1, 0))
            m_old = m_scr[hd]
            m_new = jnp.maximum(m_old, smax_scr[hd, slot])
            m_scr[hd] = m_new
            p_scr[hd, slot] = jnp.exp2((s_scr[hd, slot] - m_new).astype(BF16))
            acc_scr[hd] = jnp.exp2(m_old - m_new) * acc_scr[hd]

    def finish(slot, t):
        for hd in range(nh):
            add_weighted_values(hd, slot, t)

    m_scr[...] = jnp.full(m_scr.shape, NEG_BIG, F32)
    acc_scr[...] = jnp.zeros(acc_scr.shape, F32)
    for hd in range(nh):
        p_scr[hd, 1] = jnp.zeros(p_scr.shape[2:], BF16)

    @pl.when(n_blocks == 1)
    def _():
        for hd in range(nh):
            scores_into(hd, 0, 0, True)

    @pl.when(n_blocks > 1)
    def _():
        for hd in range(nh):
            scores_into(hd, 0, 0, False)

    def pair(u, carry):
        tick(2 * u, 0, "full")
        tick(2 * u + 1, 1, "full")
        return carry

    n_pairs = jnp.maximum(n_blocks - 2, 0) // 2
    lax.fori_loop(0, n_pairs, pair, 0)
    t0 = 2 * n_pairs

    @pl.when(n_blocks == 1)
    def _():
        tick(t0, 0, None)
        finish(0, t0)

    @pl.when(jnp.logical_and(n_blocks > 1, n_blocks % 2 == 0))
    def _():
        tick(t0, 0, "diagonal")
        tick(t0 + 1, 1, None)
        finish(1, t0 + 1)

    @pl.when(jnp.logical_and(n_blocks > 1, n_blocks % 2 == 1))
    def _():
        tick(t0, 0, "full")
        tick(t0 + 1, 1, "diagonal")
        tick(t0 + 2, 0, None)
        finish(0, t0 + 2)

    lv = lam_ref[...]
    lam = (jnp.exp(jnp.sum(lv[0:1] * lv[1:2], axis=-1, keepdims=True))
           - jnp.exp(jnp.sum(lv[2:3] * lv[3:4], axis=-1, keepdims=True)) + lam_init)
    for hd in range(nh):
        o = acc_scr[hd, :, :LANES] / acc_scr[hd, :, LANES:]
        o = o[:tq] - lam * o[tq:]
        o = _rms(o, sw_ref[...], SUBLN_EPS) * (1.0 - lam_init)
        o_ref[:, hd * LANES:(hd + 1) * LANES] = o.astype(BF16)


def _diff_attn(q, k, v, lam_vecs, subln_w, lam_init, lp):
    tq, tk, nh = ATT_Q_BLOCK, ATT_K_BLOCK, ATT_HEADS
    lpk = k.shape[0]
    blk = pl.BlockSpec((tq, nh * LANES), lambda h, i: (i, h))
    resident = lambda width: pl.BlockSpec((lpk, width), lambda h, i: (0, h), pipeline_mode=pl.Buffered(1))
    full = lambda arr: pl.BlockSpec(arr.shape, lambda h, i: (0,) * arr.ndim)
    return pl.pallas_call(
        functools.partial(_attn_kernel, lam_init=lam_init, tq=tq, tk=tk, nh=nh),
        grid=(DIFF_HEADS // nh, lp // tq),
        in_specs=[full(lam_vecs), full(subln_w), blk, resident(nh * LANES),
                  resident(2 * nh * LANES)],
        out_specs=blk,
        out_shape=jax.ShapeDtypeStruct((lp, D_MODEL), BF16),
        scratch_shapes=[pltpu.VMEM((nh, 2 * tq, LANES), BF16), pltpu.VMEM((nh, 2, 2 * tq, tk), F32),
                        pltpu.VMEM((nh, 2, 2 * tq, 1), F32), pltpu.VMEM((nh, 2, 2 * tq, tk), BF16),
                        pltpu.VMEM((nh, 2 * tq, 1), F32), pltpu.VMEM((nh, 2 * tq, 2 * LANES), F32)],
        compiler_params=pltpu.CompilerParams(dimension_semantics=("parallel", "arbitrary"),
                                             vmem_limit_bytes=VMEM_LIMIT),
        name="diff_attn",
    )(lam_vecs, subln_w, q, k, v)


def _odd_out_kernel(h_ref, o_ref, wout_ref, g1_ref, g2_ref, g3_ref, w1_ref, w2_ref, out_ref):
    m = jnp.dot(o_ref[...], wout_ref[...], preferred_element_type=F32)
    hm = h_ref[...] + _rms(m, g1_ref[...], RMS_EPS)
    out_ref[...] = _mlp_residual(hm, g2_ref[...], g3_ref[...], w1_ref, w2_ref)


def _odd_out(h, o, w_out, g1, g2, g3, w1, w2):
    lp = h.shape[0]
    tm = _pick_tile(lp, (256,))
    row_spec = lambda width: pl.BlockSpec((tm, width), lambda i: (i, 0))
    full = lambda arr: pl.BlockSpec(arr.shape, lambda i: (0,) * arr.ndim)
    return pl.pallas_call(
        _odd_out_kernel,
        grid=(lp // tm,),
        in_specs=[row_spec(D_MODEL), row_spec(D_MODEL), _weight_spec(w_out), full(g1), full(g2), full(g3),
                  _weight_spec(w1), _weight_spec(w2)],
        out_specs=row_spec(D_MODEL),
        out_shape=jax.ShapeDtypeStruct((lp, D_MODEL), F32),
        compiler_params=pltpu.CompilerParams(dimension_semantics=("parallel",),
                                             vmem_limit_bytes=VMEM_LIMIT),
        name="odd_out",
    )(h, o, w_out, g1, g2, g3, w1, w2)


def _rotate_half_weights(w):
    d_in, d_out = w.shape
    w4 = w.reshape(d_in, d_out // DIFF_HEAD, 2, DIFF_HEAD // 2)
    return jnp.concatenate([-w4[:, :, 1:2], w4[:, :, 0:1]], axis=2).reshape(d_in, d_out)


def _forward(x, meta, norm_g, mlp_w1, mlp_w2, ev, od):
    seq = x.shape[0]
    length = N_META + seq
    lp = -(-length // ROW_ALIGN) * ROW_ALIGN
    h = jnp.concatenate([meta.astype(x.dtype), x, jnp.zeros((lp - length, D_MODEL), x.dtype)], axis=0)

    lpk = -(-lp // ATT_K_BLOCK) * ATT_K_BLOCK
    pos = jnp.arange(lpk, dtype=F32)
    inv = ROPE_THETA ** (-jnp.arange(0, DIFF_HEAD, 2, dtype=F32) / DIFF_HEAD)
    ang = pos[:, None] * inv[None, :]
    ang = jnp.concatenate([ang, ang, ang, ang], axis=-1)
    cos, sin = jnp.cos(ang), jnp.sin(ang)

    head_id = jnp.arange(RWKV_WIDTH) // RWKV_HEAD
    seg = (head_id[:, None] == head_id[None, :]).astype(BF16)
    row2 = lambda t: t.reshape(1, -1)

    depth = norm_g.shape[0]
    for i in range(depth):
        g = norm_g[i]
        g0, g1, g2, g3 = (row2(g[n]) for n in range(4))
        w1 = mlp_w1[i].astype(BF16)
        w2 = mlp_w2[i].astype(BF16)
        j = i // 2
        if i % 2 == 0:
            (w_in, mu, w0, w_up, a0, a_up, g_up, k_k, k_a, r_k, ln_w, ln_b, pool_w, pool_scale,
             w_out) = (t[j] for t in ev)
            zeros = jnp.zeros((DECAY_RANK, RWKV_WIDTH), F32)
            lora_w = jnp.concatenate([jnp.concatenate([w_up, zeros], axis=1),
                                      jnp.concatenate([zeros, a_up], axis=1)], axis=0).astype(BF16)
            r, k2, v, na, b, logw, gate, z = _even_in(
                h, g0, w_in.astype(BF16), row2(mu), row2(w0), lora_w, row2(a0), g_up.astype(BF16),
                row2(k_k), row2(k_a), seg, pool_w.astype(BF16), row2(pool_scale))
            o = _rwkv_scan(r, k2, v, na, b, logw)
            h = _even_out(h, o, r, k2, v, gate, z, row2(ln_w), row2(ln_b), row2(r_k), seg,
                          w_out.astype(BF16), g1, g2, g3, w1, w2)
        else:
            w_in, lam_vecs, subln_w, w_out = (t[j] for t in od)
            w_rot = _rotate_half_weights(w_in[:, :2 * D_MODEL])
            q, k, v = _odd_in(h, g0, w_in.astype(BF16), w_rot.astype(BF16), cos, sin)
            lam_init = 0.8 - 0.6 * math.exp(-0.3 * i)
            o = _diff_attn(q, k, v, lam_vecs, row2(subln_w), lam_init, lp)
            h = _odd_out(h, o, w_out.astype(BF16), g1, g2, g3, w1, w2)
    return h[N_META:length]


def kernel(x, meta, norm_g, mlp_w1, mlp_w2, ev_w_in, ev_mu, ev_w0, ev_w_up, ev_a0, ev_a_up, ev_g_up, ev_k_k,
           ev_k_a, ev_r_k, ev_ln_w, ev_ln_b, ev_pool_w, ev_pool_scale, ev_w_out, od_w_in, od_lambda,
           od_subln_w, od_w_out):
    ev = (ev_w_in, ev_mu, ev_w0, ev_w_up, ev_a0, ev_a_up, ev_g_up, ev_k_k, ev_k_a, ev_r_k, ev_ln_w, ev_ln_b,
          ev_pool_w, ev_pool_scale, ev_w_out)
    od = (od_w_in, od_lambda, od_subln_w, od_w_out)
    outs = [_forward(x[bi], meta, norm_g, mlp_w1, mlp_w2, ev, od) for bi in range(x.shape[0])]
    return jnp.stack(outs, axis=0)
```

```python
import functools
import math

import jax
import jax.numpy as jnp
from jax import lax
from jax.experimental import pallas as pl
from jax.experimental.pallas import tpu as pltpu

F32, BF16 = jnp.float32, jnp.bfloat16

D_MODEL = 1024
N_META = 16
RMS_EPS = 1e-6
D_FF = 4 * D_MODEL
RWKV_HEAD = 64
RWKV_WIDTH = D_MODEL // 2
DECAY_RANK = 64
ICLR_RANK = 64
GATE_RANK = 128
GN_EPS = RWKV_HEAD * 1e-5
POOL_WIDTH = D_MODEL - RWKV_WIDTH
POOL_WINDOWS = (2, 4, 8, 16)
POOL_GROUP_W = POOL_WIDTH // len(POOL_WINDOWS)
POOL_CARRY = 16
SHIFT_WIDTH = 3 * RWKV_WIDTH + DECAY_RANK + ICLR_RANK + GATE_RANK
EVEN_IN = SHIFT_WIDTH + POOL_WIDTH
DIFF_HEADS = 8
DIFF_HEAD = D_MODEL // (2 * DIFF_HEADS)
SUBLN_EPS = 1e-5
ROPE_THETA = 10000.0

LANES = 128
HALF = LANES // 2
ROW_ALIGN = 256
CHUNK = 64
SOLVE_BLOCK = 8
SCAN_CHUNKS = 2
ATT_Q_BLOCK = 256
ATT_K_BLOCK = 1024
ATT_HEADS = 2
FF_CHUNK = 1024
NEG_BIG = -1e30
VMEM_LIMIT = 56 * 1024 * 1024


def _pick_tile(n, candidates):
    for c in candidates:
        if n % c == 0:
            return c
    raise ValueError(f"no tile in {candidates} divides {n}")


def _rms(t, g, eps):
    return t * lax.rsqrt(jnp.mean(t * t, axis=-1, keepdims=True) + eps) * g


def _split2(x):
    hi = x.astype(BF16)
    lo = (x - hi.astype(F32)).astype(BF16)
    return hi, lo


def _dot(a, b):
    return jnp.dot(a.astype(BF16), b.astype(BF16), preferred_element_type=F32)


def _dot2(a, b):
    ab = a.astype(BF16)
    bh, bl = _split2(b)
    d = functools.partial(jnp.dot, preferred_element_type=F32)
    return d(ab, bh) + d(ab, bl)


def _dot_nt(a, b):
    return lax.dot_general(a.astype(BF16), b.astype(BF16), (((1,), (1,)), ((), ())),
                           preferred_element_type=F32)


def _head_sum(x, seg_ref):
    xh, xl = _split2(x)
    d = functools.partial(jnp.dot, preferred_element_type=F32)
    return d(xh, seg_ref[...]) + d(xl, seg_ref[...])


def _sigmoid(x):
    return 1.0 / (1.0 + jnp.exp(-x))


def _softplus(x):
    return jnp.maximum(x, 0.0) + jnp.log(1.0 + jnp.exp(-jnp.abs(x)))


def _mlp_residual(hm, g2, g3, w1_ref, w2_ref):
    n = _rms(hm, g2, RMS_EPS).astype(BF16)
    acc = jnp.zeros(hm.shape, F32)
    for c in range(D_FF // FF_CHUNK):
        cols = slice(c * FF_CHUNK, (c + 1) * FF_CHUNK)
        a = jnp.dot(n, w1_ref[:, cols], preferred_element_type=F32)
        a = jnp.square(jnp.maximum(a, 0.0)).astype(BF16)
        acc = acc + jnp.dot(a, w2_ref[cols, :], preferred_element_type=F32)
    return hm + _rms(acc, g3, RMS_EPS)


def _even_in_kernel(h_ref, g0_ref, win_ref, mu_ref, w0_ref, lora_ref, a0_ref, gup_ref,
                    kk_ref, ka_ref, seg_ref, poolw_ref, pscale_ref,
                    r_out, k_out, v_out, na_out, b_out, lw_out, g_out, z_out,
                    ycarry, ucarry, *, tm):
    i = pl.program_id(0)

    @pl.when(i == 0)
    def _():
        ycarry[...] = jnp.zeros(ycarry.shape, F32)
        ucarry[...] = jnp.zeros(ucarry.shape, F32)

    hn = _rms(h_ref[...], g0_ref[...], RMS_EPS).astype(BF16)
    y = jnp.dot(hn, win_ref[...], preferred_element_type=F32)

    ysh = y[:, :SHIFT_WIDTH]
    row = lax.broadcasted_iota(jnp.int32, (tm, 1), 0)
    prev = jnp.where(row == 0, ycarry[7:8, :], pltpu.roll(ysh, 1, axis=0))
    ycarry[...] = ysh[tm - 8:, :]
    ys = ysh + (prev - ysh) * mu_ref[...]

    rw = RWKV_WIDTH
    r = ys[:, 0:rw]
    k = ys[:, rw:2 * rw]
    v = ys[:, 2 * rw:3 * rw]
    wa = ys[:, 3 * rw:3 * rw + LANES]
    gd = ys[:, 3 * rw + LANES:SHIFT_WIDTH]

    lane = lax.broadcasted_iota(jnp.int32, (1, LANES), 1)
    lora_in = jnp.where(lane < DECAY_RANK, jnp.tanh(wa), wa)
    lora = _dot(lora_in, lora_ref[...])
    wlog = -_softplus(-(w0_ref[...] + lora[:, :rw])) - 0.5
    logw = -jnp.exp(wlog)
    a = _sigmoid(a0_ref[...] + lora[:, rw:])
    g = _dot(_sigmoid(gd), gup_ref[...])

    kk = k * kk_ref[...]
    kk = kk * lax.rsqrt(jnp.maximum(_head_sum(kk * kk, seg_ref), 1e-24))
    k2 = k * (1.0 + (a - 1.0) * ka_ref[...])

    r_out[...] = r
    k_out[...] = k2
    v_out[...] = v
    na_out[...] = -kk
    b_out[...] = kk * a
    lw_out[...] = logw
    g_out[...] = g

    u = y[:, SHIFT_WIDTH:]
    ext = jnp.concatenate([ucarry[...], u], axis=0)
    ucarry[...] = u[tm - POOL_CARRY:, :]
    t_idx = i * tm + row
    for gi, win in enumerate(POOL_WINDOWS):
        cols = slice(gi * POOL_GROUP_W, (gi + 1) * POOL_GROUP_W)
        s = ext[:, cols]
        span = 1
        while span < win:
            s = s + pltpu.roll(s, span, axis=0)
            span *= 2
        cnt = jnp.minimum(t_idx + 1, win).astype(F32)
        d = s[POOL_CARRY:, :] / cnt - u[:, cols]
        z_out[:, cols] = _dot(d, poolw_ref[gi]) * pscale_ref[:, cols]


def _even_in(h, g0, w_in, mu, w0, lora_w, a0, g_up, k_k, k_a, seg, pool_w, pool_scale):
    lp = h.shape[0]
    tm = _pick_tile(lp, (256,))
    rw = RWKV_WIDTH
    row_spec = lambda width: pl.BlockSpec((tm, width), lambda i: (i, 0))
    full = lambda arr: pl.BlockSpec(arr.shape, lambda i: (0,) * arr.ndim)
    out_sds = jax.ShapeDtypeStruct((lp, rw), F32)
    return pl.pallas_call(
        functools.partial(_even_in_kernel, tm=tm),
        grid=(lp // tm,),
        in_specs=[row_spec(D_MODEL), full(g0), full(w_in), full(mu), full(w0), full(lora_w), full(a0),
                  full(g_up), full(k_k), full(k_a), full(seg), full(pool_w), full(pool_scale)],
        out_specs=[row_spec(rw)] * 8,
        out_shape=[out_sds] * 8,
        scratch_shapes=[pltpu.VMEM((8, SHIFT_WIDTH), F32), pltpu.VMEM((POOL_CARRY, POOL_WIDTH), F32)],
        compiler_params=pltpu.CompilerParams(dimension_semantics=("arbitrary",),
                                             vmem_limit_bytes=VMEM_LIMIT),
        name="even_in",
    )(h, g0, w_in, mu, w0, lora_w, a0, g_up, k_k, k_a, seg, pool_w, pool_scale)


def _scan_kernel(r_ref, k_ref, v_ref, na_ref, b_ref, lw_ref, o_ref, h_scr):
    @pl.when(pl.program_id(0) == 0)
    def _():
        h_scr[...] = jnp.zeros(h_scr.shape, F32)

    c = CHUNK
    n_chunks = r_ref.shape[0] // c
    row = lax.broadcasted_iota(jnp.int32, (c, c), 0)
    col = lax.broadcasted_iota(jnp.int32, (c, c), 1)
    tri = jnp.where(col <= row, 1.0, 0.0).astype(BF16)
    lane = lax.broadcasted_iota(jnp.int32, (1, LANES), 1)
    mlo = lane < HALF
    prow = lax.broadcasted_iota(jnp.int32, (LANES, LANES), 0)
    pcol = lax.broadcasted_iota(jnp.int32, (LANES, LANES), 1)
    same_head = (prow < HALF) == (pcol < HALF)
    diag = prow == pcol
    trow = lax.broadcasted_iota(jnp.int32, (c, LANES), 0)
    tcol = lax.broadcasted_iota(jnp.int32, (c, LANES), 1)
    tcol = jnp.where(tcol >= HALF, tcol - HALF, tcol)
    strict2 = tcol < trow
    incl2 = jnp.concatenate([tcol <= trow] * 2, axis=0)
    same_blk = (tcol // SOLVE_BLOCK) == (trow // SOLVE_BLOCK)
    lane2 = lax.broadcasted_iota(jnp.int32, (1, 2 * LANES), 1)
    zeros_c = jnp.zeros((c, LANES), F32)
    zeros_2c = jnp.zeros((c, 2 * LANES), F32)
    n_pairs = RWKV_WIDTH // LANES
    pair_cols = [slice(p * LANES, (p + 1) * LANES) for p in range(n_pairs)]
    d = functools.partial(jnp.dot, preferred_element_type=F32)

    prep = []
    for ci in range(n_chunks):
        rows = slice(ci * c, (ci + 1) * c)
        lw = lw_ref[rows, :]
        lw_hi = lw.astype(BF16)
        lw_r = lw - lw_hi.astype(F32)
        lw_mid = lw_r.astype(BF16)
        lw_lo = (lw_r - lw_mid.astype(F32)).astype(BF16)
        cum = d(tri, lw_hi) + (d(tri, lw_mid) + d(tri, lw_lo))
        cum_end = cum[c - 1:c, :]
        e_neg = jnp.exp(-cum)
        e_end = jnp.exp(cum_end - cum)
        b_all = b_ref[rows, :]
        k_all = k_ref[rows, :]
        prep.append(dict(r_t=r_ref[rows, :] * jnp.exp(cum), a_t=na_ref[rows, :] * jnp.exp(cum - lw),
                         b_t=b_all * e_neg, k_t=k_all * e_neg, b_h=b_all * e_end, k_h=k_all * e_end,
                         v=v_ref[rows, :], p_end=jnp.exp(cum_end)))
    units = [(ci, p) for ci in range(n_chunks) for p in range(n_pairs)]

    a_all = {}
    for ci, p in units:
        cols = pair_cols[p]
        rt, at = prep[ci]["r_t"][:, cols], prep[ci]["a_t"][:, cols]
        lhs4 = jnp.concatenate([jnp.where(mlo, at, zeros_c), jnp.where(mlo, zeros_c, at),
                                jnp.where(mlo, rt, zeros_c), jnp.where(mlo, zeros_c, rt)], axis=0)
        a_all[ci, p] = _dot_nt(lhs4, jnp.concatenate([prep[ci]["b_t"][:, cols], prep[ci]["k_t"][:, cols]], axis=0))

    heads = []
    for ci, p in units:
        cols = pair_cols[p]
        at, vp = prep[ci]["a_t"][:, cols], prep[ci]["v"][:, cols]
        at_sw = pltpu.roll(at, HALF, axis=1)
        vp_sw = pltpu.roll(vp, HALF, axis=1)
        for hh in range(2):
            nk = jnp.where(strict2, a_all[ci, p][hh * c:(hh + 1) * c], 0.0)
            av = _dot(nk, jnp.concatenate([zeros_c, vp_sw if hh == 0 else vp], axis=0))
            x0 = jnp.where(mlo, at if hh == 0 else at_sw, av)
            nk_sw = pltpu.roll(nk, HALF, axis=1)
            n_split = jnp.where(mlo, jnp.where(same_blk, 0.0, nk), jnp.where(same_blk, nk_sw, 0.0))
            heads.append(jnp.concatenate([x0, n_split], axis=1))

    for _ in range(int(math.log2(SOLVE_BLOCK))):
        nxt = []
        for y in heads:
            prod = _dot2(y[:, LANES:], jnp.concatenate([zeros_2c, y], axis=0))
            nxt.append(jnp.where(lane2 >= LANES + HALF, prod, y + prod))
        heads = nxt
    for _ in range(int(math.log2(c // SOLVE_BLOCK))):
        nxt = []
        for y in heads:
            prod = _dot2(y[:, LANES:], jnp.concatenate([y, zeros_2c], axis=0))
            nxt.append(jnp.where(lane2 < LANES, y + prod, prod))
        heads = nxt

    big = {}
    for n, (ci, p) in enumerate(units):
        cols = pair_cols[p]
        x_lo, x_hi = heads[2 * n][:, :LANES], heads[2 * n + 1][:, :LANES]
        w_p = jnp.where(mlo, x_lo, pltpu.roll(x_hi, HALF, axis=1))
        u0_p = jnp.where(mlo, pltpu.roll(x_lo, HALF, axis=1), x_hi)
        rhs = jnp.concatenate([jnp.concatenate([w_p, u0_p], axis=1),
                               jnp.concatenate([zeros_c, prep[ci]["v"][:, cols]], axis=1)], axis=0)
        a_r = jnp.where(incl2, a_all[ci, p][2 * c:], 0.0)
        bk_t = jnp.concatenate([prep[ci]["b_h"][:, cols], prep[ci]["k_h"][:, cols]], axis=0).T
        big[ci, p] = _dot(jnp.concatenate([a_r, bk_t], axis=0), rhs)

    state = [h_scr[p] for p in range(n_pairs)]
    for ci in range(n_chunks):
        rows = slice(ci * c, (ci + 1) * c)
        for p, cols in enumerate(pair_cols):
            res = big[ci, p]
            q_hat = prep[ci]["r_t"][:, cols] + jnp.where(mlo, res[:c, :LANES], res[c:2 * c, :LANES])
            o_hat = jnp.where(mlo, res[:c, LANES:], res[c:2 * c, LANES:])
            g_mat = (jnp.where(same_head, res[2 * c:, :LANES], 0.0)
                     + jnp.where(diag, prep[ci]["p_end"][:, cols], 0.0))
            j_mat = jnp.where(same_head, res[2 * c:, LANES:], 0.0)
            st = _dot2(jnp.concatenate([q_hat, g_mat], axis=0), state[p])
            o_ref[rows, cols] = st[:c] + o_hat
            state[p] = st[c:] + j_mat
    for p in range(n_pairs):
        h_scr[p] = state[p]


def _rwkv_scan(r, k2, v, na, b, logw):
    lp, rw = r.shape
    rows = CHUNK * SCAN_CHUNKS
    spec = pl.BlockSpec((rows, rw), lambda i: (i, 0))
    return pl.pallas_call(
        _scan_kernel,
        grid=(lp // rows,),
        in_specs=[spec] * 6,
        out_specs=spec,
        out_shape=jax.ShapeDtypeStruct((lp, rw), F32),
        scratch_shapes=[pltpu.VMEM((rw // LANES, LANES, LANES), F32)],
        compiler_params=pltpu.CompilerParams(dimension_semantics=("arbitrary",),
                                             vmem_limit_bytes=VMEM_LIMIT),
        name="rwkv_scan",
    )(r, k2, v, na, b, logw)


def _even_out_kernel(h_ref, o_ref, r_ref, k_ref, v_ref, g_ref, z_ref, lnw_ref, lnb_ref, rk_ref, seg_ref,
                     wout_ref, g1_ref, g2_ref, g3_ref, w1_ref, w2_ref, out_ref):
    inv_n = 1.0 / RWKV_HEAD
    o = o_ref[...]
    mean = _head_sum(o, seg_ref) * inv_n
    dev = o - mean
    var = _head_sum(dev * dev, seg_ref) * inv_n
    on = dev * lax.rsqrt(var + GN_EPS) * lnw_ref[...] + lnb_ref[...]
    bonus = _head_sum(r_ref[...] * k_ref[...] * rk_ref[...], seg_ref) * v_ref[...]
    om = (on + bonus) * g_ref[...]
    rw = RWKV_WIDTH
    m = _dot(om, wout_ref[:rw, :]) + _dot(z_ref[...], wout_ref[rw:, :])
    hm = h_ref[...] + _rms(m, g1_ref[...], RMS_EPS)
    out_ref[...] = _mlp_residual(hm, g2_ref[...], g3_ref[...], w1_ref, w2_ref)


def _weight_spec(arr):
    return pl.BlockSpec(arr.shape, lambda i: (0,) * arr.ndim, pipeline_mode=pl.Buffered(1))


def _even_out(h, o, r, k2, v, g, z, ln_w, ln_b, r_k, seg, w_out, g1, g2, g3, w1, w2):
    lp = h.shape[0]
    tm = _pick_tile(lp, (256,))
    row_spec = lambda width: pl.BlockSpec((tm, width), lambda i: (i, 0))
    full = lambda arr: pl.BlockSpec(arr.shape, lambda i: (0,) * arr.ndim)
    rw = RWKV_WIDTH
    return pl.pallas_call(
        _even_out_kernel,
        grid=(lp // tm,),
        in_specs=[row_spec(D_MODEL)] + [row_spec(rw)] * 6 +
                 [full(ln_w), full(ln_b), full(r_k), full(seg), _weight_spec(w_out), full(g1), full(g2),
                  full(g3), _weight_spec(w1), _weight_spec(w2)],
        out_specs=row_spec(D_MODEL),
        out_shape=jax.ShapeDtypeStruct((lp, D_MODEL), F32),
        compiler_params=pltpu.CompilerParams(dimension_semantics=("parallel",),
                                             vmem_limit_bytes=VMEM_LIMIT),
        name="even_out",
    )(h, o, r, k2, v, g, z, ln_w, ln_b, r_k, seg, w_out, g1, g2, g3, w1, w2)


def _odd_in_kernel(h_ref, g0_ref, w_ref, wrot_ref, cos_ref, sin_ref, q_out, k_out, v_out):
    hn = _rms(h_ref[...], g0_ref[...], RMS_EPS).astype(BF16)
    y = jnp.dot(hn, w_ref[...], preferred_element_type=F32)
    yr = jnp.dot(hn, wrot_ref[...], preferred_element_type=F32)
    cos = cos_ref[...]
    sin = sin_ref[...]
    scale = DIFF_HEAD ** -0.5 * math.log2(math.e)
    for j in range(D_MODEL // LANES):
        cols = slice(j * LANES, (j + 1) * LANES)
        kcols = slice(D_MODEL + j * LANES, D_MODEL + (j + 1) * LANES)
        q_out[:, cols] = ((y[:, cols] * cos + yr[:, cols] * sin) * scale).astype(BF16)
        k_out[:, cols] = (y[:, kcols] * cos + yr[:, kcols] * sin).astype(BF16)
    ones = jnp.ones((y.shape[0], LANES), BF16)
    for j in range(D_MODEL // LANES):
        v_out[:, 2 * j * LANES:(2 * j + 1) * LANES] = y[:, 2 * D_MODEL + j * LANES:
                                                        2 * D_MODEL + (j + 1) * LANES].astype(BF16)
        v_out[:, (2 * j + 1) * LANES:(2 * j + 2) * LANES] = ones


def _odd_in(h, g0, w, w_rot, cos, sin):
    lp, lpk = h.shape[0], cos.shape[0]
    tm = _pick_tile(lp, (256,))
    last = lp // tm - 1
    row_spec = lambda width: pl.BlockSpec((tm, width), lambda i: (i, 0))
    full = lambda arr: pl.BlockSpec(arr.shape, lambda i: (0,) * arr.ndim)
    sds = jax.ShapeDtypeStruct((lpk, D_MODEL), BF16)
    return pl.pallas_call(
        _odd_in_kernel,
        grid=(lpk // tm,),
        in_specs=[pl.BlockSpec((tm, D_MODEL), lambda i: (jnp.minimum(i, last), 0)), full(g0),
                  _weight_spec(w), _weight_spec(w_rot), row_spec(LANES), row_spec(LANES)],
        out_specs=[row_spec(D_MODEL), row_spec(D_MODEL), row_spec(2 * D_MODEL)],
        out_shape=[sds, sds, jax.ShapeDtypeStruct((lpk, 2 * D_MODEL), BF16)],
        compiler_params=pltpu.CompilerParams(dimension_semantics=("parallel",),
                                             vmem_limit_bytes=VMEM_LIMIT),
        name="odd_in",
    )(h, g0, w, w_rot, cos, sin)


def _attn_kernel(lam_ref, sw_ref, q_ref, k_ref, v_ref, o_ref, q2_scr, s_scr, smax_scr, p_scr, m_scr, acc_scr,
                 *, lam_init, tq, tk, nh):
    i = pl.program_id(1)
    lane = lax.broadcasted_iota(jnp.int32, (1, LANES), 1)
    mlo = lane < HALF
    for hd in range(nh):
        q = q_ref[:, hd * LANES:(hd + 1) * LANES]
        zq = jnp.zeros_like(q)
        q2_scr[hd] = jnp.concatenate([jnp.where(mlo, q, zq), jnp.where(mlo, zq, q)], axis=0)
    nt = (((1,), (1,)), ((), ()))

    n_blocks = (i * tq) // tk + 1

    def scores_into(hd, slot, t, diagonal):
        off = pl.multiple_of(t * tk, tk)
        s = lax.dot_general(q2_scr[hd], k_ref[pl.ds(off, tk), hd * LANES:(hd + 1) * LANES], nt,
                            preferred_element_type=F32)
        if diagonal:
            qrow = lax.broadcasted_iota(jnp.int32, (2 * tq, tk), 0)
            qrow = jnp.where(qrow >= tq, qrow - tq, qrow)
            kcol = lax.broadcasted_iota(jnp.int32, (2 * tq, tk), 1)
            s = jnp.where(kcol - qrow <= i * tq - off, s, NEG_BIG)
        s_scr[hd, slot] = s
        smax_scr[hd, slot] = jnp.max(s_scr[hd, slot], axis=-1, keepdims=True)

    def add_weighted_values(hd, slot, t):
        off = pl.multiple_of(t * tk, tk)
        acc_scr[hd] += jnp.dot(p_scr[hd, slot], v_ref[pl.ds(off, tk), 2 * hd * LANES:2 * (hd + 1) * LANES],
                               preferred_element_type=F32)

    def tick(t, slot, next_kind):
        for hd in range(nh):
            if next_kind is not None:
                scores_into(hd, 1 - slot, t + 1, next_kind == "diagonal")
            add_weighted_values(hd, 1 - slot, jnp.maximum(t - 1, 0))
            m_old = m_scr[hd]
            m_new = jnp.maximum(m_old, smax_scr[hd, slot])
            m_scr[hd] = m_new
            p_scr[hd, slot] = jnp.exp2((s_scr[hd, slot] - m_new).astype(BF16))
            acc_scr[hd] = jnp.exp2(m_old - m_new) * acc_scr[hd]

    def finish(slot, t):
        for hd in range(nh):
            add_weighted_values(hd, slot, t)

    m_scr[...] = jnp.full(m_scr.shape, NEG_BIG, F32)
    acc_scr[...] = jnp.zeros(acc_scr.shape, F32)
    for hd in range(nh):
        p_scr[hd, 1] = jnp.zeros(p_scr.shape[2:], BF16)

    @pl.when(n_blocks == 1)
    def _():
        for hd in range(nh):
            scores_into(hd, 0, 0, True)

    @pl.when(n_blocks > 1)
    def _():
        for hd in range(nh):
            scores_into(hd, 0, 0, False)

    def pair(u, carry):
        tick(2 * u, 0, "full")
        tick(2 * u + 1, 1, "full")
        return carry

    n_pairs = jnp.maximum(n_blocks - 2, 0) // 2
    lax.fori_loop(0, n_pairs, pair, 0)
    t0 = 2 * n_pairs

    @pl.when(n_blocks == 1)
    def _():
        tick(t0, 0, None)
        finish(0, t0)

    @pl.when(jnp.logical_and(n_blocks > 1, n_blocks % 2 == 0))
    def _():
        tick(t0, 0, "diagonal")
        tick(t0 + 1, 1, None)
        finish(1, t0 + 1)

    @pl.when(jnp.logical_and(n_blocks > 1, n_blocks % 2 == 1))
    def _():
        tick(t0, 0, "full")
        tick(t0 + 1, 1, "diagonal")
        tick(t0 + 2, 0, None)
        finish(0, t0 + 2)

    lv = lam_ref[...]
    lam = (jnp.exp(jnp.sum(lv[0:1] * lv[1:2], axis=-1, keepdims=True))
           - jnp.exp(jnp.sum(lv[2:3] * lv[3:4], axis=-1, keepdims=True)) + lam_init)
    for hd in range(nh):
        o = acc_scr[hd, :, :LANES] / acc_scr[hd, :, LANES:]
        o = o[:tq] - lam * o[tq:]
        o = _rms(o, sw_ref[...], SUBLN_EPS) * (1.0 - lam_init)
        o_ref[:, hd * LANES:(hd + 1) * LANES] = o.astype(BF16)


def _diff_attn(q, k, v, lam_vecs, subln_w, lam_init, lp):
    tq, tk, nh = ATT_Q_BLOCK, ATT_K_BLOCK, ATT_HEADS
    lpk = k.shape[0]
    blk = pl.BlockSpec((tq, nh * LANES), lambda h, i: (i, h))
    resident = lambda width: pl.BlockSpec((lpk, width), lambda h, i: (0, h), pipeline_mode=pl.Buffered(1))
    full = lambda arr: pl.BlockSpec(arr.shape, lambda h, i: (0,) * arr.ndim)
    return pl.pallas_call(
        functools.partial(_attn_kernel, lam_init=lam_init, tq=tq, tk=tk, nh=nh),
        grid=(DIFF_HEADS // nh, lp // tq),
        in_specs=[full(lam_vecs), full(subln_w), blk, resident(nh * LANES),
                  resident(2 * nh * LANES)],
        out_specs=blk,
        out_shape=jax.ShapeDtypeStruct((lp, D_MODEL), BF16),
        scratch_shapes=[pltpu.VMEM((nh, 2 * tq, LANES), BF16), pltpu.VMEM((nh, 2, 2 * tq, tk), F32),
                        pltpu.VMEM((nh, 2, 2 * tq, 1), F32), pltpu.VMEM((nh, 2, 2 * tq, tk), BF16),
                        pltpu.VMEM((nh, 2 * tq, 1), F32), pltpu.VMEM((nh, 2 * tq, 2 * LANES), F32)],
        compiler_params=pltpu.CompilerParams(dimension_semantics=("parallel", "arbitrary"),
                                             vmem_limit_bytes=VMEM_LIMIT),
        name="diff_attn",
    )(lam_vecs, subln_w, q, k, v)


def _odd_out_kernel(h_ref, o_ref, wout_ref, g1_ref, g2_ref, g3_ref, w1_ref, w2_ref, out_ref):
    m = jnp.dot(o_ref[...], wout_ref[...], preferred_element_type=F32)
    hm = h_ref[...] + _rms(m, g1_ref[...], RMS_EPS)
    out_ref[...] = _mlp_residual(hm, g2_ref[...], g3_ref[...], w1_ref, w2_ref)


def _odd_out(h, o, w_out, g1, g2, g3, w1, w2):
    lp = h.shape[0]
    tm = _pick_tile(lp, (256,))
    row_spec = lambda width: pl.BlockSpec((tm, width), lambda i: (i, 0))
    full = lambda arr: pl.BlockSpec(arr.shape, lambda i: (0,) * arr.ndim)
    return pl.pallas_call(
        _odd_out_kernel,
        grid=(lp // tm,),
        in_specs=[row_spec(D_MODEL), row_spec(D_MODEL), _weight_spec(w_out), full(g1), full(g2), full(g3),
                  _weight_spec(w1), _weight_spec(w2)],
        out_specs=row_spec(D_MODEL),
        out_shape=jax.ShapeDtypeStruct((lp, D_MODEL), F32),
        compiler_params=pltpu.CompilerParams(dimension_semantics=("parallel",),
                                             vmem_limit_bytes=VMEM_LIMIT),
        name="odd_out",
    )(h, o, w_out, g1, g2, g3, w1, w2)


def _rotate_half_weights(w):
    d_in, d_out = w.shape
    w4 = w.reshape(d_in, d_out // DIFF_HEAD, 2, DIFF_HEAD // 2)
    return jnp.concatenate([-w4[:, :, 1:2], w4[:, :, 0:1]], axis=2).reshape(d_in, d_out)


def _forward(x, meta, norm_g, mlp_w1, mlp_w2, ev, od):
    seq = x.shape[0]
    length = N_META + seq
    lp = -(-length // ROW_ALIGN) * ROW_ALIGN
    h = jnp.concatenate([meta.astype(x.dtype), x, jnp.zeros((lp - length, D_MODEL), x.dtype)], axis=0)

    lpk = -(-lp // ATT_K_BLOCK) * ATT_K_BLOCK
    pos = jnp.arange(lpk, dtype=F32)
    inv = ROPE_THETA ** (-jnp.arange(0, DIFF_HEAD, 2, dtype=F32) / DIFF_HEAD)
    ang = pos[:, None] * inv[None, :]
    ang = jnp.concatenate([ang, ang, ang, ang], axis=-1)
    cos, sin = jnp.cos(ang), jnp.sin(ang)

    head_id = jnp.arange(RWKV_WIDTH) // RWKV_HEAD
    seg = (head_id[:, None] == head_id[None, :]).astype(BF16)
    row2 = lambda t: t.reshape(1, -1)

    depth = norm_g.shape[0]
    for i in range(depth):
        g = norm_g[i]
        g0, g1, g2, g3 = (row2(g[n]) for n in range(4))
        w1 = mlp_w1[i].astype(BF16)
        w2 = mlp_w2[i].astype(BF16)
        j = i // 2
        if i % 2 == 0:
            (w_in, mu, w0, w_up, a0, a_up, g_up, k_k, k_a, r_k, ln_w, ln_b, pool_w, pool_scale,
             w_out) = (t[j] for t in ev)
            zeros = jnp.zeros((DECAY_RANK, RWKV_WIDTH), F32)
            lora_w = jnp.concatenate([jnp.concatenate([w_up, zeros], axis=1),
                                      jnp.concatenate([zeros, a_up], axis=1)], axis=0).astype(BF16)
            r, k2, v, na, b, logw, gate, z = _even_in(
                h, g0, w_in.astype(BF16), row2(mu), row2(w0), lora_w, row2(a0), g_up.astype(BF16),
                row2(k_k), row2(k_a), seg, pool_w.astype(BF16), row2(pool_scale))
            o = _rwkv_scan(r, k2, v, na, b, logw)
            h = _even_out(h, o, r, k2, v, gate, z, row2(ln_w), row2(ln_b), row2(r_k), seg,
                          w_out.astype(BF16), g1, g2, g3, w1, w2)
        else:
            w_in, lam_vecs, subln_w, w_out = (t[j] for t in od)
            w_rot = _rotate_half_weights(w_in[:, :2 * D_MODEL])
            q, k, v = _odd_in(h, g0, w_in.astype(BF16), w_rot.astype(BF16), cos, sin)
            lam_init = 0.8 - 0.6 * math.exp(-0.3 * i)
            o = _diff_attn(q, k, v, lam_vecs, row2(subln_w), lam_init, lp)
            h = _odd_out(h, o, w_out.astype(BF16), g1, g2, g3, w1, w2)
    return h[N_META:length]


def kernel(x, meta, norm_g, mlp_w1, mlp_w2, ev_w_in, ev_mu, ev_w0, ev_w_up, ev_a0, ev_a_up, ev_g_up, ev_k_k,
           ev_k_a, ev_r_k, ev_ln_w, ev_ln_b, ev_pool_w, ev_pool_scale, ev_w_out, od_w_in, od_lambda,
           od_subln_w, od_w_out):
    ev = (ev_w_in, ev_mu, ev_w0, ev_w_up, ev_a0, ev_a_up, ev_g_up, ev_k_k, ev_k_a, ev_r_k, ev_ln_w, ev_ln_b,
          ev_pool_w, ev_pool_scale, ev_w_out)
    od = (od_w_in, od_lambda, od_subln_w, od_w_out)
    outs = [_forward(x[bi], meta, norm_g, mlp_w1, mlp_w2, ev, od) for bi in range(x.shape[0])]
    return jnp.stack(outs, axis=0)
```

```python
import functools
import math

import jax
import jax.numpy as jnp
from jax import lax
from jax.experimental import pallas as pl
from jax.experimental.pallas import tpu as pltpu

F32, BF16 = jnp.float32, jnp.bfloat16

D_MODEL = 1024
N_META = 16
RMS_EPS = 1e-6
D_FF = 4 * D_MODEL
RWKV_HEAD = 64
RWKV_WIDTH = D_MODEL // 2
DECAY_RANK = 64
ICLR_RANK = 64
GATE_RANK = 128
GN_EPS = RWKV_HEAD * 1e-5
POOL_WIDTH = D_MODEL - RWKV_WIDTH
POOL_WINDOWS = (2, 4, 8, 16)
POOL_GROUP_W = POOL_WIDTH // len(POOL_WINDOWS)
POOL_CARRY = 16
SHIFT_WIDTH = 3 * RWKV_WIDTH + DECAY_RANK + ICLR_RANK + GATE_RANK
EVEN_IN = SHIFT_WIDTH + POOL_WIDTH
DIFF_HEADS = 8
DIFF_HEAD = D_MODEL // (2 * DIFF_HEADS)
SUBLN_EPS = 1e-5
ROPE_THETA = 10000.0

LANES = 128
HALF = LANES // 2
ROW_ALIGN = 256
CHUNK = 64
SOLVE_BLOCK = 8
SCAN_CHUNKS = 2
ATT_Q_BLOCK = 256
ATT_K_BLOCK = 1024
ATT_HEADS = 2
FF_CHUNK = 1024
NEG_BIG = -1e30
BOUND_SLACK = 1.02
MIN_ROW_SUM = 2.0 ** -100
VMEM_LIMIT = 56 * 1024 * 1024


def _pick_tile(n, candidates):
    for c in candidates:
        if n % c == 0:
            return c
    raise ValueError(f"no tile in {candidates} divides {n}")


def _rms(t, g, eps):
    return t * lax.rsqrt(jnp.mean(t * t, axis=-1, keepdims=True) + eps) * g


def _split2(x):
    hi = x.astype(BF16)
    lo = (x - hi.astype(F32)).astype(BF16)
    return hi, lo


def _dot(a, b):
    return jnp.dot(a.astype(BF16), b.astype(BF16), preferred_element_type=F32)


def _dot2(a, b):
    ab = a.astype(BF16)
    bh, bl = _split2(b)
    d = functools.partial(jnp.dot, preferred_element_type=F32)
    return d(ab, bh) + d(ab, bl)


def _dot_nt(a, b):
    return lax.dot_general(a.astype(BF16), b.astype(BF16), (((1,), (1,)), ((), ())),
                           preferred_element_type=F32)


def _head_sum(x, seg_ref):
    xh, xl = _split2(x)
    d = functools.partial(jnp.dot, preferred_element_type=F32)
    return d(xh, seg_ref[...]) + d(xl, seg_ref[...])


def _sigmoid(x):
    return 1.0 / (1.0 + jnp.exp(-x))


def _softplus(x):
    return jnp.maximum(x, 0.0) + jnp.log(1.0 + jnp.exp(-jnp.abs(x)))


def _mlp_residual(hm, g2, g3, w1_ref, w2_ref):
    n = _rms(hm, g2, RMS_EPS).astype(BF16)
    acc = jnp.zeros(hm.shape, F32)
    for c in range(D_FF // FF_CHUNK):
        cols = slice(c * FF_CHUNK, (c + 1) * FF_CHUNK)
        a = jnp.dot(n, w1_ref[:, cols], preferred_element_type=F32)
        a = jnp.square(jnp.maximum(a, 0.0)).astype(BF16)
        acc = acc + jnp.dot(a, w2_ref[cols, :], preferred_element_type=F32)
    return hm + _rms(acc, g3, RMS_EPS)


def _even_in_kernel(h_ref, g0_ref, win_ref, mu_ref, w0_ref, lora_ref, a0_ref, gup_ref,
                    kk_ref, ka_ref, seg_ref, poolw_ref, pscale_ref,
                    r_out, k_out, v_out, na_out, b_out, lw_out, g_out, z_out,
                    ycarry, ucarry, *, tm):
    i = pl.program_id(0)

    @pl.when(i == 0)
    def _():
        ycarry[...] = jnp.zeros(ycarry.shape, F32)
        ucarry[...] = jnp.zeros(ucarry.shape, F32)

    hn = _rms(h_ref[...], g0_ref[...], RMS_EPS).astype(BF16)
    y = jnp.dot(hn, win_ref[...], preferred_element_type=F32)

    ysh = y[:, :SHIFT_WIDTH]
    row = lax.broadcasted_iota(jnp.int32, (tm, 1), 0)
    prev = jnp.where(row == 0, ycarry[7:8, :], pltpu.roll(ysh, 1, axis=0))
    ycarry[...] = ysh[tm - 8:, :]
    ys = ysh + (prev - ysh) * mu_ref[...]

    rw = RWKV_WIDTH
    r = ys[:, 0:rw]
    k = ys[:, rw:2 * rw]
    v = ys[:, 2 * rw:3 * rw]
    wa = ys[:, 3 * rw:3 * rw + LANES]
    gd = ys[:, 3 * rw + LANES:SHIFT_WIDTH]

    lane = lax.broadcasted_iota(jnp.int32, (1, LANES), 1)
    lora_in = jnp.where(lane < DECAY_RANK, jnp.tanh(wa), wa)
    lora = _dot(lora_in, lora_ref[...])
    wlog = -_softplus(-(w0_ref[...] + lora[:, :rw])) - 0.5
    logw = -jnp.exp(wlog)
    a = _sigmoid(a0_ref[...] + lora[:, rw:])
    g = _dot(_sigmoid(gd), gup_ref[...])

    kk = k * kk_ref[...]
    kk = kk * lax.rsqrt(jnp.maximum(_head_sum(kk * kk, seg_ref), 1e-24))
    k2 = k * (1.0 + (a - 1.0) * ka_ref[...])

    r_out[...] = r
    k_out[...] = k2
    v_out[...] = v
    na_out[...] = -kk
    b_out[...] = kk * a
    lw_out[...] = logw
    g_out[...] = g

    u = y[:, SHIFT_WIDTH:]
    ext = jnp.concatenate([ucarry[...], u], axis=0)
    ucarry[...] = u[tm - POOL_CARRY:, :]
    t_idx = i * tm + row
    for gi, win in enumerate(POOL_WINDOWS):
        cols = slice(gi * POOL_GROUP_W, (gi + 1) * POOL_GROUP_W)
        s = ext[:, cols]
        span = 1
        while span < win:
            s = s + pltpu.roll(s, span, axis=0)
            span *= 2
        cnt = jnp.minimum(t_idx + 1, win).astype(F32)
        d = s[POOL_CARRY:, :] / cnt - u[:, cols]
        z_out[:, cols] = _dot(d, poolw_ref[gi]) * pscale_ref[:, cols]


def _even_in(h, g0, w_in, mu, w0, lora_w, a0, g_up, k_k, k_a, seg, pool_w, pool_scale):
    lp = h.shape[0]
    tm = _pick_tile(lp, (256,))
    rw = RWKV_WIDTH
    row_spec = lambda width: pl.BlockSpec((tm, width), lambda i: (i, 0))
    full = lambda arr: pl.BlockSpec(arr.shape, lambda i: (0,) * arr.ndim)
    out_sds = jax.ShapeDtypeStruct((lp, rw), F32)
    return pl.pallas_call(
        functools.partial(_even_in_kernel, tm=tm),
        grid=(lp // tm,),
        in_specs=[row_spec(D_MODEL), full(g0), full(w_in), full(mu), full(w0), full(lora_w), full(a0),
                  full(g_up), full(k_k), full(k_a), full(seg), full(pool_w), full(pool_scale)],
        out_specs=[row_spec(rw)] * 8,
        out_shape=[out_sds] * 8,
        scratch_shapes=[pltpu.VMEM((8, SHIFT_WIDTH), F32), pltpu.VMEM((POOL_CARRY, POOL_WIDTH), F32)],
        compiler_params=pltpu.CompilerParams(dimension_semantics=("arbitrary",),
                                             vmem_limit_bytes=VMEM_LIMIT),
        name="even_in",
    )(h, g0, w_in, mu, w0, lora_w, a0, g_up, k_k, k_a, seg, pool_w, pool_scale)


def _scan_kernel(r_ref, k_ref, v_ref, na_ref, b_ref, lw_ref, o_ref, h_scr):
    @pl.when(pl.program_id(0) == 0)
    def _():
        h_scr[...] = jnp.zeros(h_scr.shape, F32)

    c = CHUNK
    n_chunks = r_ref.shape[0] // c
    row = lax.broadcasted_iota(jnp.int32, (c, c), 0)
    col = lax.broadcasted_iota(jnp.int32, (c, c), 1)
    tri = jnp.where(col <= row, 1.0, 0.0).astype(BF16)
    lane = lax.broadcasted_iota(jnp.int32, (1, LANES), 1)
    mlo = lane < HALF
    prow = lax.broadcasted_iota(jnp.int32, (LANES, LANES), 0)
    pcol = lax.broadcasted_iota(jnp.int32, (LANES, LANES), 1)
    same_head = (prow < HALF) == (pcol < HALF)
    diag = prow == pcol
    trow = lax.broadcasted_iota(jnp.int32, (c, LANES), 0)
    tcol = lax.broadcasted_iota(jnp.int32, (c, LANES), 1)
    tcol = jnp.where(tcol >= HALF, tcol - HALF, tcol)
    strict2 = tcol < trow
    incl2 = jnp.concatenate([tcol <= trow] * 2, axis=0)
    same_blk = (tcol // SOLVE_BLOCK) == (trow // SOLVE_BLOCK)
    lane2 = lax.broadcasted_iota(jnp.int32, (1, 2 * LANES), 1)
    zeros_c = jnp.zeros((c, LANES), F32)
    zeros_2c = jnp.zeros((c, 2 * LANES), F32)
    n_pairs = RWKV_WIDTH // LANES
    pair_cols = [slice(p * LANES, (p + 1) * LANES) for p in range(n_pairs)]
    d = functools.partial(jnp.dot, preferred_element_type=F32)

    prep = []
    for ci in range(n_chunks):
        rows = slice(ci * c, (ci + 1) * c)
        lw = lw_ref[rows, :]
        lw_hi = lw.astype(BF16)
        lw_r = lw - lw_hi.astype(F32)
        lw_mid = lw_r.astype(BF16)
        lw_lo = (lw_r - lw_mid.astype(F32)).astype(BF16)
        cum = d(tri, lw_hi) + (d(tri, lw_mid) + d(tri, lw_lo))
        cum_end = cum[c - 1:c, :]
        e_neg = jnp.exp(-cum)
        e_end = jnp.exp(cum_end - cum)
        b_all = b_ref[rows, :]
        k_all = k_ref[rows, :]
        prep.append(dict(r_t=r_ref[rows, :] * jnp.exp(cum), a_t=na_ref[rows, :] * jnp.exp(cum - lw),
                         b_t=b_all * e_neg, k_t=k_all * e_neg, b_h=b_all * e_end, k_h=k_all * e_end,
                         v=v_ref[rows, :], p_end=jnp.exp(cum_end)))
    units = [(ci, p) for ci in range(n_chunks) for p in range(n_pairs)]

    a_all = {}
    for ci, p in units:
        cols = pair_cols[p]
        rt, at = prep[ci]["r_t"][:, cols], prep[ci]["a_t"][:, cols]
        lhs4 = jnp.concatenate([jnp.where(mlo, at, zeros_c), jnp.where(mlo, zeros_c, at),
                                jnp.where(mlo, rt, zeros_c), jnp.where(mlo, zeros_c, rt)], axis=0)
        a_all[ci, p] = _dot_nt(lhs4, jnp.concatenate([prep[ci]["b_t"][:, cols], prep[ci]["k_t"][:, cols]], axis=0))

    heads = []
    for ci, p in units:
        cols = pair_cols[p]
        at, vp = prep[ci]["a_t"][:, cols], prep[ci]["v"][:, cols]
        at_sw = pltpu.roll(at, HALF, axis=1)
        vp_sw = pltpu.roll(vp, HALF, axis=1)
        for hh in range(2):
            nk = jnp.where(strict2, a_all[ci, p][hh * c:(hh + 1) * c], 0.0)
            av = _dot(nk, jnp.concatenate([zeros_c, vp_sw if hh == 0 else vp], axis=0))
            x0 = jnp.where(mlo, at if hh == 0 else at_sw, av)
            nk_sw = pltpu.roll(nk, HALF, axis=1)
            n_split = jnp.where(mlo, jnp.where(same_blk, 0.0, nk), jnp.where(same_blk, nk_sw, 0.0))
            heads.append(jnp.concatenate([x0, n_split], axis=1))

    for _ in range(int(math.log2(SOLVE_BLOCK))):
        nxt = []
        for y in heads:
            prod = _dot2(y[:, LANES:], jnp.concatenate([zeros_2c, y], axis=0))
            nxt.append(jnp.where(lane2 >= LANES + HALF, prod, y + prod))
        heads = nxt
    for _ in range(int(math.log2(c // SOLVE_BLOCK))):
        nxt = []
        for y in heads:
            prod = _dot2(y[:, LANES:], jnp.concatenate([y, zeros_2c], axis=0))
            nxt.append(jnp.where(lane2 < LANES, y + prod, prod))
        heads = nxt

    big = {}
    for n, (ci, p) in enumerate(units):
        cols = pair_cols[p]
        x_lo, x_hi = heads[2 * n][:, :LANES], heads[2 * n + 1][:, :LANES]
        w_p = jnp.where(mlo, x_lo, pltpu.roll(x_hi, HALF, axis=1))
        u0_p = jnp.where(mlo, pltpu.roll(x_lo, HALF, axis=1), x_hi)
        rhs = jnp.concatenate([jnp.concatenate([w_p, u0_p], axis=1),
                               jnp.concatenate([zeros_c, prep[ci]["v"][:, cols]], axis=1)], axis=0)
        a_r = jnp.where(incl2, a_all[ci, p][2 * c:], 0.0)
        bk_t = jnp.concatenate([prep[ci]["b_h"][:, cols], prep[ci]["k_h"][:, cols]], axis=0).T
        big[ci, p] = _dot(jnp.concatenate([a_r, bk_t], axis=0), rhs)

    state = [h_scr[p] for p in range(n_pairs)]
    for ci in range(n_chunks):
        rows = slice(ci * c, (ci + 1) * c)
        for p, cols in enumerate(pair_cols):
            res = big[ci, p]
            q_hat = prep[ci]["r_t"][:, cols] + jnp.where(mlo, res[:c, :LANES], res[c:2 * c, :LANES])
            o_hat = jnp.where(mlo, res[:c, LANES:], res[c:2 * c, LANES:])
            g_mat = (jnp.where(same_head, res[2 * c:, :LANES], 0.0)
                     + jnp.where(diag, prep[ci]["p_end"][:, cols], 0.0))
            j_mat = jnp.where(same_head, res[2 * c:, LANES:], 0.0)
            st = _dot2(jnp.concatenate([q_hat, g_mat], axis=0), state[p])
            o_ref[rows, cols] = st[:c] + o_hat
            state[p] = st[c:] + j_mat
    for p in range(n_pairs):
        h_scr[p] = state[p]


def _rwkv_scan(r, k2, v, na, b, logw):
    lp, rw = r.shape
    rows = CHUNK * SCAN_CHUNKS
    spec = pl.BlockSpec((rows, rw), lambda i: (i, 0))
    return pl.pallas_call(
        _scan_kernel,
        grid=(lp // rows,),
        in_specs=[spec] * 6,
        out_specs=spec,
        out_shape=jax.ShapeDtypeStruct((lp, rw), F32),
        scratch_shapes=[pltpu.VMEM((rw // LANES, LANES, LANES), F32)],
        compiler_params=pltpu.CompilerParams(dimension_semantics=("arbitrary",),
                                             vmem_limit_bytes=VMEM_LIMIT),
        name="rwkv_scan",
    )(r, k2, v, na, b, logw)


def _even_out_kernel(h_ref, o_ref, r_ref, k_ref, v_ref, g_ref, z_ref, lnw_ref, lnb_ref, rk_ref, seg_ref,
                     wout_ref, g1_ref, g2_ref, g3_ref, w1_ref, w2_ref, out_ref):
    inv_n = 1.0 / RWKV_HEAD
    o = o_ref[...]
    mean = _head_sum(o, seg_ref) * inv_n
    dev = o - mean
    var = _head_sum(dev * dev, seg_ref) * inv_n
    on = dev * lax.rsqrt(var + GN_EPS) * lnw_ref[...] + lnb_ref[...]
    bonus = _head_sum(r_ref[...] * k_ref[...] * rk_ref[...], seg_ref) * v_ref[...]
    om = (on + bonus) * g_ref[...]
    rw = RWKV_WIDTH
    m = _dot(om, wout_ref[:rw, :]) + _dot(z_ref[...], wout_ref[rw:, :])
    hm = h_ref[...] + _rms(m, g1_ref[...], RMS_EPS)
    out_ref[...] = _mlp_residual(hm, g2_ref[...], g3_ref[...], w1_ref, w2_ref)


def _weight_spec(arr):
    return pl.BlockSpec(arr.shape, lambda i: (0,) * arr.ndim, pipeline_mode=pl.Buffered(1))


def _even_out(h, o, r, k2, v, g, z, ln_w, ln_b, r_k, seg, w_out, g1, g2, g3, w1, w2):
    lp = h.shape[0]
    tm = _pick_tile(lp, (256,))
    row_spec = lambda width: pl.BlockSpec((tm, width), lambda i: (i, 0))
    full = lambda arr: pl.BlockSpec(arr.shape, lambda i: (0,) * arr.ndim)
    rw = RWKV_WIDTH
    return pl.pallas_call(
        _even_out_kernel,
        grid=(lp // tm,),
        in_specs=[row_spec(D_MODEL)] + [row_spec(rw)] * 6 +
                 [full(ln_w), full(ln_b), full(r_k), full(seg), _weight_spec(w_out), full(g1), full(g2),
                  full(g3), _weight_spec(w1), _weight_spec(w2)],
        out_specs=row_spec(D_MODEL),
        out_shape=jax.ShapeDtypeStruct((lp, D_MODEL), F32),
        compiler_params=pltpu.CompilerParams(dimension_semantics=("parallel",),
                                             vmem_limit_bytes=VMEM_LIMIT),
        name="even_out",
    )(h, o, r, k2, v, g, z, ln_w, ln_b, r_k, seg, w_out, g1, g2, g3, w1, w2)


def _odd_in_kernel(h_ref, g0_ref, w_ref, wrot_ref, cos_ref, sin_ref, q_out, k_out, v_out, kmax_out):
    @pl.when(pl.program_id(0) == 0)
    def _():
        kmax_out[...] = jnp.zeros(kmax_out.shape, F32)

    hn = _rms(h_ref[...], g0_ref[...], RMS_EPS).astype(BF16)
    y = jnp.dot(hn, w_ref[...], preferred_element_type=F32)
    yr = jnp.dot(hn, wrot_ref[...], preferred_element_type=F32)
    cos = cos_ref[...]
    sin = sin_ref[...]
    scale = DIFF_HEAD ** -0.5 * math.log2(math.e)
    tm = y.shape[0]
    lane = lax.broadcasted_iota(jnp.int32, (tm, LANES), 1)
    minus_one = jnp.where(lane == 0, -1.0, 0.0).astype(BF16)
    ones = jnp.ones((tm, LANES), BF16)
    for j in range(D_MODEL // LANES):
        cols = slice(j * LANES, (j + 1) * LANES)
        kcols = slice(D_MODEL + j * LANES, D_MODEL + (j + 1) * LANES)
        vcols = slice(2 * D_MODEL + j * LANES, 2 * D_MODEL + (j + 1) * LANES)
        wide = slice(2 * j * LANES, (2 * j + 1) * LANES)
        wide_hi = slice((2 * j + 1) * LANES, (2 * j + 2) * LANES)
        q_out[:, cols] = ((y[:, cols] * cos + yr[:, cols] * sin) * scale).astype(BF16)
        kb = (y[:, kcols] * cos + yr[:, kcols] * sin).astype(BF16)
        k_out[:, wide] = kb
        k_out[:, wide_hi] = minus_one
        v_out[:, wide] = y[:, vcols].astype(BF16)
        v_out[:, wide_hi] = ones
        kf = kb.astype(F32)
        knorm = jnp.sqrt(jnp.max(jnp.sum(kf * kf, axis=-1, keepdims=True), axis=0, keepdims=True))
        kmax_out[:, cols] = jnp.maximum(kmax_out[:, cols], jnp.broadcast_to(knorm, (8, LANES)))


def _odd_in(h, g0, w, w_rot, cos, sin):
    lp, lpk = h.shape[0], cos.shape[0]
    tm = _pick_tile(lp, (256,))
    last = lp // tm - 1
    row_spec = lambda width: pl.BlockSpec((tm, width), lambda i: (i, 0))
    full = lambda arr: pl.BlockSpec(arr.shape, lambda i: (0,) * arr.ndim)
    wide = jax.ShapeDtypeStruct((lpk, 2 * D_MODEL), BF16)
    return pl.pallas_call(
        _odd_in_kernel,
        grid=(lpk // tm,),
        in_specs=[pl.BlockSpec((tm, D_MODEL), lambda i: (jnp.minimum(i, last), 0)), full(g0),
                  _weight_spec(w), _weight_spec(w_rot), row_spec(LANES), row_spec(LANES)],
        out_specs=[row_spec(D_MODEL), row_spec(2 * D_MODEL), row_spec(2 * D_MODEL),
                   pl.BlockSpec((8, D_MODEL), lambda i: (0, 0))],
        out_shape=[jax.ShapeDtypeStruct((lpk, D_MODEL), BF16), wide, wide,
                   jax.ShapeDtypeStruct((8, D_MODEL), F32)],
        compiler_params=pltpu.CompilerParams(dimension_semantics=("arbitrary",),
                                             vmem_limit_bytes=VMEM_LIMIT),
        name="odd_in",
    )(h, g0, w, w_rot, cos, sin)


def _attn_kernel(lam_ref, sw_ref, kmax_ref, q_ref, k_ref, v_ref, o_ref, q2_scr, p_scr, acc_scr,
                 *, lam_init, tq, tk, nh):
    i = pl.program_id(1)
    lane = lax.broadcasted_iota(jnp.int32, (1, LANES), 1)
    mlo = lane < HALF
    nt = (((1,), (1,)), ((), ()))
    n_blocks = (i * tq) // tk + 1
    wl = 2 * LANES

    def diagonal_mask(off):
        qrow = lax.broadcasted_iota(jnp.int32, (2 * tq, tk), 0)
        qrow = jnp.where(qrow >= tq, qrow - tq, qrow)
        kcol = lax.broadcasted_iota(jnp.int32, (2 * tq, tk), 1)
        return kcol - qrow <= i * tq - off

    for hd in range(nh):
        q = q_ref[:, hd * LANES:(hd + 1) * LANES]
        zq = jnp.zeros_like(q)
        q2 = jnp.concatenate([jnp.where(mlo, q, zq), jnp.where(mlo, zq, q)], axis=0)
        q2f = q2.astype(F32)
        bound = (jnp.sqrt(jnp.sum(q2f * q2f, axis=-1, keepdims=True))
                 * kmax_ref[0:1, hd * LANES:hd * LANES + 1] * BOUND_SLACK)
        q2_scr[hd] = jnp.concatenate([q2, jnp.where(lane == 0, bound, 0.0).astype(BF16)], axis=1)

    def weights_into(hd, slot, t, diagonal):
        off = pl.multiple_of(t * tk, tk)
        s = lax.dot_general(q2_scr[hd], k_ref[pl.ds(off, tk), hd * wl:(hd + 1) * wl], nt,
                            preferred_element_type=F32)
        if diagonal:
            s = jnp.where(diagonal_mask(off), s, NEG_BIG)
        p_scr[hd, slot] = jnp.exp2(s).astype(BF16)

    def add_weighted_values(hd, slot, t):
        off = pl.multiple_of(t * tk, tk)
        acc_scr[hd] += jnp.dot(p_scr[hd, slot], v_ref[pl.ds(off, tk), hd * wl:(hd + 1) * wl],
                               preferred_element_type=F32)

    def tick(t, slot, diagonal):
        for hd in range(nh):
            weights_into(hd, slot, t, diagonal)
            add_weighted_values(hd, 1 - slot, jnp.maximum(t - 1, 0))

    acc_scr[...] = jnp.zeros(acc_scr.shape, F32)
    for hd in range(nh):
        p_scr[hd, 1] = jnp.zeros(p_scr.shape[2:], BF16)

    def pair(u, carry):
        tick(2 * u, 0, False)
        tick(2 * u + 1, 1, False)
        return carry

    n_pairs = (n_blocks - 1) // 2
    lax.fori_loop(0, n_pairs, pair, 0)
    t0 = 2 * n_pairs

    @pl.when((n_blocks - 1) % 2 == 0)
    def _():
        tick(t0, 0, True)
        for hd in range(nh):
            add_weighted_values(hd, 0, t0)

    @pl.when((n_blocks - 1) % 2 == 1)
    def _():
        tick(t0, 0, False)
        tick(t0 + 1, 1, True)
        for hd in range(nh):
            add_weighted_values(hd, 1, t0 + 1)

    lv = lam_ref[...]
    lam = (jnp.exp(jnp.sum(lv[0:1] * lv[1:2], axis=-1, keepdims=True))
           - jnp.exp(jnp.sum(lv[2:3] * lv[3:4], axis=-1, keepdims=True)) + lam_init)

    def write_out(hd, acc):
        o = acc[:, :LANES] / acc[:, LANES:]
        o = o[:tq] - lam * o[tq:]
        o = _rms(o, sw_ref[...], SUBLN_EPS) * (1.0 - lam_init)
        o_ref[:, hd * LANES:(hd + 1) * LANES] = o.astype(BF16)

    for hd in range(nh):
        write_out(hd, acc_scr[hd])

    smallest = jnp.min(acc_scr[:, :, LANES:LANES + 1])
    @pl.when(jnp.logical_not(smallest >= MIN_ROW_SUM))
    def _():
        for hd in range(nh):
            q2 = q2_scr[hd, :, :LANES]

            def block(t, carry, diagonal):
                m, acc = carry
                off = pl.multiple_of(t * tk, tk)
                s = lax.dot_general(q2, k_ref[pl.ds(off, tk), hd * wl:hd * wl + LANES], nt,
                                    preferred_element_type=F32)
                if diagonal:
                    s = jnp.where(diagonal_mask(off), s, NEG_BIG)
                m_new = jnp.maximum(m, jnp.max(s, axis=-1, keepdims=True))
                p = jnp.exp2(s - m_new).astype(BF16)
                pv = jnp.dot(p, v_ref[pl.ds(off, tk), hd * wl:(hd + 1) * wl], preferred_element_type=F32)
                return m_new, jnp.exp2(m - m_new) * acc + pv

            init = (jnp.full((2 * tq, 1), NEG_BIG, F32), jnp.zeros((2 * tq, wl), F32))
            carry = lax.fori_loop(0, n_blocks - 1, lambda t, c: block(t, c, False), init)
            write_out(hd, block(n_blocks - 1, carry, True)[1])


def _diff_attn(q, k, v, kmax, lam_vecs, subln_w, lam_init, lp):
    tq, tk, nh = ATT_Q_BLOCK, ATT_K_BLOCK, ATT_HEADS
    lpk = k.shape[0]
    blk = pl.BlockSpec((tq, nh * LANES), lambda h, i: (i, h))
    resident = pl.BlockSpec((lpk, 2 * nh * LANES), lambda h, i: (0, h), pipeline_mode=pl.Buffered(1))
    full = lambda arr: pl.BlockSpec(arr.shape, lambda h, i: (0,) * arr.ndim)
    return pl.pallas_call(
        functools.partial(_attn_kernel, lam_init=lam_init, tq=tq, tk=tk, nh=nh),
        grid=(DIFF_HEADS // nh, lp // tq),
        in_specs=[full(lam_vecs), full(subln_w), pl.BlockSpec((8, nh * LANES), lambda h, i: (0, h)),
                  blk, resident, resident],
        out_specs=blk,
        out_shape=jax.ShapeDtypeStruct((lp, D_MODEL), BF16),
        scratch_shapes=[pltpu.VMEM((nh, 2 * tq, 2 * LANES), BF16), pltpu.VMEM((nh, 2, 2 * tq, tk), BF16),
                        pltpu.VMEM((nh, 2 * tq, 2 * LANES), F32)],
        compiler_params=pltpu.CompilerParams(dimension_semantics=("parallel", "arbitrary"),
                                             vmem_limit_bytes=VMEM_LIMIT),
        name="diff_attn",
    )(lam_vecs, subln_w, kmax, q, k, v)


def _odd_out_kernel(h_ref, o_ref, wout_ref, g1_ref, g2_ref, g3_ref, w1_ref, w2_ref, out_ref):
    m = jnp.dot(o_ref[...], wout_ref[...], preferred_element_type=F32)
    hm = h_ref[...] + _rms(m, g1_ref[...], RMS_EPS)
    out_ref[...] = _mlp_residual(hm, g2_ref[...], g3_ref[...], w1_ref, w2_ref)


def _odd_out(h, o, w_out, g1, g2, g3, w1, w2):
    lp = h.shape[0]
    tm = _pick_tile(lp, (256,))
    row_spec = lambda width: pl.BlockSpec((tm, width), lambda i: (i, 0))
    full = lambda arr: pl.BlockSpec(arr.shape, lambda i: (0,) * arr.ndim)
    return pl.pallas_call(
        _odd_out_kernel,
        grid=(lp // tm,),
        in_specs=[row_spec(D_MODEL), row_spec(D_MODEL), _weight_spec(w_out), full(g1), full(g2), full(g3),
                  _weight_spec(w1), _weight_spec(w2)],
        out_specs=row_spec(D_MODEL),
        out_shape=jax.ShapeDtypeStruct((lp, D_MODEL), F32),
        compiler_params=pltpu.CompilerParams(dimension_semantics=("parallel",),
                                             vmem_limit_bytes=VMEM_LIMIT),
        name="odd_out",
    )(h, o, w_out, g1, g2, g3, w1, w2)


def _rotate_half_weights(w):
    d_in, d_out = w.shape
    w4 = w.reshape(d_in, d_out // DIFF_HEAD, 2, DIFF_HEAD // 2)
    return jnp.concatenate([-w4[:, :, 1:2], w4[:, :, 0:1]], axis=2).reshape(d_in, d_out)


def _forward(x, meta, norm_g, mlp_w1, mlp_w2, ev, od):
    seq = x.shape[0]
    length = N_META + seq
    lp = -(-length // ROW_ALIGN) * ROW_ALIGN
    h = jnp.concatenate([meta.astype(x.dtype), x, jnp.zeros((lp - length, D_MODEL), x.dtype)], axis=0)

    lpk = -(-lp // ATT_K_BLOCK) * ATT_K_BLOCK
    pos = jnp.arange(lpk, dtype=F32)
    inv = ROPE_THETA ** (-jnp.arange(0, DIFF_HEAD, 2, dtype=F32) / DIFF_HEAD)
    ang = pos[:, None] * inv[None, :]
    ang = jnp.concatenate([ang, ang, ang, ang], axis=-1)
    cos, sin = jnp.cos(ang), jnp.sin(ang)

    head_id = jnp.arange(RWKV_WIDTH) // RWKV_HEAD
    seg = (head_id[:, None] == head_id[None, :]).astype(BF16)
    row2 = lambda t: t.reshape(1, -1)

    depth = norm_g.shape[0]
    for i in range(depth):
        g = norm_g[i]
        g0, g1, g2, g3 = (row2(g[n]) for n in range(4))
        w1 = mlp_w1[i].astype(BF16)
        w2 = mlp_w2[i].astype(BF16)
        j = i // 2
        if i % 2 == 0:
            (w_in, mu, w0, w_up, a0, a_up, g_up, k_k, k_a, r_k, ln_w, ln_b, pool_w, pool_scale,
             w_out) = (t[j] for t in ev)
            zeros = jnp.zeros((DECAY_RANK, RWKV_WIDTH), F32)
            lora_w = jnp.concatenate([jnp.concatenate([w_up, zeros], axis=1),
                                      jnp.concatenate([zeros, a_up], axis=1)], axis=0).astype(BF16)
            r, k2, v, na, b, logw, gate, z = _even_in(
                h, g0, w_in.astype(BF16), row2(mu), row2(w0), lora_w, row2(a0), g_up.astype(BF16),
                row2(k_k), row2(k_a), seg, pool_w.astype(BF16), row2(pool_scale))
            o = _rwkv_scan(r, k2, v, na, b, logw)
            h = _even_out(h, o, r, k2, v, gate, z, row2(ln_w), row2(ln_b), row2(r_k), seg,
                          w_out.astype(BF16), g1, g2, g3, w1, w2)
        else:
            w_in, lam_vecs, subln_w, w_out = (t[j] for t in od)
            w_rot = _rotate_half_weights(w_in[:, :2 * D_MODEL])
            q, k, v, kmax = _odd_in(h, g0, w_in.astype(BF16), w_rot.astype(BF16), cos, sin)
            lam_init = 0.8 - 0.6 * math.exp(-0.3 * i)
            o = _diff_attn(q, k, v, kmax, lam_vecs, row2(subln_w), lam_init, lp)
            h = _odd_out(h, o, w_out.astype(BF16), g1, g2, g3, w1, w2)
    return h[N_META:length]


def kernel(x, meta, norm_g, mlp_w1, mlp_w2, ev_w_in, ev_mu, ev_w0, ev_w_up, ev_a0, ev_a_up, ev_g_up, ev_k_k,
           ev_k_a, ev_r_k, ev_ln_w, ev_ln_b, ev_pool_w, ev_pool_scale, ev_w_out, od_w_in, od_lambda,
           od_subln_w, od_w_out):
    ev = (ev_w_in, ev_mu, ev_w0, ev_w_up, ev_a0, ev_a_up, ev_g_up, ev_k_k, ev_k_a, ev_r_k, ev_ln_w, ev_ln_b,
          ev_pool_w, ev_pool_scale, ev_w_out)
    od = (od_w_in, od_lambda, od_subln_w, od_w_out)
    outs = [_forward(x[bi], meta, norm_g, mlp_w1, mlp_w2, ev, od) for bi in range(x.shape[0])]
    return jnp.stack(outs, axis=0)
```

```python
import functools
import math

import jax
import jax.numpy as jnp
from jax import lax
from jax.experimental import pallas as pl
from jax.experimental.pallas import tpu as pltpu

F32, BF16 = jnp.float32, jnp.bfloat16

D_MODEL = 1024
N_META = 16
RMS_EPS = 1e-6
D_FF = 4 * D_MODEL
RWKV_HEAD = 64
RWKV_WIDTH = D_MODEL // 2
DECAY_RANK = 64
ICLR_RANK = 64
GATE_RANK = 128
GN_EPS = RWKV_HEAD * 1e-5
POOL_WIDTH = D_MODEL - RWKV_WIDTH
POOL_WINDOWS = (2, 4, 8, 16)
POOL_GROUP_W = POOL_WIDTH // len(POOL_WINDOWS)
POOL_CARRY = 16
SHIFT_WIDTH = 3 * RWKV_WIDTH + DECAY_RANK + ICLR_RANK + GATE_RANK
EVEN_IN = SHIFT_WIDTH + POOL_WIDTH
DIFF_HEADS = 8
DIFF_HEAD = D_MODEL // (2 * DIFF_HEADS)
SUBLN_EPS = 1e-5
ROPE_THETA = 10000.0

LANES = 128
HALF = LANES // 2
ROW_ALIGN = 256
CHUNK = 64
SOLVE_BLOCK = 8
SCAN_CHUNKS = 2
ATT_Q_BLOCK = 256
ATT_K_BLOCK = 1024
ATT_HEADS = 2
FF_CHUNK = 1024
NEG_BIG = -1e30
BOUND_SLACK = 1.02
MIN_ROW_SUM = 2.0 ** -100
VMEM_LIMIT = 56 * 1024 * 1024


def _pick_tile(n, candidates):
    for c in candidates:
        if n % c == 0:
            return c
    raise ValueError(f"no tile in {candidates} divides {n}")


def _rms(t, g, eps):
    return t * lax.rsqrt(jnp.mean(t * t, axis=-1, keepdims=True) + eps) * g


def _split2(x):
    hi = x.astype(BF16)
    lo = (x - hi.astype(F32)).astype(BF16)
    return hi, lo


def _dot(a, b):
    return jnp.dot(a.astype(BF16), b.astype(BF16), preferred_element_type=F32)


def _dot2(a, b):
    ab = a.astype(BF16)
    bh, bl = _split2(b)
    d = functools.partial(jnp.dot, preferred_element_type=F32)
    return d(ab, bh) + d(ab, bl)


def _dot_nt(a, b):
    return lax.dot_general(a.astype(BF16), b.astype(BF16), (((1,), (1,)), ((), ())),
                           preferred_element_type=F32)


def _head_sum(x, seg_ref):
    xh, xl = _split2(x)
    d = functools.partial(jnp.dot, preferred_element_type=F32)
    return d(xh, seg_ref[...]) + d(xl, seg_ref[...])


def _sigmoid(x):
    return 1.0 / (1.0 + jnp.exp(-x))


def _softplus(x):
    return jnp.maximum(x, 0.0) + jnp.log(1.0 + jnp.exp(-jnp.abs(x)))


def _mlp_residual(hm, g2, g3, w1_ref, w2_ref):
    n = _rms(hm, g2, RMS_EPS).astype(BF16)
    acc = jnp.zeros(hm.shape, F32)
    for c in range(D_FF // FF_CHUNK):
        cols = slice(c * FF_CHUNK, (c + 1) * FF_CHUNK)
        a = jnp.dot(n, w1_ref[:, cols], preferred_element_type=F32)
        a = jnp.square(jnp.maximum(a, 0.0)).astype(BF16)
        acc = acc + jnp.dot(a, w2_ref[cols, :], preferred_element_type=F32)
    return hm + _rms(acc, g3, RMS_EPS)


def _even_in_kernel(h_ref, g0_ref, win_ref, mu_ref, w0_ref, lora_ref, a0_ref, gup_ref,
                    kk_ref, ka_ref, seg_ref, poolw_ref, pscale_ref,
                    r_out, k_out, v_out, na_out, b_out, lw_out, g_out, z_out,
                    ycarry, ucarry, *, tm):
    i = pl.program_id(0)

    @pl.when(i == 0)
    def _():
        ycarry[...] = jnp.zeros(ycarry.shape, F32)
        ucarry[...] = jnp.zeros(ucarry.shape, F32)

    hn = _rms(h_ref[...], g0_ref[...], RMS_EPS).astype(BF16)
    y = jnp.dot(hn, win_ref[...], preferred_element_type=F32)

    ysh = y[:, :SHIFT_WIDTH]
    row = lax.broadcasted_iota(jnp.int32, (tm, 1), 0)
    prev = jnp.where(row == 0, ycarry[7:8, :], pltpu.roll(ysh, 1, axis=0))
    ycarry[...] = ysh[tm - 8:, :]
    ys = ysh + (prev - ysh) * mu_ref[...]

    rw = RWKV_WIDTH
    r = ys[:, 0:rw]
    k = ys[:, rw:2 * rw]
    v = ys[:, 2 * rw:3 * rw]
    wa = ys[:, 3 * rw:3 * rw + LANES]
    gd = ys[:, 3 * rw + LANES:SHIFT_WIDTH]

    lane = lax.broadcasted_iota(jnp.int32, (1, LANES), 1)
    lora_in = jnp.where(lane < DECAY_RANK, jnp.tanh(wa), wa)
    lora = _dot(lora_in, lora_ref[...])
    wlog = -_softplus(-(w0_ref[...] + lora[:, :rw])) - 0.5
    logw = -jnp.exp(wlog)
    a = _sigmoid(a0_ref[...] + lora[:, rw:])
    g = _dot(_sigmoid(gd), gup_ref[...])

    kk = k * kk_ref[...]
    kk = kk * lax.rsqrt(jnp.maximum(_head_sum(kk * kk, seg_ref), 1e-24))
    k2 = k * (1.0 + (a - 1.0) * ka_ref[...])

    r_out[...] = r
    k_out[...] = k2
    v_out[...] = v
    na_out[...] = -kk
    b_out[...] = kk * a
    lw_out[...] = logw
    g_out[...] = g

    u = y[:, SHIFT_WIDTH:]
    ext = jnp.concatenate([ucarry[...], u], axis=0)
    ucarry[...] = u[tm - POOL_CARRY:, :]
    t_idx = i * tm + row
    for gi, win in enumerate(POOL_WINDOWS):
        cols = slice(gi * POOL_GROUP_W, (gi + 1) * POOL_GROUP_W)
        s = ext[:, cols]
        span = 1
        while span < win:
            s = s + pltpu.roll(s, span, axis=0)
            span *= 2
        cnt = jnp.minimum(t_idx + 1, win).astype(F32)
        d = s[POOL_CARRY:, :] / cnt - u[:, cols]
        z_out[:, cols] = _dot(d, poolw_ref[gi]) * pscale_ref[:, cols]


def _even_in(h, g0, w_in, mu, w0, lora_w, a0, g_up, k_k, k_a, seg, pool_w, pool_scale):
    lp = h.shape[0]
    tm = _pick_tile(lp, (256,))
    rw = RWKV_WIDTH
    row_spec = lambda width: pl.BlockSpec((tm, width), lambda i: (i, 0))
    full = lambda arr: pl.BlockSpec(arr.shape, lambda i: (0,) * arr.ndim)
    out_sds = jax.ShapeDtypeStruct((lp, rw), F32)
    return pl.pallas_call(
        functools.partial(_even_in_kernel, tm=tm),
        grid=(lp // tm,),
        in_specs=[row_spec(D_MODEL), full(g0), full(w_in), full(mu), full(w0), full(lora_w), full(a0),
                  full(g_up), full(k_k), full(k_a), full(seg), full(pool_w), full(pool_scale)],
        out_specs=[row_spec(rw)] * 8,
        out_shape=[out_sds] * 8,
        scratch_shapes=[pltpu.VMEM((8, SHIFT_WIDTH), F32), pltpu.VMEM((POOL_CARRY, POOL_WIDTH), F32)],
        compiler_params=pltpu.CompilerParams(dimension_semantics=("arbitrary",),
                                             vmem_limit_bytes=VMEM_LIMIT),
        name="even_in",
    )(h, g0, w_in, mu, w0, lora_w, a0, g_up, k_k, k_a, seg, pool_w, pool_scale)


def _scan_kernel(r_ref, k_ref, v_ref, na_ref, b_ref, lw_ref, o_ref, h_scr):
    @pl.when(pl.program_id(0) == 0)
    def _():
        h_scr[...] = jnp.zeros(h_scr.shape, F32)

    c = CHUNK
    n_chunks = r_ref.shape[0] // c
    row = lax.broadcasted_iota(jnp.int32, (c, c), 0)
    col = lax.broadcasted_iota(jnp.int32, (c, c), 1)
    tri = jnp.where(col <= row, 1.0, 0.0).astype(BF16)
    lane = lax.broadcasted_iota(jnp.int32, (1, LANES), 1)
    mlo = lane < HALF
    prow = lax.broadcasted_iota(jnp.int32, (LANES, LANES), 0)
    pcol = lax.broadcasted_iota(jnp.int32, (LANES, LANES), 1)
    same_head = (prow < HALF) == (pcol < HALF)
    diag = prow == pcol
    trow = lax.broadcasted_iota(jnp.int32, (c, LANES), 0)
    tcol = lax.broadcasted_iota(jnp.int32, (c, LANES), 1)
    tcol = jnp.where(tcol >= HALF, tcol - HALF, tcol)
    strict2 = tcol < trow
    incl2 = jnp.concatenate([tcol <= trow] * 2, axis=0)
    same_blk = (tcol // SOLVE_BLOCK) == (trow // SOLVE_BLOCK)
    lane2 = lax.broadcasted_iota(jnp.int32, (1, 2 * LANES), 1)
    zeros_c = jnp.zeros((c, LANES), F32)
    zeros_2c = jnp.zeros((c, 2 * LANES), F32)
    n_pairs = RWKV_WIDTH // LANES
    pair_cols = [slice(p * LANES, (p + 1) * LANES) for p in range(n_pairs)]
    d = functools.partial(jnp.dot, preferred_element_type=F32)

    prep = []
    for ci in range(n_chunks):
        rows = slice(ci * c, (ci + 1) * c)
        lw = lw_ref[rows, :]
        lw_hi = lw.astype(BF16)
        lw_r = lw - lw_hi.astype(F32)
        lw_mid = lw_r.astype(BF16)
        lw_lo = (lw_r - lw_mid.astype(F32)).astype(BF16)
        cum = d(tri, lw_hi) + (d(tri, lw_mid) + d(tri, lw_lo))
        cum_end = cum[c - 1:c, :]
        e_neg = jnp.exp(-cum)
        e_end = jnp.exp(cum_end - cum)
        b_all = b_ref[rows, :]
        k_all = k_ref[rows, :]
        prep.append(dict(r_t=r_ref[rows, :] * jnp.exp(cum), a_t=na_ref[rows, :] * jnp.exp(cum - lw),
                         b_t=b_all * e_neg, k_t=k_all * e_neg, b_h=b_all * e_end, k_h=k_all * e_end,
                         v=v_ref[rows, :], p_end=jnp.exp(cum_end)))
    units = [(ci, p) for ci in range(n_chunks) for p in range(n_pairs)]

    a_all = {}
    for ci, p in units:
        cols = pair_cols[p]
        rt, at = prep[ci]["r_t"][:, cols], prep[ci]["a_t"][:, cols]
        lhs4 = jnp.concatenate([jnp.where(mlo, at, zeros_c), jnp.where(mlo, zeros_c, at),
                                jnp.where(mlo, rt, zeros_c), jnp.where(mlo, zeros_c, rt)], axis=0)
        a_all[ci, p] = _dot_nt(lhs4, jnp.concatenate([prep[ci]["b_t"][:, cols], prep[ci]["k_t"][:, cols]], axis=0))

    heads = []
    for ci, p in units:
        cols = pair_cols[p]
        at, vp = prep[ci]["a_t"][:, cols], prep[ci]["v"][:, cols]
        at_sw = pltpu.roll(at, HALF, axis=1)
        vp_sw = pltpu.roll(vp, HALF, axis=1)
        for hh in range(2):
            nk = jnp.where(strict2, a_all[ci, p][hh * c:(hh + 1) * c], 0.0)
            av = _dot(nk, jnp.concatenate([zeros_c, vp_sw if hh == 0 else vp], axis=0))
            x0 = jnp.where(mlo, at if hh == 0 else at_sw, av)
            nk_sw = pltpu.roll(nk, HALF, axis=1)
            n_split = jnp.where(mlo, jnp.where(same_blk, 0.0, nk), jnp.where(same_blk, nk_sw, 0.0))
            heads.append(jnp.concatenate([x0, n_split], axis=1))

    for _ in range(int(math.log2(SOLVE_BLOCK))):
        nxt = []
        for y in heads:
            prod = _dot2(y[:, LANES:], jnp.concatenate([zeros_2c, y], axis=0))
            nxt.append(jnp.where(lane2 >= LANES + HALF, prod, y + prod))
        heads = nxt
    for _ in range(int(math.log2(c // SOLVE_BLOCK))):
        nxt = []
        for y in heads:
            prod = _dot2(y[:, LANES:], jnp.concatenate([y, zeros_2c], axis=0))
            nxt.append(jnp.where(lane2 < LANES, y + prod, prod))
        heads = nxt

    big = {}
    for n, (ci, p) in enumerate(units):
        cols = pair_cols[p]
        x_lo, x_hi = heads[2 * n][:, :LANES], heads[2 * n + 1][:, :LANES]
        w_p = jnp.where(mlo, x_lo, pltpu.roll(x_hi, HALF, axis=1))
        u0_p = jnp.where(mlo, pltpu.roll(x_lo, HALF, axis=1), x_hi)
        rhs = jnp.concatenate([jnp.concatenate([w_p, u0_p], axis=1),
                               jnp.concatenate([zeros_c, prep[ci]["v"][:, cols]], axis=1)], axis=0)
        a_r = jnp.where(incl2, a_all[ci, p][2 * c:], 0.0)
        bk_t = jnp.concatenate([prep[ci]["b_h"][:, cols], prep[ci]["k_h"][:, cols]], axis=0).T
        big[ci, p] = _dot(jnp.concatenate([a_r, bk_t], axis=0), rhs)

    state = [h_scr[p] for p in range(n_pairs)]
    for ci in range(n_chunks):
        rows = slice(ci * c, (ci + 1) * c)
        for p, cols in enumerate(pair_cols):
            res = big[ci, p]
            q_hat = prep[ci]["r_t"][:, cols] + jnp.where(mlo, res[:c, :LANES], res[c:2 * c, :LANES])
            o_hat = jnp.where(mlo, res[:c, LANES:], res[c:2 * c, LANES:])
            g_mat = (jnp.where(same_head, res[2 * c:, :LANES], 0.0)
                     + jnp.where(diag, prep[ci]["p_end"][:, cols], 0.0))
            j_mat = jnp.where(same_head, res[2 * c:, LANES:], 0.0)
            st = _dot2(jnp.concatenate([q_hat, g_mat], axis=0), state[p])
            o_ref[rows, cols] = st[:c] + o_hat
            state[p] = st[c:] + j_mat
    for p in range(n_pairs):
        h_scr[p] = state[p]


def _rwkv_scan(r, k2, v, na, b, logw):
    lp, rw = r.shape
    rows = CHUNK * SCAN_CHUNKS
    spec = pl.BlockSpec((rows, rw), lambda i: (i, 0))
    return pl.pallas_call(
        _scan_kernel,
        grid=(lp // rows,),
        in_specs=[spec] * 6,
        out_specs=spec,
        out_shape=jax.ShapeDtypeStruct((lp, rw), F32),
        scratch_shapes=[pltpu.VMEM((rw // LANES, LANES, LANES), F32)],
        compiler_params=pltpu.CompilerParams(dimension_semantics=("arbitrary",),
                                             vmem_limit_bytes=VMEM_LIMIT),
        name="rwkv_scan",
    )(r, k2, v, na, b, logw)


def _even_out_kernel(h_ref, o_ref, r_ref, k_ref, v_ref, g_ref, z_ref, lnw_ref, lnb_ref, rk_ref, seg_ref,
                     wout_ref, g1_ref, g2_ref, g3_ref, w1_ref, w2_ref, out_ref):
    inv_n = 1.0 / RWKV_HEAD
    o = o_ref[...]
    mean = _head_sum(o, seg_ref) * inv_n
    dev = o - mean
    var = _head_sum(dev * dev, seg_ref) * inv_n
    on = dev * lax.rsqrt(var + GN_EPS) * lnw_ref[...] + lnb_ref[...]
    bonus = _head_sum(r_ref[...] * k_ref[...] * rk_ref[...], seg_ref) * v_ref[...]
    om = (on + bonus) * g_ref[...]
    rw = RWKV_WIDTH
    m = _dot(om, wout_ref[:rw, :]) + _dot(z_ref[...], wout_ref[rw:, :])
    hm = h_ref[...] + _rms(m, g1_ref[...], RMS_EPS)
    out_ref[...] = _mlp_residual(hm, g2_ref[...], g3_ref[...], w1_ref, w2_ref)


def _weight_spec(arr):
    return pl.BlockSpec(arr.shape, lambda i: (0,) * arr.ndim, pipeline_mode=pl.Buffered(1))


def _even_out(h, o, r, k2, v, g, z, ln_w, ln_b, r_k, seg, w_out, g1, g2, g3, w1, w2):
    lp = h.shape[0]
    tm = _pick_tile(lp, (256,))
    row_spec = lambda width: pl.BlockSpec((tm, width), lambda i: (i, 0))
    full = lambda arr: pl.BlockSpec(arr.shape, lambda i: (0,) * arr.ndim)
    rw = RWKV_WIDTH
    return pl.pallas_call(
        _even_out_kernel,
        grid=(lp // tm,),
        in_specs=[row_spec(D_MODEL)] + [row_spec(rw)] * 6 +
                 [full(ln_w), full(ln_b), full(r_k), full(seg), _weight_spec(w_out), full(g1), full(g2),
                  full(g3), _weight_spec(w1), _weight_spec(w2)],
        out_specs=row_spec(D_MODEL),
        out_shape=jax.ShapeDtypeStruct((lp, D_MODEL), F32),
        compiler_params=pltpu.CompilerParams(dimension_semantics=("parallel",),
                                             vmem_limit_bytes=VMEM_LIMIT),
        name="even_out",
    )(h, o, r, k2, v, g, z, ln_w, ln_b, r_k, seg, w_out, g1, g2, g3, w1, w2)


def _odd_in_kernel(h_ref, g0_ref, w_ref, wrot_ref, cos_ref, sin_ref, q_out, k_out, v_out, kmax_out):
    @pl.when(pl.program_id(0) == 0)
    def _():
        kmax_out[...] = jnp.zeros(kmax_out.shape, F32)

    hn = _rms(h_ref[...], g0_ref[...], RMS_EPS).astype(BF16)
    y = jnp.dot(hn, w_ref[...], preferred_element_type=F32)
    yr = jnp.dot(hn, wrot_ref[...], preferred_element_type=F32)
    cos = cos_ref[...]
    sin = sin_ref[...]
    scale = DIFF_HEAD ** -0.5 * math.log2(math.e)
    tm = y.shape[0]
    lane = lax.broadcasted_iota(jnp.int32, (tm, LANES), 1)
    minus_one = jnp.where(lane == 0, -1.0, 0.0).astype(BF16)
    ones = jnp.ones((tm, LANES), BF16)
    for j in range(D_MODEL // LANES):
        cols = slice(j * LANES, (j + 1) * LANES)
        kcols = slice(D_MODEL + j * LANES, D_MODEL + (j + 1) * LANES)
        vcols = slice(2 * D_MODEL + j * LANES, 2 * D_MODEL + (j + 1) * LANES)
        wide = slice(2 * j * LANES, (2 * j + 1) * LANES)
        wide_hi = slice((2 * j + 1) * LANES, (2 * j + 2) * LANES)
        q_out[:, cols] = ((y[:, cols] * cos + yr[:, cols] * sin) * scale).astype(BF16)
        kb = (y[:, kcols] * cos + yr[:, kcols] * sin).astype(BF16)
        k_out[:, wide] = kb
        k_out[:, wide_hi] = minus_one
        v_out[:, wide] = y[:, vcols].astype(BF16)
        v_out[:, wide_hi] = ones
        kf = kb.astype(F32)
        knorm = jnp.sqrt(jnp.max(jnp.sum(kf * kf, axis=-1, keepdims=True), axis=0, keepdims=True))
        kmax_out[:, cols] = jnp.maximum(kmax_out[:, cols], jnp.broadcast_to(knorm, (8, LANES)))


def _odd_in(h, g0, w, w_rot, cos, sin):
    lp, lpk = h.shape[0], cos.shape[0]
    tm = _pick_tile(lp, (256,))
    last = lp // tm - 1
    row_spec = lambda width: pl.BlockSpec((tm, width), lambda i: (i, 0))
    full = lambda arr: pl.BlockSpec(arr.shape, lambda i: (0,) * arr.ndim)
    wide = jax.ShapeDtypeStruct((lpk, 2 * D_MODEL), BF16)
    return pl.pallas_call(
        _odd_in_kernel,
        grid=(lpk // tm,),
        in_specs=[pl.BlockSpec((tm, D_MODEL), lambda i: (jnp.minimum(i, last), 0)), full(g0),
                  _weight_spec(w), _weight_spec(w_rot), row_spec(LANES), row_spec(LANES)],
        out_specs=[row_spec(D_MODEL), row_spec(2 * D_MODEL), row_spec(2 * D_MODEL),
                   pl.BlockSpec((8, D_MODEL), lambda i: (0, 0))],
        out_shape=[jax.ShapeDtypeStruct((lpk, D_MODEL), BF16), wide, wide,
                   jax.ShapeDtypeStruct((8, D_MODEL), F32)],
        compiler_params=pltpu.CompilerParams(dimension_semantics=("arbitrary",),
                                             vmem_limit_bytes=VMEM_LIMIT),
        name="odd_in",
    )(h, g0, w, w_rot, cos, sin)


def _attn_kernel(lam_ref, sw_ref, kmax_ref, q_ref, k_ref, v_ref, o_ref, q2_scr, p_scr, acc_scr,
                 *, lam_init, tq, tk, nh):
    i = pl.program_id(1)
    lane = lax.broadcasted_iota(jnp.int32, (1, LANES), 1)
    mlo = lane < HALF
    nt = (((1,), (1,)), ((), ()))
    n_blocks = (i * tq) // tk + 1
    wl = 2 * LANES

    def diagonal_mask(off):
        qrow = lax.broadcasted_iota(jnp.int32, (2 * tq, tk), 0)
        qrow = jnp.where(qrow >= tq, qrow - tq, qrow)
        kcol = lax.broadcasted_iota(jnp.int32, (2 * tq, tk), 1)
        return kcol - qrow <= i * tq - off

    for hd in range(nh):
        q = q_ref[:, hd * LANES:(hd + 1) * LANES]
        zq = jnp.zeros_like(q)
        q2 = jnp.concatenate([jnp.where(mlo, q, zq), jnp.where(mlo, zq, q)], axis=0)
        q2f = q2.astype(F32)
        bound = (jnp.sqrt(jnp.sum(q2f * q2f, axis=-1, keepdims=True))
                 * kmax_ref[0:1, hd * LANES:hd * LANES + 1] * BOUND_SLACK)
        q2_scr[hd] = jnp.concatenate([q2, jnp.where(lane == 0, bound, 0.0).astype(BF16)], axis=1)

    def weights_into(hd, slot, t, diagonal):
        off = pl.multiple_of(t * tk, tk)
        s = lax.dot_general(q2_scr[hd], k_ref[pl.ds(off, tk), hd * wl:(hd + 1) * wl], nt,
                            preferred_element_type=F32)
        if diagonal:
            s = jnp.where(diagonal_mask(off), s, NEG_BIG)
        p_scr[hd, slot] = jnp.exp2(s).astype(BF16)

    def add_weighted_values(hd, slot, t):
        off = pl.multiple_of(t * tk, tk)
        acc_scr[hd] += jnp.dot(p_scr[hd, slot], v_ref[pl.ds(off, tk), hd * wl:(hd + 1) * wl],
                               preferred_element_type=F32)

    def tick(t, slot, diagonal):
        for hd in range(nh):
            weights_into(hd, slot, t, diagonal)
            add_weighted_values(hd, 1 - slot, t - 1)

    def drain(slot, t):
        for hd in range(nh):
            add_weighted_values(hd, slot, t)

    acc_scr[...] = jnp.zeros(acc_scr.shape, F32)

    @pl.when(n_blocks == 1)
    def _():
        for hd in range(nh):
            weights_into(hd, 0, 0, True)
        drain(0, 0)

    @pl.when(n_blocks > 1)
    def _():
        for hd in range(nh):
            weights_into(hd, 0, 0, False)

    def pair(u, carry):
        tick(2 * u + 1, 1, False)
        tick(2 * u + 2, 0, False)
        return carry

    n_pairs = jnp.maximum(n_blocks - 2, 0) // 2
    lax.fori_loop(0, n_pairs, pair, 0)
    t1 = 2 * n_pairs + 1

    @pl.when(jnp.logical_and(n_blocks > 1, n_blocks % 2 == 0))
    def _():
        tick(t1, 1, True)
        drain(1, t1)

    @pl.when(jnp.logical_and(n_blocks > 1, n_blocks % 2 == 1))
    def _():
        tick(t1, 1, False)
        tick(t1 + 1, 0, True)
        drain(0, t1 + 1)

    lv = lam_ref[...]
    lam = (jnp.exp(jnp.sum(lv[0:1] * lv[1:2], axis=-1, keepdims=True))
           - jnp.exp(jnp.sum(lv[2:3] * lv[3:4], axis=-1, keepdims=True)) + lam_init)

    def write_out(hd, acc):
        o = acc[:, :LANES] / acc[:, LANES:]
        o = o[:tq] - lam * o[tq:]
        o = _rms(o, sw_ref[...], SUBLN_EPS) * (1.0 - lam_init)
        o_ref[:, hd * LANES:(hd + 1) * LANES] = o.astype(BF16)

    for hd in range(nh):
        write_out(hd, acc_scr[hd])

    smallest = jnp.min(acc_scr[:, :, LANES:LANES + 1])
    @pl.when(jnp.logical_not(smallest >= MIN_ROW_SUM))
    def _():
        for hd in range(nh):
            q2 = q2_scr[hd, :, :LANES]

            def block(t, carry, diagonal):
                m, acc = carry
                off = pl.multiple_of(t * tk, tk)
                s = lax.dot_general(q2, k_ref[pl.ds(off, tk), hd * wl:hd * wl + LANES], nt,
                                    preferred_element_type=F32)
                if diagonal:
                    s = jnp.where(diagonal_mask(off), s, NEG_BIG)
                m_new = jnp.maximum(m, jnp.max(s, axis=-1, keepdims=True))
                p = jnp.exp2(s - m_new).astype(BF16)
                pv = jnp.dot(p, v_ref[pl.ds(off, tk), hd * wl:(hd + 1) * wl], preferred_element_type=F32)
                return m_new, jnp.exp2(m - m_new) * acc + pv

            init = (jnp.full((2 * tq, 1), NEG_BIG, F32), jnp.zeros((2 * tq, wl), F32))
            carry = lax.fori_loop(0, n_blocks - 1, lambda t, c: block(t, c, False), init)
            write_out(hd, block(n_blocks - 1, carry, True)[1])


def _diff_attn(q, k, v, kmax, lam_vecs, subln_w, lam_init, lp):
    tq, tk, nh = ATT_Q_BLOCK, ATT_K_BLOCK, ATT_HEADS
    lpk = k.shape[0]
    blk = pl.BlockSpec((tq, nh * LANES), lambda h, i: (i, h))
    resident = pl.BlockSpec((lpk, 2 * nh * LANES), lambda h, i: (0, h), pipeline_mode=pl.Buffered(1))
    full = lambda arr: pl.BlockSpec(arr.shape, lambda h, i: (0,) * arr.ndim)
    return pl.pallas_call(
        functools.partial(_attn_kernel, lam_init=lam_init, tq=tq, tk=tk, nh=nh),
        grid=(DIFF_HEADS // nh, lp // tq),
        in_specs=[full(lam_vecs), full(subln_w), pl.BlockSpec((8, nh * LANES), lambda h, i: (0, h)),
                  blk, resident, resident],
        out_specs=blk,
        out_shape=jax.ShapeDtypeStruct((lp, D_MODEL), BF16),
        scratch_shapes=[pltpu.VMEM((nh, 2 * tq, 2 * LANES), BF16), pltpu.VMEM((nh, 2, 2 * tq, tk), BF16),
                        pltpu.VMEM((nh, 2 * tq, 2 * LANES), F32)],
        compiler_params=pltpu.CompilerParams(dimension_semantics=("parallel", "arbitrary"),
                                             vmem_limit_bytes=VMEM_LIMIT),
        name="diff_attn",
    )(lam_vecs, subln_w, kmax, q, k, v)


def _odd_out_kernel(h_ref, o_ref, wout_ref, g1_ref, g2_ref, g3_ref, w1_ref, w2_ref, out_ref):
    m = jnp.dot(o_ref[...], wout_ref[...], preferred_element_type=F32)
    hm = h_ref[...] + _rms(m, g1_ref[...], RMS_EPS)
    out_ref[...] = _mlp_residual(hm, g2_ref[...], g3_ref[...], w1_ref, w2_ref)


def _odd_out(h, o, w_out, g1, g2, g3, w1, w2):
    lp = h.shape[0]
    tm = _pick_tile(lp, (256,))
    row_spec = lambda width: pl.BlockSpec((tm, width), lambda i: (i, 0))
    full = lambda arr: pl.BlockSpec(arr.shape, lambda i: (0,) * arr.ndim)
    return pl.pallas_call(
        _odd_out_kernel,
        grid=(lp // tm,),
        in_specs=[row_spec(D_MODEL), row_spec(D_MODEL), _weight_spec(w_out), full(g1), full(g2), full(g3),
                  _weight_spec(w1), _weight_spec(w2)],
        out_specs=row_spec(D_MODEL),
        out_shape=jax.ShapeDtypeStruct((lp, D_MODEL), F32),
        compiler_params=pltpu.CompilerParams(dimension_semantics=("parallel",),
                                             vmem_limit_bytes=VMEM_LIMIT),
        name="odd_out",
    )(h, o, w_out, g1, g2, g3, w1, w2)


def _rotate_half_weights(w):
    d_in, d_out = w.shape
    w4 = w.reshape(d_in, d_out // DIFF_HEAD, 2, DIFF_HEAD // 2)
    return jnp.concatenate([-w4[:, :, 1:2], w4[:, :, 0:1]], axis=2).reshape(d_in, d_out)


def _forward(x, meta, norm_g, mlp_w1, mlp_w2, ev, od):
    seq = x.shape[0]
    length = N_META + seq
    lp = -(-length // ROW_ALIGN) * ROW_ALIGN
    h = jnp.concatenate([meta.astype(x.dtype), x, jnp.zeros((lp - length, D_MODEL), x.dtype)], axis=0)

    lpk = -(-lp // ATT_K_BLOCK) * ATT_K_BLOCK
    pos = jnp.arange(lpk, dtype=F32)
    inv = ROPE_THETA ** (-jnp.arange(0, DIFF_HEAD, 2, dtype=F32) / DIFF_HEAD)
    ang = pos[:, None] * inv[None, :]
    ang = jnp.concatenate([ang, ang, ang, ang], axis=-1)
    cos, sin = jnp.cos(ang), jnp.sin(ang)

    head_id = jnp.arange(RWKV_WIDTH) // RWKV_HEAD
    seg = (head_id[:, None] == head_id[None, :]).astype(BF16)
    row2 = lambda t: t.reshape(1, -1)

    depth = norm_g.shape[0]
    for i in range(depth):
        g = norm_g[i]
        g0, g1, g2, g3 = (row2(g[n]) for n in range(4))
        w1 = mlp_w1[i].astype(BF16)
        w2 = mlp_w2[i].astype(BF16)
        j = i // 2
        if i % 2 == 0:
            (w_in, mu, w0, w_up, a0, a_up, g_up, k_k, k_a, r_k, ln_w, ln_b, pool_w, pool_scale,
             w_out) = (t[j] for t in ev)
            zeros = jnp.zeros((DECAY_RANK, RWKV_WIDTH), F32)
            lora_w = jnp.concatenate([jnp.concatenate([w_up, zeros], axis=1),
                                      jnp.concatenate([zeros, a_up], axis=1)], axis=0).astype(BF16)
            r, k2, v, na, b, logw, gate, z = _even_in(
                h, g0, w_in.astype(BF16), row2(mu), row2(w0), lora_w, row2(a0), g_up.astype(BF16),
                row2(k_k), row2(k_a), seg, pool_w.astype(BF16), row2(pool_scale))
            o = _rwkv_scan(r, k2, v, na, b, logw)
            h = _even_out(h, o, r, k2, v, gate, z, row2(ln_w), row2(ln_b), row2(r_k), seg,
                          w_out.astype(BF16), g1, g2, g3, w1, w2)
        else:
            w_in, lam_vecs, subln_w, w_out = (t[j] for t in od)
            w_rot = _rotate_half_weights(w_in[:, :2 * D_MODEL])
            q, k, v, kmax = _odd_in(h, g0, w_in.astype(BF16), w_rot.astype(BF16), cos, sin)
            lam_init = 0.8 - 0.6 * math.exp(-0.3 * i)
            o = _diff_attn(q, k, v, kmax, lam_vecs, row2(subln_w), lam_init, lp)
            h = _odd_out(h, o, w_out.astype(BF16), g1, g2, g3, w1, w2)
    return h[N_META:length]


def kernel(x, meta, norm_g, mlp_w1, mlp_w2, ev_w_in, ev_mu, ev_w0, ev_w_up, ev_a0, ev_a_up, ev_g_up, ev_k_k,
           ev_k_a, ev_r_k, ev_ln_w, ev_ln_b, ev_pool_w, ev_pool_scale, ev_w_out, od_w_in, od_lambda,
           od_subln_w, od_w_out):
    ev = (ev_w_in, ev_mu, ev_w0, ev_w_up, ev_a0, ev_a_up, ev_g_up, ev_k_k, ev_k_a, ev_r_k, ev_ln_w, ev_ln_b,
          ev_pool_w, ev_pool_scale, ev_w_out)
    od = (od_w_in, od_lambda, od_subln_w, od_w_out)
    outs = [_forward(x[bi], meta, norm_g, mlp_w1, mlp_w2, ev, od) for bi in range(x.shape[0])]
    return jnp.stack(outs, axis=0)
```

```python
import functools
import math

import jax
import jax.numpy as jnp
from jax import lax
from jax.experimental import pallas as pl
from jax.experimental.pallas import tpu as pltpu

F32, BF16 = jnp.float32, jnp.bfloat16

D_MODEL = 1024
N_META = 16
RMS_EPS = 1e-6
D_FF = 4 * D_MODEL
RWKV_HEAD = 64
RWKV_WIDTH = D_MODEL // 2
DECAY_RANK = 64
ICLR_RANK = 64
GATE_RANK = 128
GN_EPS = RWKV_HEAD * 1e-5
POOL_WIDTH = D_MODEL - RWKV_WIDTH
POOL_WINDOWS = (2, 4, 8, 16)
POOL_GROUP_W = POOL_WIDTH // len(POOL_WINDOWS)
POOL_CARRY = 16
SHIFT_WIDTH = 3 * RWKV_WIDTH + DECAY_RANK + ICLR_RANK + GATE_RANK
EVEN_IN = SHIFT_WIDTH + POOL_WIDTH
DIFF_HEADS = 8
DIFF_HEAD = D_MODEL // (2 * DIFF_HEADS)
SUBLN_EPS = 1e-5
ROPE_THETA = 10000.0

LANES = 128
HALF = LANES // 2
ROW_ALIGN = 256
CHUNK = 64
SOLVE_BLOCK = 8
SCAN_CHUNKS = 2
ATT_Q_BLOCK = 256
ATT_K_BLOCK = 1024
ATT_HEADS = 2
FF_CHUNK = 1024
NEG_BIG = -1e30
BOUND_SLACK = 1.02
MIN_ROW_SUM = 2.0 ** -100
VMEM_LIMIT = 56 * 1024 * 1024


def _pick_tile(n, candidates):
    for c in candidates:
        if n % c == 0:
            return c
    raise ValueError(f"no tile in {candidates} divides {n}")


def _rms(t, g, eps):
    return t * lax.rsqrt(jnp.mean(t * t, axis=-1, keepdims=True) + eps) * g


def _split2(x):
    hi = x.astype(BF16)
    lo = (x - hi.astype(F32)).astype(BF16)
    return hi, lo


def _dot(a, b):
    return jnp.dot(a.astype(BF16), b.astype(BF16), preferred_element_type=F32)


def _dot2(a, b):
    ab = a.astype(BF16)
    bh, bl = _split2(b)
    d = functools.partial(jnp.dot, preferred_element_type=F32)
    return d(ab, bh) + d(ab, bl)


def _dot_nt(a, b):
    return lax.dot_general(a.astype(BF16), b.astype(BF16), (((1,), (1,)), ((), ())),
                           preferred_element_type=F32)


def _head_sum(x, seg_ref):
    xh, xl = _split2(x)
    d = functools.partial(jnp.dot, preferred_element_type=F32)
    return d(xh, seg_ref[...]) + d(xl, seg_ref[...])


def _sigmoid(x):
    return 1.0 / (1.0 + jnp.exp(-x))


def _softplus(x):
    return jnp.maximum(x, 0.0) + jnp.log(1.0 + jnp.exp(-jnp.abs(x)))


def _mlp_residual(hm, g2, g3, w1_ref, w2_ref):
    n = _rms(hm, g2, RMS_EPS).astype(BF16)
    acc = jnp.zeros(hm.shape, F32)
    for c in range(D_FF // FF_CHUNK):
        cols = slice(c * FF_CHUNK, (c + 1) * FF_CHUNK)
        a = jnp.dot(n, w1_ref[:, cols], preferred_element_type=F32)
        a = jnp.square(jnp.maximum(a, 0.0)).astype(BF16)
        acc = acc + jnp.dot(a, w2_ref[cols, :], preferred_element_type=F32)
    return hm + _rms(acc, g3, RMS_EPS)


def _even_in_kernel(h_ref, g0_ref, win_ref, mu_ref, w0_ref, lora_ref, a0_ref, gup_ref,
                    kk_ref, ka_ref, seg_ref, poolw_ref, pscale_ref,
                    r_out, k_out, v_out, na_out, b_out, lw_out, g_out, z_out,
                    ycarry, ucarry, *, tm):
    i = pl.program_id(0)

    @pl.when(i == 0)
    def _():
        ycarry[...] = jnp.zeros(ycarry.shape, F32)
        ucarry[...] = jnp.zeros(ucarry.shape, F32)

    hn = _rms(h_ref[...], g0_ref[...], RMS_EPS).astype(BF16)
    y = jnp.dot(hn, win_ref[...], preferred_element_type=F32)

    ysh = y[:, :SHIFT_WIDTH]
    row = lax.broadcasted_iota(jnp.int32, (tm, 1), 0)
    prev = jnp.where(row == 0, ycarry[7:8, :], pltpu.roll(ysh, 1, axis=0))
    ycarry[...] = ysh[tm - 8:, :]
    ys = ysh + (prev - ysh) * mu_ref[...]

    rw = RWKV_WIDTH
    r = ys[:, 0:rw]
    k = ys[:, rw:2 * rw]
    v = ys[:, 2 * rw:3 * rw]
    wa = ys[:, 3 * rw:3 * rw + LANES]
    gd = ys[:, 3 * rw + LANES:SHIFT_WIDTH]

    lane = lax.broadcasted_iota(jnp.int32, (1, LANES), 1)
    lora_in = jnp.where(lane < DECAY_RANK, jnp.tanh(wa), wa)
    lora = _dot(lora_in, lora_ref[...])
    wlog = -_softplus(-(w0_ref[...] + lora[:, :rw])) - 0.5
    logw = -jnp.exp(wlog)
    a = _sigmoid(a0_ref[...] + lora[:, rw:])
    g = _dot(_sigmoid(gd), gup_ref[...])

    kk = k * kk_ref[...]
    kk = kk * lax.rsqrt(jnp.maximum(_head_sum(kk * kk, seg_ref), 1e-24))
    k2 = k * (1.0 + (a - 1.0) * ka_ref[...])

    r_out[...] = r
    k_out[...] = k2
    v_out[...] = v
    na_out[...] = -kk
    b_out[...] = kk * a
    lw_out[...] = logw
    g_out[...] = g

    u = y[:, SHIFT_WIDTH:]
    ext = jnp.concatenate([ucarry[...], u], axis=0)
    ucarry[...] = u[tm - POOL_CARRY:, :]
    t_idx = i * tm + row
    for gi, win in enumerate(POOL_WINDOWS):
        cols = slice(gi * POOL_GROUP_W, (gi + 1) * POOL_GROUP_W)
        s = ext[:, cols]
        span = 1
        while span < win:
            s = s + pltpu.roll(s, span, axis=0)
            span *= 2
        cnt = jnp.minimum(t_idx + 1, win).astype(F32)
        d = s[POOL_CARRY:, :] / cnt - u[:, cols]
        z_out[:, cols] = _dot(d, poolw_ref[gi]) * pscale_ref[:, cols]


def _even_in(h, g0, w_in, mu, w0, lora_w, a0, g_up, k_k, k_a, seg, pool_w, pool_scale):
    lp = h.shape[0]
    tm = _pick_tile(lp, (640, 256))
    rw = RWKV_WIDTH
    row_spec = lambda width: pl.BlockSpec((tm, width), lambda i: (i, 0))
    full = lambda arr: pl.BlockSpec(arr.shape, lambda i: (0,) * arr.ndim)
    out_sds = jax.ShapeDtypeStruct((lp, rw), F32)
    return pl.pallas_call(
        functools.partial(_even_in_kernel, tm=tm),
        grid=(lp // tm,),
        in_specs=[row_spec(D_MODEL), full(g0), full(w_in), full(mu), full(w0), full(lora_w), full(a0),
                  full(g_up), full(k_k), full(k_a), full(seg), full(pool_w), full(pool_scale)],
        out_specs=[row_spec(rw)] * 8,
        out_shape=[out_sds] * 8,
        scratch_shapes=[pltpu.VMEM((8, SHIFT_WIDTH), F32), pltpu.VMEM((POOL_CARRY, POOL_WIDTH), F32)],
        compiler_params=pltpu.CompilerParams(dimension_semantics=("arbitrary",),
                                             vmem_limit_bytes=VMEM_LIMIT),
        name="even_in",
    )(h, g0, w_in, mu, w0, lora_w, a0, g_up, k_k, k_a, seg, pool_w, pool_scale)


def _scan_kernel(r_ref, k_ref, v_ref, na_ref, b_ref, lw_ref, o_ref, h_scr):
    @pl.when(pl.program_id(0) == 0)
    def _():
        h_scr[...] = jnp.zeros(h_scr.shape, F32)

    c = CHUNK
    n_chunks = r_ref.shape[0] // c
    row = lax.broadcasted_iota(jnp.int32, (c, c), 0)
    col = lax.broadcasted_iota(jnp.int32, (c, c), 1)
    tri = jnp.where(col <= row, 1.0, 0.0).astype(BF16)
    lane = lax.broadcasted_iota(jnp.int32, (1, LANES), 1)
    mlo = lane < HALF
    prow = lax.broadcasted_iota(jnp.int32, (LANES, LANES), 0)
    pcol = lax.broadcasted_iota(jnp.int32, (LANES, LANES), 1)
    same_head = (prow < HALF) == (pcol < HALF)
    diag = prow == pcol
    trow = lax.broadcasted_iota(jnp.int32, (c, LANES), 0)
    tcol = lax.broadcasted_iota(jnp.int32, (c, LANES), 1)
    tcol = jnp.where(tcol >= HALF, tcol - HALF, tcol)
    strict2 = tcol < trow
    incl2 = jnp.concatenate([tcol <= trow] * 2, axis=0)
    same_blk = (tcol // SOLVE_BLOCK) == (trow // SOLVE_BLOCK)
    lane2 = lax.broadcasted_iota(jnp.int32, (1, 2 * LANES), 1)
    zeros_c = jnp.zeros((c, LANES), F32)
    zeros_2c = jnp.zeros((c, 2 * LANES), F32)
    n_pairs = RWKV_WIDTH // LANES
    pair_cols = [slice(p * LANES, (p + 1) * LANES) for p in range(n_pairs)]
    d = functools.partial(jnp.dot, preferred_element_type=F32)

    prep = []
    for ci in range(n_chunks):
        rows = slice(ci * c, (ci + 1) * c)
        lw = lw_ref[rows, :]
        lw_hi = lw.astype(BF16)
        lw_r = lw - lw_hi.astype(F32)
        lw_mid = lw_r.astype(BF16)
        lw_lo = (lw_r - lw_mid.astype(F32)).astype(BF16)
        cum = d(tri, lw_hi) + (d(tri, lw_mid) + d(tri, lw_lo))
        cum_end = cum[c - 1:c, :]
        e_neg = jnp.exp(-cum)
        e_end = jnp.exp(cum_end - cum)
        b_all = b_ref[rows, :]
        k_all = k_ref[rows, :]
        prep.append(dict(r_t=r_ref[rows, :] * jnp.exp(cum), a_t=na_ref[rows, :] * jnp.exp(cum - lw),
                         b_t=b_all * e_neg, k_t=k_all * e_neg, b_h=b_all * e_end, k_h=k_all * e_end,
                         v=v_ref[rows, :], p_end=jnp.exp(cum_end)))
    units = [(ci, p) for ci in range(n_chunks) for p in range(n_pairs)]

    a_all = {}
    for ci, p in units:
        cols = pair_cols[p]
        rt, at = prep[ci]["r_t"][:, cols], prep[ci]["a_t"][:, cols]
        lhs4 = jnp.concatenate([jnp.where(mlo, at, zeros_c), jnp.where(mlo, zeros_c, at),
                                jnp.where(mlo, rt, zeros_c), jnp.where(mlo, zeros_c, rt)], axis=0)
        a_all[ci, p] = _dot_nt(lhs4, jnp.concatenate([prep[ci]["b_t"][:, cols], prep[ci]["k_t"][:, cols]], axis=0))

    heads = []
    for ci, p in units:
        cols = pair_cols[p]
        at, vp = prep[ci]["a_t"][:, cols], prep[ci]["v"][:, cols]
        at_sw = pltpu.roll(at, HALF, axis=1)
        vp_sw = pltpu.roll(vp, HALF, axis=1)
        for hh in range(2):
            nk = jnp.where(strict2, a_all[ci, p][hh * c:(hh + 1) * c], 0.0)
            av = _dot(nk, jnp.concatenate([zeros_c, vp_sw if hh == 0 else vp], axis=0))
            x0 = jnp.where(mlo, at if hh == 0 else at_sw, av)
            nk_sw = pltpu.roll(nk, HALF, axis=1)
            n_split = jnp.where(mlo, jnp.where(same_blk, 0.0, nk), jnp.where(same_blk, nk_sw, 0.0))
            heads.append(jnp.concatenate([x0, n_split], axis=1))

    for _ in range(int(math.log2(SOLVE_BLOCK))):
        nxt = []
        for y in heads:
            prod = _dot2(y[:, LANES:], jnp.concatenate([zeros_2c, y], axis=0))
            nxt.append(jnp.where(lane2 >= LANES + HALF, prod, y + prod))
        heads = nxt
    for _ in range(int(math.log2(c // SOLVE_BLOCK))):
        nxt = []
        for y in heads:
            prod = _dot2(y[:, LANES:], jnp.concatenate([y, zeros_2c], axis=0))
            nxt.append(jnp.where(lane2 < LANES, y + prod, prod))
        heads = nxt

    big = {}
    for n, (ci, p) in enumerate(units):
        cols = pair_cols[p]
        x_lo, x_hi = heads[2 * n][:, :LANES], heads[2 * n + 1][:, :LANES]
        w_p = jnp.where(mlo, x_lo, pltpu.roll(x_hi, HALF, axis=1))
        u0_p = jnp.where(mlo, pltpu.roll(x_lo, HALF, axis=1), x_hi)
        rhs = jnp.concatenate([jnp.concatenate([w_p, u0_p], axis=1),
                               jnp.concatenate([zeros_c, prep[ci]["v"][:, cols]], axis=1)], axis=0)
        a_r = jnp.where(incl2, a_all[ci, p][2 * c:], 0.0)
        bk_t = jnp.concatenate([prep[ci]["b_h"][:, cols], prep[ci]["k_h"][:, cols]], axis=0).T
        big[ci, p] = _dot(jnp.concatenate([a_r, bk_t], axis=0), rhs)

    state = [h_scr[p] for p in range(n_pairs)]
    for ci in range(n_chunks):
        rows = slice(ci * c, (ci + 1) * c)
        for p, cols in enumerate(pair_cols):
            res = big[ci, p]
            q_hat = prep[ci]["r_t"][:, cols] + jnp.where(mlo, res[:c, :LANES], res[c:2 * c, :LANES])
            o_hat = jnp.where(mlo, res[:c, LANES:], res[c:2 * c, LANES:])
            g_mat = (jnp.where(same_head, res[2 * c:, :LANES], 0.0)
                     + jnp.where(diag, prep[ci]["p_end"][:, cols], 0.0))
            j_mat = jnp.where(same_head, res[2 * c:, LANES:], 0.0)
            st = _dot2(jnp.concatenate([q_hat, g_mat], axis=0), state[p])
            o_ref[rows, cols] = st[:c] + o_hat
            state[p] = st[c:] + j_mat
    for p in range(n_pairs):
        h_scr[p] = state[p]


def _rwkv_scan(r, k2, v, na, b, logw):
    lp, rw = r.shape
    rows = CHUNK * SCAN_CHUNKS
    spec = pl.BlockSpec((rows, rw), lambda i: (i, 0))
    return pl.pallas_call(
        _scan_kernel,
        grid=(lp // rows,),
        in_specs=[spec] * 6,
        out_specs=spec,
        out_shape=jax.ShapeDtypeStruct((lp, rw), F32),
        scratch_shapes=[pltpu.VMEM((rw // LANES, LANES, LANES), F32)],
        compiler_params=pltpu.CompilerParams(dimension_semantics=("arbitrary",),
                                             vmem_limit_bytes=VMEM_LIMIT),
        name="rwkv_scan",
    )(r, k2, v, na, b, logw)


def _even_out_kernel(h_ref, o_ref, r_ref, k_ref, v_ref, g_ref, z_ref, lnw_ref, lnb_ref, rk_ref, seg_ref,
                     wout_ref, g1_ref, g2_ref, g3_ref, w1_ref, w2_ref, out_ref):
    inv_n = 1.0 / RWKV_HEAD
    o = o_ref[...]
    mean = _head_sum(o, seg_ref) * inv_n
    dev = o - mean
    var = _head_sum(dev * dev, seg_ref) * inv_n
    on = dev * lax.rsqrt(var + GN_EPS) * lnw_ref[...] + lnb_ref[...]
    bonus = _head_sum(r_ref[...] * k_ref[...] * rk_ref[...], seg_ref) * v_ref[...]
    om = (on + bonus) * g_ref[...]
    rw = RWKV_WIDTH
    m = _dot(om, wout_ref[:rw, :]) + _dot(z_ref[...], wout_ref[rw:, :])
    hm = h_ref[...] + _rms(m, g1_ref[...], RMS_EPS)
    out_ref[...] = _mlp_residual(hm, g2_ref[...], g3_ref[...], w1_ref, w2_ref)


def _weight_spec(arr):
    return pl.BlockSpec(arr.shape, lambda i: (0,) * arr.ndim, pipeline_mode=pl.Buffered(1))


def _even_out(h, o, r, k2, v, g, z, ln_w, ln_b, r_k, seg, w_out, g1, g2, g3, w1, w2):
    lp = h.shape[0]
    tm = _pick_tile(lp, (640, 256))
    row_spec = lambda width: pl.BlockSpec((tm, width), lambda i: (i, 0))
    full = lambda arr: pl.BlockSpec(arr.shape, lambda i: (0,) * arr.ndim)
    rw = RWKV_WIDTH
    return pl.pallas_call(
        _even_out_kernel,
        grid=(lp // tm,),
        in_specs=[row_spec(D_MODEL)] + [row_spec(rw)] * 6 +
                 [full(ln_w), full(ln_b), full(r_k), full(seg), _weight_spec(w_out), full(g1), full(g2),
                  full(g3), _weight_spec(w1), _weight_spec(w2)],
        out_specs=row_spec(D_MODEL),
        out_shape=jax.ShapeDtypeStruct((lp, D_MODEL), F32),
        compiler_params=pltpu.CompilerParams(dimension_semantics=("parallel",),
                                             vmem_limit_bytes=VMEM_LIMIT),
        name="even_out",
    )(h, o, r, k2, v, g, z, ln_w, ln_b, r_k, seg, w_out, g1, g2, g3, w1, w2)


def _odd_in_kernel(h_ref, g0_ref, w_ref, wrot_ref, cos_ref, sin_ref, q_out, k_out, v_out, kmax_out):
    @pl.when(pl.program_id(0) == 0)
    def _():
        kmax_out[...] = jnp.zeros(kmax_out.shape, F32)

    hn = _rms(h_ref[...], g0_ref[...], RMS_EPS).astype(BF16)
    y = jnp.dot(hn, w_ref[...], preferred_element_type=F32)
    yr = jnp.dot(hn, wrot_ref[...], preferred_element_type=F32)
    cos = cos_ref[...]
    sin = sin_ref[...]
    scale = DIFF_HEAD ** -0.5 * math.log2(math.e)
    tm = y.shape[0]
    lane = lax.broadcasted_iota(jnp.int32, (tm, LANES), 1)
    minus_one = jnp.where(lane == 0, -1.0, 0.0).astype(BF16)
    ones = jnp.ones((tm, LANES), BF16)
    for j in range(D_MODEL // LANES):
        cols = slice(j * LANES, (j + 1) * LANES)
        kcols = slice(D_MODEL + j * LANES, D_MODEL + (j + 1) * LANES)
        vcols = slice(2 * D_MODEL + j * LANES, 2 * D_MODEL + (j + 1) * LANES)
        wide = slice(2 * j * LANES, (2 * j + 1) * LANES)
        wide_hi = slice((2 * j + 1) * LANES, (2 * j + 2) * LANES)
        q_out[:, cols] = ((y[:, cols] * cos + yr[:, cols] * sin) * scale).astype(BF16)
        kb = (y[:, kcols] * cos + yr[:, kcols] * sin).astype(BF16)
        k_out[:, wide] = kb
        k_out[:, wide_hi] = minus_one
        v_out[:, wide] = y[:, vcols].astype(BF16)
        v_out[:, wide_hi] = ones
        kf = kb.astype(F32)
        knorm = jnp.sqrt(jnp.max(jnp.sum(kf * kf, axis=-1, keepdims=True), axis=0, keepdims=True))
        kmax_out[:, cols] = jnp.maximum(kmax_out[:, cols], jnp.broadcast_to(knorm, (8, LANES)))


def _odd_in(h, g0, w, w_rot, cos, sin):
    lp, lpk = h.shape[0], cos.shape[0]
    tm = _pick_tile(lp, (256,))
    last = lp // tm - 1
    row_spec = lambda width: pl.BlockSpec((tm, width), lambda i: (i, 0))
    full = lambda arr: pl.BlockSpec(arr.shape, lambda i: (0,) * arr.ndim)
    wide = jax.ShapeDtypeStruct((lpk, 2 * D_MODEL), BF16)
    return pl.pallas_call(
        _odd_in_kernel,
        grid=(lpk // tm,),
        in_specs=[pl.BlockSpec((tm, D_MODEL), lambda i: (jnp.minimum(i, last), 0)), full(g0),
                  _weight_spec(w), _weight_spec(w_rot), row_spec(LANES), row_spec(LANES)],
        out_specs=[row_spec(D_MODEL), row_spec(2 * D_MODEL), row_spec(2 * D_MODEL),
                   pl.BlockSpec((8, D_MODEL), lambda i: (0, 0))],
        out_shape=[jax.ShapeDtypeStruct((lpk, D_MODEL), BF16), wide, wide,
                   jax.ShapeDtypeStruct((8, D_MODEL), F32)],
        compiler_params=pltpu.CompilerParams(dimension_semantics=("arbitrary",),
                                             vmem_limit_bytes=VMEM_LIMIT),
        name="odd_in",
    )(h, g0, w, w_rot, cos, sin)


def _attn_kernel(lam_ref, sw_ref, kmax_ref, q_ref, k_ref, v_ref, o_ref, q2_scr, p_scr, acc_scr,
                 *, lam_init, tq, tk, nh):
    i = pl.program_id(1)
    lane = lax.broadcasted_iota(jnp.int32, (1, LANES), 1)
    mlo = lane < HALF
    nt = (((1,), (1,)), ((), ()))
    n_blocks = (i * tq) // tk + 1
    wl = 2 * LANES

    def diagonal_mask(off):
        qrow = lax.broadcasted_iota(jnp.int32, (2 * tq, tk), 0)
        qrow = jnp.where(qrow >= tq, qrow - tq, qrow)
        kcol = lax.broadcasted_iota(jnp.int32, (2 * tq, tk), 1)
        return kcol - qrow <= i * tq - off

    for hd in range(nh):
        q = q_ref[:, hd * LANES:(hd + 1) * LANES]
        zq = jnp.zeros_like(q)
        q2 = jnp.concatenate([jnp.where(mlo, q, zq), jnp.where(mlo, zq, q)], axis=0)
        q2f = q2.astype(F32)
        bound = (jnp.sqrt(jnp.sum(q2f * q2f, axis=-1, keepdims=True))
                 * kmax_ref[0:1, hd * LANES:hd * LANES + 1] * BOUND_SLACK)
        q2_scr[hd] = jnp.concatenate([q2, jnp.where(lane == 0, bound, 0.0).astype(BF16)], axis=1)

    def weights_into(hd, slot, t, diagonal):
        off = pl.multiple_of(t * tk, tk)
        s = lax.dot_general(q2_scr[hd], k_ref[pl.ds(off, tk), hd * wl:(hd + 1) * wl], nt,
                            preferred_element_type=F32)
        if diagonal:
            s = jnp.where(diagonal_mask(off), s, NEG_BIG)
        p_scr[hd, slot] = jnp.exp2(s).astype(BF16)

    def add_weighted_values(hd, slot, t):
        off = pl.multiple_of(t * tk, tk)
        acc_scr[hd] += jnp.dot(p_scr[hd, slot], v_ref[pl.ds(off, tk), hd * wl:(hd + 1) * wl],
                               preferred_element_type=F32)

    def tick(t, slot, diagonal):
        for hd in range(nh):
            weights_into(hd, slot, t, diagonal)
            add_weighted_values(hd, 1 - slot, t - 1)

    def drain(slot, t):
        for hd in range(nh):
            add_weighted_values(hd, slot, t)

    acc_scr[...] = jnp.zeros(acc_scr.shape, F32)

    @pl.when(n_blocks == 1)
    def _():
        for hd in range(nh):
            weights_into(hd, 0, 0, True)
        drain(0, 0)

    @pl.when(n_blocks > 1)
    def _():
        for hd in range(nh):
            weights_into(hd, 0, 0, False)

    def pair(u, carry):
        tick(2 * u + 1, 1, False)
        tick(2 * u + 2, 0, False)
        return carry

    n_pairs = jnp.maximum(n_blocks - 2, 0) // 2
    lax.fori_loop(0, n_pairs, pair, 0)
    t1 = 2 * n_pairs + 1

    @pl.when(jnp.logical_and(n_blocks > 1, n_blocks % 2 == 0))
    def _():
        tick(t1, 1, True)
        drain(1, t1)

    @pl.when(jnp.logical_and(n_blocks > 1, n_blocks % 2 == 1))
    def _():
        tick(t1, 1, False)
        tick(t1 + 1, 0, True)
        drain(0, t1 + 1)

    lv = lam_ref[...]
    lam = (jnp.exp(jnp.sum(lv[0:1] * lv[1:2], axis=-1, keepdims=True))
           - jnp.exp(jnp.sum(lv[2:3] * lv[3:4], axis=-1, keepdims=True)) + lam_init)

    def write_out(hd, acc):
        o = acc[:, :LANES] / acc[:, LANES:]
        o = o[:tq] - lam * o[tq:]
        o = _rms(o, sw_ref[...], SUBLN_EPS) * (1.0 - lam_init)
        o_ref[:, hd * LANES:(hd + 1) * LANES] = o.astype(BF16)

    for hd in range(nh):
        write_out(hd, acc_scr[hd])

    smallest = jnp.min(acc_scr[:, :, LANES:LANES + 1])
    @pl.when(jnp.logical_not(smallest >= MIN_ROW_SUM))
    def _():
        for hd in range(nh):
            q2 = q2_scr[hd, :, :LANES]

            def block(t, carry, diagonal):
                m, acc = carry
                off = pl.multiple_of(t * tk, tk)
                s = lax.dot_general(q2, k_ref[pl.ds(off, tk), hd * wl:hd * wl + LANES], nt,
                                    preferred_element_type=F32)
                if diagonal:
                    s = jnp.where(diagonal_mask(off), s, NEG_BIG)
                m_new = jnp.maximum(m, jnp.max(s, axis=-1, keepdims=True))
                p = jnp.exp2(s - m_new).astype(BF16)
                pv = jnp.dot(p, v_ref[pl.ds(off, tk), hd * wl:(hd + 1) * wl], preferred_element_type=F32)
                return m_new, jnp.exp2(m - m_new) * acc + pv

            init = (jnp.full((2 * tq, 1), NEG_BIG, F32), jnp.zeros((2 * tq, wl), F32))
            carry = lax.fori_loop(0, n_blocks - 1, lambda t, c: block(t, c, False), init)
            write_out(hd, block(n_blocks - 1, carry, True)[1])


def _diff_attn(q, k, v, kmax, lam_vecs, subln_w, lam_init, lp):
    tq, tk, nh = ATT_Q_BLOCK, ATT_K_BLOCK, ATT_HEADS
    lpk = k.shape[0]
    blk = pl.BlockSpec((tq, nh * LANES), lambda h, i: (i, h))
    resident = pl.BlockSpec((lpk, 2 * nh * LANES), lambda h, i: (0, h), pipeline_mode=pl.Buffered(1))
    full = lambda arr: pl.BlockSpec(arr.shape, lambda h, i: (0,) * arr.ndim)
    return pl.pallas_call(
        functools.partial(_attn_kernel, lam_init=lam_init, tq=tq, tk=tk, nh=nh),
        grid=(DIFF_HEADS // nh, lp // tq),
        in_specs=[full(lam_vecs), full(subln_w), pl.BlockSpec((8, nh * LANES), lambda h, i: (0, h)),
                  blk, resident, resident],
        out_specs=blk,
        out_shape=jax.ShapeDtypeStruct((lp, D_MODEL), BF16),
        scratch_shapes=[pltpu.VMEM((nh, 2 * tq, 2 * LANES), BF16), pltpu.VMEM((nh, 2, 2 * tq, tk), BF16),
                        pltpu.VMEM((nh, 2 * tq, 2 * LANES), F32)],
        compiler_params=pltpu.CompilerParams(dimension_semantics=("parallel", "arbitrary"),
                                             vmem_limit_bytes=VMEM_LIMIT),
        name="diff_attn",
    )(lam_vecs, subln_w, kmax, q, k, v)


def _odd_out_kernel(h_ref, o_ref, wout_ref, g1_ref, g2_ref, g3_ref, w1_ref, w2_ref, out_ref):
    m = jnp.dot(o_ref[...], wout_ref[...], preferred_element_type=F32)
    hm = h_ref[...] + _rms(m, g1_ref[...], RMS_EPS)
    out_ref[...] = _mlp_residual(hm, g2_ref[...], g3_ref[...], w1_ref, w2_ref)


def _odd_out(h, o, w_out, g1, g2, g3, w1, w2):
    lp = h.shape[0]
    tm = _pick_tile(lp, (640, 256))
    row_spec = lambda width: pl.BlockSpec((tm, width), lambda i: (i, 0))
    full = lambda arr: pl.BlockSpec(arr.shape, lambda i: (0,) * arr.ndim)
    return pl.pallas_call(
        _odd_out_kernel,
        grid=(lp // tm,),
        in_specs=[row_spec(D_MODEL), row_spec(D_MODEL), _weight_spec(w_out), full(g1), full(g2), full(g3),
                  _weight_spec(w1), _weight_spec(w2)],
        out_specs=row_spec(D_MODEL),
        out_shape=jax.ShapeDtypeStruct((lp, D_MODEL), F32),
        compiler_params=pltpu.CompilerParams(dimension_semantics=("parallel",),
                                             vmem_limit_bytes=VMEM_LIMIT),
        name="odd_out",
    )(h, o, w_out, g1, g2, g3, w1, w2)


def _rotate_half_weights(w):
    d_in, d_out = w.shape
    w4 = w.reshape(d_in, d_out // DIFF_HEAD, 2, DIFF_HEAD // 2)
    return jnp.concatenate([-w4[:, :, 1:2], w4[:, :, 0:1]], axis=2).reshape(d_in, d_out)


def _forward(x, meta, norm_g, mlp_w1, mlp_w2, ev, od):
    seq = x.shape[0]
    length = N_META + seq
    lp = -(-length // ROW_ALIGN) * ROW_ALIGN
    h = jnp.concatenate([meta.astype(x.dtype), x, jnp.zeros((lp - length, D_MODEL), x.dtype)], axis=0)

    lpk = -(-lp // ATT_K_BLOCK) * ATT_K_BLOCK
    pos = jnp.arange(lpk, dtype=F32)
    inv = ROPE_THETA ** (-jnp.arange(0, DIFF_HEAD, 2, dtype=F32) / DIFF_HEAD)
    ang = pos[:, None] * inv[None, :]
    ang = jnp.concatenate([ang, ang, ang, ang], axis=-1)
    cos, sin = jnp.cos(ang), jnp.sin(ang)

    head_id = jnp.arange(RWKV_WIDTH) // RWKV_HEAD
    seg = (head_id[:, None] == head_id[None, :]).astype(BF16)
    row2 = lambda t: t.reshape(1, -1)

    depth = norm_g.shape[0]
    for i in range(depth):
        g = norm_g[i]
        g0, g1, g2, g3 = (row2(g[n]) for n in range(4))
        w1 = mlp_w1[i].astype(BF16)
        w2 = mlp_w2[i].astype(BF16)
        j = i // 2
        if i % 2 == 0:
            (w_in, mu, w0, w_up, a0, a_up, g_up, k_k, k_a, r_k, ln_w, ln_b, pool_w, pool_scale,
             w_out) = (t[j] for t in ev)
            zeros = jnp.zeros((DECAY_RANK, RWKV_WIDTH), F32)
            lora_w = jnp.concatenate([jnp.concatenate([w_up, zeros], axis=1),
                                      jnp.concatenate([zeros, a_up], axis=1)], axis=0).astype(BF16)
            r, k2, v, na, b, logw, gate, z = _even_in(
                h, g0, w_in.astype(BF16), row2(mu), row2(w0), lora_w, row2(a0), g_up.astype(BF16),
                row2(k_k), row2(k_a), seg, pool_w.astype(BF16), row2(pool_scale))
            o = _rwkv_scan(r, k2, v, na, b, logw)
            h = _even_out(h, o, r, k2, v, gate, z, row2(ln_w), row2(ln_b), row2(r_k), seg,
                          w_out.astype(BF16), g1, g2, g3, w1, w2)
        else:
            w_in, lam_vecs, subln_w, w_out = (t[j] for t in od)
            w_rot = _rotate_half_weights(w_in[:, :2 * D_MODEL])
            q, k, v, kmax = _odd_in(h, g0, w_in.astype(BF16), w_rot.astype(BF16), cos, sin)
            lam_init = 0.8 - 0.6 * math.exp(-0.3 * i)
            o = _diff_attn(q, k, v, kmax, lam_vecs, row2(subln_w), lam_init, lp)
            h = _odd_out(h, o, w_out.astype(BF16), g1, g2, g3, w1, w2)
    return h[N_META:length]


def kernel(x, meta, norm_g, mlp_w1, mlp_w2, ev_w_in, ev_mu, ev_w0, ev_w_up, ev_a0, ev_a_up, ev_g_up, ev_k_k,
           ev_k_a, ev_r_k, ev_ln_w, ev_ln_b, ev_pool_w, ev_pool_scale, ev_w_out, od_w_in, od_lambda,
           od_subln_w, od_w_out):
    ev = (ev_w_in, ev_mu, ev_w0, ev_w_up, ev_a0, ev_a_up, ev_g_up, ev_k_k, ev_k_a, ev_r_k, ev_ln_w, ev_ln_b,
          ev_pool_w, ev_pool_scale, ev_w_out)
    od = (od_w_in, od_lambda, od_subln_w, od_w_out)
    outs = [_forward(x[bi], meta, norm_g, mlp_w1, mlp_w2, ev, od) for bi in range(x.shape[0])]
    return jnp.stack(outs, axis=0)
```

```python
import functools
import math

import jax
import jax.numpy as jnp
from jax import lax
from jax.experimental import pallas as pl
from jax.experimental.pallas import tpu as pltpu

F32, BF16 = jnp.float32, jnp.bfloat16

D_MODEL = 1024
N_META = 16
RMS_EPS = 1e-6
D_FF = 4 * D_MODEL
RWKV_HEAD = 64
RWKV_WIDTH = D_MODEL // 2
DECAY_RANK = 64
ICLR_RANK = 64
GATE_RANK = 128
GN_EPS = RWKV_HEAD * 1e-5
POOL_WIDTH = D_MODEL - RWKV_WIDTH
POOL_WINDOWS = (2, 4, 8, 16)
POOL_GROUP_W = POOL_WIDTH // len(POOL_WINDOWS)
POOL_CARRY = 16
SHIFT_WIDTH = 3 * RWKV_WIDTH + DECAY_RANK + ICLR_RANK + GATE_RANK
EVEN_IN = SHIFT_WIDTH + POOL_WIDTH
DIFF_HEADS = 8
DIFF_HEAD = D_MODEL // (2 * DIFF_HEADS)
SUBLN_EPS = 1e-5
ROPE_THETA = 10000.0

LANES = 128
HALF = LANES // 2
ROW_ALIGN = 256
CHUNK = 64
SOLVE_BLOCK = 8
SCAN_CHUNKS = 4
ATT_Q_BLOCK = 256
ATT_K_BLOCK = 1024
ATT_HEADS = 2
FF_CHUNK = 1024
NEG_BIG = -1e30
BOUND_SLACK = 1.02
MIN_ROW_SUM = 2.0 ** -100
VMEM_LIMIT = 56 * 1024 * 1024


def _pick_tile(n, candidates):
    for c in candidates:
        if n % c == 0:
            return c
    raise ValueError(f"no tile in {candidates} divides {n}")


def _rms(t, g, eps):
    return t * lax.rsqrt(jnp.mean(t * t, axis=-1, keepdims=True) + eps) * g


def _split2(x):
    hi = x.astype(BF16)
    lo = (x - hi.astype(F32)).astype(BF16)
    return hi, lo


def _dot(a, b):
    return jnp.dot(a.astype(BF16), b.astype(BF16), preferred_element_type=F32)


def _dot2(a, b):
    ab = a.astype(BF16)
    bh, bl = _split2(b)
    d = functools.partial(jnp.dot, preferred_element_type=F32)
    return d(ab, bh) + d(ab, bl)


def _dot_nt(a, b):
    return lax.dot_general(a.astype(BF16), b.astype(BF16), (((1,), (1,)), ((), ())),
                           preferred_element_type=F32)


def _head_sum(x, seg_ref):
    xh, xl = _split2(x)
    d = functools.partial(jnp.dot, preferred_element_type=F32)
    return d(xh, seg_ref[...]) + d(xl, seg_ref[...])


def _sigmoid(x):
    return 1.0 / (1.0 + jnp.exp(-x))


def _softplus(x):
    return jnp.maximum(x, 0.0) + jnp.log(1.0 + jnp.exp(-jnp.abs(x)))


def _mlp_residual(hm, g2, g3, w1_ref, w2_ref):
    n = _rms(hm, g2, RMS_EPS).astype(BF16)
    acc = jnp.zeros(hm.shape, F32)
    for c in range(D_FF // FF_CHUNK):
        cols = slice(c * FF_CHUNK, (c + 1) * FF_CHUNK)
        a = jnp.dot(n, w1_ref[:, cols], preferred_element_type=F32)
        a = jnp.square(jnp.maximum(a, 0.0)).astype(BF16)
        acc = acc + jnp.dot(a, w2_ref[cols, :], preferred_element_type=F32)
    return hm + _rms(acc, g3, RMS_EPS)


def _even_in_kernel(h_ref, g0_ref, win_ref, mu_ref, w0_ref, lora_ref, a0_ref, gup_ref,
                    kk_ref, ka_ref, seg_ref, poolw_ref, pscale_ref,
                    r_out, k_out, v_out, na_out, b_out, lw_out, g_out, z_out,
                    ycarry, ucarry, *, tm):
    i = pl.program_id(0)

    @pl.when(i == 0)
    def _():
        ycarry[...] = jnp.zeros(ycarry.shape, F32)
        ucarry[...] = jnp.zeros(ucarry.shape, F32)

    hn = _rms(h_ref[...], g0_ref[...], RMS_EPS).astype(BF16)
    y = jnp.dot(hn, win_ref[...], preferred_element_type=F32)

    ysh = y[:, :SHIFT_WIDTH]
    row = lax.broadcasted_iota(jnp.int32, (tm, 1), 0)
    prev = jnp.where(row == 0, ycarry[7:8, :], pltpu.roll(ysh, 1, axis=0))
    ycarry[...] = ysh[tm - 8:, :]
    ys = ysh + (prev - ysh) * mu_ref[...]

    rw = RWKV_WIDTH
    r = ys[:, 0:rw]
    k = ys[:, rw:2 * rw]
    v = ys[:, 2 * rw:3 * rw]
    wa = ys[:, 3 * rw:3 * rw + LANES]
    gd = ys[:, 3 * rw + LANES:SHIFT_WIDTH]

    lane = lax.broadcasted_iota(jnp.int32, (1, LANES), 1)
    lora_in = jnp.where(lane < DECAY_RANK, jnp.tanh(wa), wa)
    lora = _dot(lora_in, lora_ref[...])
    wlog = -_softplus(-(w0_ref[...] + lora[:, :rw])) - 0.5
    logw = -jnp.exp(wlog)
    a = _sigmoid(a0_ref[...] + lora[:, rw:])
    g = _dot(_sigmoid(gd), gup_ref[...])

    kk = k * kk_ref[...]
    kk = kk * lax.rsqrt(jnp.maximum(_head_sum(kk * kk, seg_ref), 1e-24))
    k2 = k * (1.0 + (a - 1.0) * ka_ref[...])

    r_out[...] = r
    k_out[...] = k2
    v_out[...] = v
    na_out[...] = -kk
    b_out[...] = kk * a
    lw_out[...] = logw
    g_out[...] = g

    u = y[:, SHIFT_WIDTH:]
    ext = jnp.concatenate([ucarry[...], u], axis=0)
    ucarry[...] = u[tm - POOL_CARRY:, :]
    t_idx = i * tm + row
    for gi, win in enumerate(POOL_WINDOWS):
        cols = slice(gi * POOL_GROUP_W, (gi + 1) * POOL_GROUP_W)
        s = ext[:, cols]
        span = 1
        while span < win:
            s = s + pltpu.roll(s, span, axis=0)
            span *= 2
        cnt = jnp.minimum(t_idx + 1, win).astype(F32)
        d = s[POOL_CARRY:, :] / cnt - u[:, cols]
        z_out[:, cols] = _dot(d, poolw_ref[gi]) * pscale_ref[:, cols]


def _even_in(h, g0, w_in, mu, w0, lora_w, a0, g_up, k_k, k_a, seg, pool_w, pool_scale):
    lp = h.shape[0]
    tm = _pick_tile(lp, (640, 256))
    rw = RWKV_WIDTH
    row_spec = lambda width: pl.BlockSpec((tm, width), lambda i: (i, 0))
    full = lambda arr: pl.BlockSpec(arr.shape, lambda i: (0,) * arr.ndim)
    out_sds = jax.ShapeDtypeStruct((lp, rw), F32)
    return pl.pallas_call(
        functools.partial(_even_in_kernel, tm=tm),
        grid=(lp // tm,),
        in_specs=[row_spec(D_MODEL), full(g0), full(w_in), full(mu), full(w0), full(lora_w), full(a0),
                  full(g_up), full(k_k), full(k_a), full(seg), full(pool_w), full(pool_scale)],
        out_specs=[row_spec(rw)] * 8,
        out_shape=[out_sds] * 8,
        scratch_shapes=[pltpu.VMEM((8, SHIFT_WIDTH), F32), pltpu.VMEM((POOL_CARRY, POOL_WIDTH), F32)],
        compiler_params=pltpu.CompilerParams(dimension_semantics=("arbitrary",),
                                             vmem_limit_bytes=VMEM_LIMIT),
        name="even_in",
    )(h, g0, w_in, mu, w0, lora_w, a0, g_up, k_k, k_a, seg, pool_w, pool_scale)


def _scan_kernel(r_ref, k_ref, v_ref, na_ref, b_ref, lw_ref, o_ref, h_scr):
    @pl.when(pl.program_id(0) == 0)
    def _():
        h_scr[...] = jnp.zeros(h_scr.shape, F32)

    c = CHUNK
    n_chunks = r_ref.shape[0] // c
    row = lax.broadcasted_iota(jnp.int32, (c, c), 0)
    col = lax.broadcasted_iota(jnp.int32, (c, c), 1)
    tri = jnp.where(col <= row, 1.0, 0.0).astype(BF16)
    lane = lax.broadcasted_iota(jnp.int32, (1, LANES), 1)
    mlo = lane < HALF
    prow = lax.broadcasted_iota(jnp.int32, (LANES, LANES), 0)
    pcol = lax.broadcasted_iota(jnp.int32, (LANES, LANES), 1)
    same_head = (prow < HALF) == (pcol < HALF)
    diag = prow == pcol
    trow = lax.broadcasted_iota(jnp.int32, (c, LANES), 0)
    tcol = lax.broadcasted_iota(jnp.int32, (c, LANES), 1)
    tcol = jnp.where(tcol >= HALF, tcol - HALF, tcol)
    strict2 = tcol < trow
    incl2 = jnp.concatenate([tcol <= trow] * 2, axis=0)
    same_blk = (tcol // SOLVE_BLOCK) == (trow // SOLVE_BLOCK)
    lane2 = lax.broadcasted_iota(jnp.int32, (1, 2 * LANES), 1)
    zeros_c = jnp.zeros((c, LANES), F32)
    zeros_2c = jnp.zeros((c, 2 * LANES), F32)
    n_pairs = RWKV_WIDTH // LANES
    pair_cols = [slice(p * LANES, (p + 1) * LANES) for p in range(n_pairs)]
    d = functools.partial(jnp.dot, preferred_element_type=F32)

    prep = []
    for ci in range(n_chunks):
        rows = slice(ci * c, (ci + 1) * c)
        lw = lw_ref[rows, :]
        lw_hi = lw.astype(BF16)
        lw_r = lw - lw_hi.astype(F32)
        lw_mid = lw_r.astype(BF16)
        lw_lo = (lw_r - lw_mid.astype(F32)).astype(BF16)
        cum = d(tri, lw_hi) + (d(tri, lw_mid) + d(tri, lw_lo))
        cum_end = cum[c - 1:c, :]
        e_neg = jnp.exp(-cum)
        e_end = jnp.exp(cum_end - cum)
        b_all = b_ref[rows, :]
        k_all = k_ref[rows, :]
        prep.append(dict(r_t=r_ref[rows, :] * jnp.exp(cum), a_t=na_ref[rows, :] * jnp.exp(cum - lw),
                         b_t=b_all * e_neg, k_t=k_all * e_neg, b_h=b_all * e_end, k_h=k_all * e_end,
                         v=v_ref[rows, :], p_end=jnp.exp(cum_end)))
    units = [(ci, p) for ci in range(n_chunks) for p in range(n_pairs)]

    a_all = {}
    for ci, p in units:
        cols = pair_cols[p]
        rt, at = prep[ci]["r_t"][:, cols], prep[ci]["a_t"][:, cols]
        lhs4 = jnp.concatenate([jnp.where(mlo, at, zeros_c), jnp.where(mlo, zeros_c, at),
                                jnp.where(mlo, rt, zeros_c), jnp.where(mlo, zeros_c, rt)], axis=0)
        a_all[ci, p] = _dot_nt(lhs4, jnp.concatenate([prep[ci]["b_t"][:, cols], prep[ci]["k_t"][:, cols]], axis=0))

    heads = []
    for ci, p in units:
        cols = pair_cols[p]
        at, vp = prep[ci]["a_t"][:, cols], prep[ci]["v"][:, cols]
        at_sw = pltpu.roll(at, HALF, axis=1)
        vp_sw = pltpu.roll(vp, HALF, axis=1)
        for hh in range(2):
            nk = jnp.where(strict2, a_all[ci, p][hh * c:(hh + 1) * c], 0.0)
            av = _dot(nk, jnp.concatenate([zeros_c, vp_sw if hh == 0 else vp], axis=0))
            x0 = jnp.where(mlo, at if hh == 0 else at_sw, av)
            nk_sw = pltpu.roll(nk, HALF, axis=1)
            n_split = jnp.where(mlo, jnp.where(same_blk, 0.0, nk), jnp.where(same_blk, nk_sw, 0.0))
            heads.append(jnp.concatenate([x0, n_split], axis=1))

    for _ in range(int(math.log2(SOLVE_BLOCK))):
        nxt = []
        for y in heads:
            prod = _dot2(y[:, LANES:], jnp.concatenate([zeros_2c, y], axis=0))
            nxt.append(jnp.where(lane2 >= LANES + HALF, prod, y + prod))
        heads = nxt
    for _ in range(int(math.log2(c // SOLVE_BLOCK))):
        nxt = []
        for y in heads:
            prod = _dot2(y[:, LANES:], jnp.concatenate([y, zeros_2c], axis=0))
            nxt.append(jnp.where(lane2 < LANES, y + prod, prod))
        heads = nxt

    big = {}
    for n, (ci, p) in enumerate(units):
        cols = pair_cols[p]
        x_lo, x_hi = heads[2 * n][:, :LANES], heads[2 * n + 1][:, :LANES]
        w_p = jnp.where(mlo, x_lo, pltpu.roll(x_hi, HALF, axis=1))
        u0_p = jnp.where(mlo, pltpu.roll(x_lo, HALF, axis=1), x_hi)
        rhs = jnp.concatenate([jnp.concatenate([w_p, u0_p], axis=1),
                               jnp.concatenate([zeros_c, prep[ci]["v"][:, cols]], axis=1)], axis=0)
        a_r = jnp.where(incl2, a_all[ci, p][2 * c:], 0.0)
        bk_t = jnp.concatenate([prep[ci]["b_h"][:, cols], prep[ci]["k_h"][:, cols]], axis=0).T
        big[ci, p] = _dot(jnp.concatenate([a_r, bk_t], axis=0), rhs)

    state = [h_scr[p] for p in range(n_pairs)]
    for ci in range(n_chunks):
        rows = slice(ci * c, (ci + 1) * c)
        for p, cols in enumerate(pair_cols):
            res = big[ci, p]
            q_hat = prep[ci]["r_t"][:, cols] + jnp.where(mlo, res[:c, :LANES], res[c:2 * c, :LANES])
            o_hat = jnp.where(mlo, res[:c, LANES:], res[c:2 * c, LANES:])
            g_mat = (jnp.where(same_head, res[2 * c:, :LANES], 0.0)
                     + jnp.where(diag, prep[ci]["p_end"][:, cols], 0.0))
            j_mat = jnp.where(same_head, res[2 * c:, LANES:], 0.0)
            st = _dot2(jnp.concatenate([q_hat, g_mat], axis=0), state[p])
            o_ref[rows, cols] = st[:c] + o_hat
            state[p] = st[c:] + j_mat
    for p in range(n_pairs):
        h_scr[p] = state[p]


def _rwkv_scan(r, k2, v, na, b, logw):
    lp, rw = r.shape
    rows = CHUNK * SCAN_CHUNKS
    spec = pl.BlockSpec((rows, rw), lambda i: (i, 0))
    return pl.pallas_call(
        _scan_kernel,
        grid=(lp // rows,),
        in_specs=[spec] * 6,
        out_specs=spec,
        out_shape=jax.ShapeDtypeStruct((lp, rw), F32),
        scratch_shapes=[pltpu.VMEM((rw // LANES, LANES, LANES), F32)],
        compiler_params=pltpu.CompilerParams(dimension_semantics=("arbitrary",),
                                             vmem_limit_bytes=VMEM_LIMIT),
        name="rwkv_scan",
    )(r, k2, v, na, b, logw)


def _even_out_kernel(h_ref, o_ref, r_ref, k_ref, v_ref, g_ref, z_ref, lnw_ref, lnb_ref, rk_ref, seg_ref,
                     wout_ref, g1_ref, g2_ref, g3_ref, w1_ref, w2_ref, out_ref):
    inv_n = 1.0 / RWKV_HEAD
    o = o_ref[...]
    mean = _head_sum(o, seg_ref) * inv_n
    dev = o - mean
    var = _head_sum(dev * dev, seg_ref) * inv_n
    on = dev * lax.rsqrt(var + GN_EPS) * lnw_ref[...] + lnb_ref[...]
    bonus = _head_sum(r_ref[...] * k_ref[...] * rk_ref[...], seg_ref) * v_ref[...]
    om = (on + bonus) * g_ref[...]
    rw = RWKV_WIDTH
    m = _dot(om, wout_ref[:rw, :]) + _dot(z_ref[...], wout_ref[rw:, :])
    hm = h_ref[...] + _rms(m, g1_ref[...], RMS_EPS)
    out_ref[...] = _mlp_residual(hm, g2_ref[...], g3_ref[...], w1_ref, w2_ref)


def _weight_spec(arr):
    return pl.BlockSpec(arr.shape, lambda i: (0,) * arr.ndim, pipeline_mode=pl.Buffered(1))


def _even_out(h, o, r, k2, v, g, z, ln_w, ln_b, r_k, seg, w_out, g1, g2, g3, w1, w2):
    lp = h.shape[0]
    tm = _pick_tile(lp, (640, 256))
    row_spec = lambda width: pl.BlockSpec((tm, width), lambda i: (i, 0))
    full = lambda arr: pl.BlockSpec(arr.shape, lambda i: (0,) * arr.ndim)
    rw = RWKV_WIDTH
    return pl.pallas_call(
        _even_out_kernel,
        grid=(lp // tm,),
        in_specs=[row_spec(D_MODEL)] + [row_spec(rw)] * 6 +
                 [full(ln_w), full(ln_b), full(r_k), full(seg), _weight_spec(w_out), full(g1), full(g2),
                  full(g3), _weight_spec(w1), _weight_spec(w2)],
        out_specs=row_spec(D_MODEL),
        out_shape=jax.ShapeDtypeStruct((lp, D_MODEL), F32),
        compiler_params=pltpu.CompilerParams(dimension_semantics=("parallel",),
                                             vmem_limit_bytes=VMEM_LIMIT),
        name="even_out",
    )(h, o, r, k2, v, g, z, ln_w, ln_b, r_k, seg, w_out, g1, g2, g3, w1, w2)


def _odd_in_kernel(h_ref, g0_ref, w_ref, cos_ref, sin_lo_ref, sin_hi_ref, q_out, k_out, v_out, kmax_out):
    @pl.when(pl.program_id(0) == 0)
    def _():
        kmax_out[...] = jnp.zeros(kmax_out.shape, F32)

    hn = _rms(h_ref[...], g0_ref[...], RMS_EPS).astype(BF16)
    y = jnp.dot(hn, w_ref[...], preferred_element_type=F32)
    cos = cos_ref[...]
    sin_lo = sin_lo_ref[...]
    sin_hi = sin_hi_ref[...]
    half = DIFF_HEAD // 2

    def rope(t):
        return (t * cos + pltpu.roll(t, LANES - half, axis=1) * sin_lo + pltpu.roll(t, half, axis=1) * sin_hi)

    scale = DIFF_HEAD ** -0.5 * math.log2(math.e)
    tm = y.shape[0]
    lane = lax.broadcasted_iota(jnp.int32, (tm, LANES), 1)
    minus_one = jnp.where(lane == 0, -1.0, 0.0).astype(BF16)
    ones = jnp.ones((tm, LANES), BF16)
    for j in range(D_MODEL // LANES):
        cols = slice(j * LANES, (j + 1) * LANES)
        kcols = slice(D_MODEL + j * LANES, D_MODEL + (j + 1) * LANES)
        vcols = slice(2 * D_MODEL + j * LANES, 2 * D_MODEL + (j + 1) * LANES)
        wide = slice(2 * j * LANES, (2 * j + 1) * LANES)
        wide_hi = slice((2 * j + 1) * LANES, (2 * j + 2) * LANES)
        q_out[:, cols] = (rope(y[:, cols]) * scale).astype(BF16)
        kb = rope(y[:, kcols]).astype(BF16)
        k_out[:, wide] = kb
        k_out[:, wide_hi] = minus_one
        v_out[:, wide] = y[:, vcols].astype(BF16)
        v_out[:, wide_hi] = ones
        kf = kb.astype(F32)
        knorm = jnp.sqrt(jnp.max(jnp.sum(kf * kf, axis=-1, keepdims=True), axis=0, keepdims=True))
        kmax_out[:, cols] = jnp.maximum(kmax_out[:, cols], jnp.broadcast_to(knorm, (8, LANES)))


def _odd_in(h, g0, w, cos, sin_lo, sin_hi):
    lp, lpk = h.shape[0], cos.shape[0]
    tm = _pick_tile(lp, (256,))
    last = lp // tm - 1
    row_spec = lambda width: pl.BlockSpec((tm, width), lambda i: (i, 0))
    full = lambda arr: pl.BlockSpec(arr.shape, lambda i: (0,) * arr.ndim)
    wide = jax.ShapeDtypeStruct((lpk, 2 * D_MODEL), BF16)
    return pl.pallas_call(
        _odd_in_kernel,
        grid=(lpk // tm,),
        in_specs=[pl.BlockSpec((tm, D_MODEL), lambda i: (jnp.minimum(i, last), 0)), full(g0),
                  _weight_spec(w), row_spec(LANES), row_spec(LANES), row_spec(LANES)],
        out_specs=[row_spec(D_MODEL), row_spec(2 * D_MODEL), row_spec(2 * D_MODEL),
                   pl.BlockSpec((8, D_MODEL), lambda i: (0, 0))],
        out_shape=[jax.ShapeDtypeStruct((lpk, D_MODEL), BF16), wide, wide,
                   jax.ShapeDtypeStruct((8, D_MODEL), F32)],
        compiler_params=pltpu.CompilerParams(dimension_semantics=("arbitrary",),
                                             vmem_limit_bytes=VMEM_LIMIT),
        name="odd_in",
    )(h, g0, w, cos, sin_lo, sin_hi)


def _attn_kernel(lam_ref, sw_ref, kmax_ref, q_ref, k_ref, v_ref, o_ref, q2_scr, p_scr, acc_scr,
                 *, lam_init, tq, tk, nh):
    i = pl.program_id(1)
    lane = lax.broadcasted_iota(jnp.int32, (1, LANES), 1)
    mlo = lane < HALF
    nt = (((1,), (1,)), ((), ()))
    n_blocks = (i * tq) // tk + 1
    wl = 2 * LANES

    def diagonal_mask(off):
        qrow = lax.broadcasted_iota(jnp.int32, (2 * tq, tk), 0)
        qrow = jnp.where(qrow >= tq, qrow - tq, qrow)
        kcol = lax.broadcasted_iota(jnp.int32, (2 * tq, tk), 1)
        return kcol - qrow <= i * tq - off

    for hd in range(nh):
        q = q_ref[:, hd * LANES:(hd + 1) * LANES]
        zq = jnp.zeros_like(q)
        q2 = jnp.concatenate([jnp.where(mlo, q, zq), jnp.where(mlo, zq, q)], axis=0)
        q2f = q2.astype(F32)
        bound = (jnp.sqrt(jnp.sum(q2f * q2f, axis=-1, keepdims=True))
                 * kmax_ref[0:1, hd * LANES:hd * LANES + 1] * BOUND_SLACK)
        q2_scr[hd] = jnp.concatenate([q2, jnp.where(lane == 0, bound, 0.0).astype(BF16)], axis=1)

    def weights_into(hd, slot, t, diagonal):
        off = pl.multiple_of(t * tk, tk)
        s = lax.dot_general(q2_scr[hd], k_ref[pl.ds(off, tk), hd * wl:(hd + 1) * wl], nt,
                            preferred_element_type=F32)
        if diagonal:
            s = jnp.where(diagonal_mask(off), s, NEG_BIG)
        p_scr[hd, slot] = jnp.exp2(s).astype(BF16)

    def add_weighted_values(hd, slot, t):
        off = pl.multiple_of(t * tk, tk)
        acc_scr[hd] += jnp.dot(p_scr[hd, slot], v_ref[pl.ds(off, tk), hd * wl:(hd + 1) * wl],
                               preferred_element_type=F32)

    def tick(t, slot, diagonal):
        for hd in range(nh):
            weights_into(hd, slot, t, diagonal)
            add_weighted_values(hd, 1 - slot, t - 1)

    def drain(slot, t):
        for hd in range(nh):
            add_weighted_values(hd, slot, t)

    acc_scr[...] = jnp.zeros(acc_scr.shape, F32)

    @pl.when(n_blocks == 1)
    def _():
        for hd in range(nh):
            weights_into(hd, 0, 0, True)
        drain(0, 0)

    @pl.when(n_blocks > 1)
    def _():
        for hd in range(nh):
            weights_into(hd, 0, 0, False)

    def pair(u, carry):
        tick(2 * u + 1, 1, False)
        tick(2 * u + 2, 0, False)
        return carry

    n_pairs = jnp.maximum(n_blocks - 2, 0) // 2
    lax.fori_loop(0, n_pairs, pair, 0)
    t1 = 2 * n_pairs + 1

    @pl.when(jnp.logical_and(n_blocks > 1, n_blocks % 2 == 0))
    def _():
        tick(t1, 1, True)
        drain(1, t1)

    @pl.when(jnp.logical_and(n_blocks > 1, n_blocks % 2 == 1))
    def _():
        tick(t1, 1, False)
        tick(t1 + 1, 0, True)
        drain(0, t1 + 1)

    lv = lam_ref[...]
    lam = (jnp.exp(jnp.sum(lv[0:1] * lv[1:2], axis=-1, keepdims=True))
           - jnp.exp(jnp.sum(lv[2:3] * lv[3:4], axis=-1, keepdims=True)) + lam_init)

    def write_out(hd, acc):
        o = acc[:, :LANES] / acc[:, LANES:]
        o = o[:tq] - lam * o[tq:]
        o = _rms(o, sw_ref[...], SUBLN_EPS) * (1.0 - lam_init)
        o_ref[:, hd * LANES:(hd + 1) * LANES] = o.astype(BF16)

    for hd in range(nh):
        write_out(hd, acc_scr[hd])

    smallest = jnp.min(acc_scr[:, :, LANES:LANES + 1])
    @pl.when(jnp.logical_not(smallest >= MIN_ROW_SUM))
    def _():
        for hd in range(nh):
            q2 = q2_scr[hd, :, :LANES]

            def block(t, carry, diagonal):
                m, acc = carry
                off = pl.multiple_of(t * tk, tk)
                s = lax.dot_general(q2, k_ref[pl.ds(off, tk), hd * wl:hd * wl + LANES], nt,
                                    preferred_element_type=F32)
                if diagonal:
                    s = jnp.where(diagonal_mask(off), s, NEG_BIG)
                m_new = jnp.maximum(m, jnp.max(s, axis=-1, keepdims=True))
                p = jnp.exp2(s - m_new).astype(BF16)
                pv = jnp.dot(p, v_ref[pl.ds(off, tk), hd * wl:(hd + 1) * wl], preferred_element_type=F32)
                return m_new, jnp.exp2(m - m_new) * acc + pv

            init = (jnp.full((2 * tq, 1), NEG_BIG, F32), jnp.zeros((2 * tq, wl), F32))
            carry = lax.fori_loop(0, n_blocks - 1, lambda t, c: block(t, c, False), init)
            write_out(hd, block(n_blocks - 1, carry, True)[1])


def _diff_attn(q, k, v, kmax, lam_vecs, subln_w, lam_init, lp):
    tq, tk, nh = ATT_Q_BLOCK, ATT_K_BLOCK, ATT_HEADS
    lpk = k.shape[0]
    blk = pl.BlockSpec((tq, nh * LANES), lambda h, i: (i, h))
    resident = pl.BlockSpec((lpk, 2 * nh * LANES), lambda h, i: (0, h), pipeline_mode=pl.Buffered(1))
    full = lambda arr: pl.BlockSpec(arr.shape, lambda h, i: (0,) * arr.ndim)
    return pl.pallas_call(
        functools.partial(_attn_kernel, lam_init=lam_init, tq=tq, tk=tk, nh=nh),
        grid=(DIFF_HEADS // nh, lp // tq),
        in_specs=[full(lam_vecs), full(subln_w), pl.BlockSpec((8, nh * LANES), lambda h, i: (0, h)),
                  blk, resident, resident],
        out_specs=blk,
        out_shape=jax.ShapeDtypeStruct((lp, D_MODEL), BF16),
        scratch_shapes=[pltpu.VMEM((nh, 2 * tq, 2 * LANES), BF16), pltpu.VMEM((nh, 2, 2 * tq, tk), BF16),
                        pltpu.VMEM((nh, 2 * tq, 2 * LANES), F32)],
        compiler_params=pltpu.CompilerParams(dimension_semantics=("parallel", "arbitrary"),
                                             vmem_limit_bytes=VMEM_LIMIT),
        name="diff_attn",
    )(lam_vecs, subln_w, kmax, q, k, v)


def _odd_out_kernel(h_ref, o_ref, wout_ref, g1_ref, g2_ref, g3_ref, w1_ref, w2_ref, out_ref):
    m = jnp.dot(o_ref[...], wout_ref[...], preferred_element_type=F32)
    hm = h_ref[...] + _rms(m, g1_ref[...], RMS_EPS)
    out_ref[...] = _mlp_residual(hm, g2_ref[...], g3_ref[...], w1_ref, w2_ref)


def _odd_out(h, o, w_out, g1, g2, g3, w1, w2):
    lp = h.shape[0]
    tm = _pick_tile(lp, (640, 256))
    row_spec = lambda width: pl.BlockSpec((tm, width), lambda i: (i, 0))
    full = lambda arr: pl.BlockSpec(arr.shape, lambda i: (0,) * arr.ndim)
    return pl.pallas_call(
        _odd_out_kernel,
        grid=(lp // tm,),
        in_specs=[row_spec(D_MODEL), row_spec(D_MODEL), _weight_spec(w_out), full(g1), full(g2), full(g3),
                  _weight_spec(w1), _weight_spec(w2)],
        out_specs=row_spec(D_MODEL),
        out_shape=jax.ShapeDtypeStruct((lp, D_MODEL), F32),
        compiler_params=pltpu.CompilerParams(dimension_semantics=("parallel",),
                                             vmem_limit_bytes=VMEM_LIMIT),
        name="odd_out",
    )(h, o, w_out, g1, g2, g3, w1, w2)


def _forward(x, meta, norm_g, mlp_w1, mlp_w2, ev, od):
    seq = x.shape[0]
    length = N_META + seq
    lp = -(-length // ROW_ALIGN) * ROW_ALIGN
    h = jnp.concatenate([meta.astype(x.dtype), x, jnp.zeros((lp - length, D_MODEL), x.dtype)], axis=0)

    lpk = -(-lp // ATT_K_BLOCK) * ATT_K_BLOCK
    pos = jnp.arange(lpk, dtype=F32)
    inv = ROPE_THETA ** (-jnp.arange(0, DIFF_HEAD, 2, dtype=F32) / DIFF_HEAD)
    ang = pos[:, None] * inv[None, :]
    ang = jnp.concatenate([ang, ang, ang, ang], axis=-1)
    cos, sin = jnp.cos(ang), jnp.sin(ang)
    first_half = (jnp.arange(LANES) % DIFF_HEAD) < DIFF_HEAD // 2
    sin_lo = jnp.where(first_half, -sin, 0.0)
    sin_hi = jnp.where(first_half, 0.0, sin)

    head_id = jnp.arange(RWKV_WIDTH) // RWKV_HEAD
    seg = (head_id[:, None] == head_id[None, :]).astype(BF16)
    row2 = lambda t: t.reshape(1, -1)

    depth = norm_g.shape[0]
    for i in range(depth):
        g = norm_g[i]
        g0, g1, g2, g3 = (row2(g[n]) for n in range(4))
        w1 = mlp_w1[i].astype(BF16)
        w2 = mlp_w2[i].astype(BF16)
        j = i // 2
        if i % 2 == 0:
            (w_in, mu, w0, w_up, a0, a_up, g_up, k_k, k_a, r_k, ln_w, ln_b, pool_w, pool_scale,
             w_out) = (t[j] for t in ev)
            zeros = jnp.zeros((DECAY_RANK, RWKV_WIDTH), F32)
            lora_w = jnp.concatenate([jnp.concatenate([w_up, zeros], axis=1),
                                      jnp.concatenate([zeros, a_up], axis=1)], axis=0).astype(BF16)
            r, k2, v, na, b, logw, gate, z = _even_in(
                h, g0, w_in.astype(BF16), row2(mu), row2(w0), lora_w, row2(a0), g_up.astype(BF16),
                row2(k_k), row2(k_a), seg, pool_w.astype(BF16), row2(pool_scale))
            o = _rwkv_scan(r, k2, v, na, b, logw)
            h = _even_out(h, o, r, k2, v, gate, z, row2(ln_w), row2(ln_b), row2(r_k), seg,
                          w_out.astype(BF16), g1, g2, g3, w1, w2)
        else:
            w_in, lam_vecs, subln_w, w_out = (t[j] for t in od)
            q, k, v, kmax = _odd_in(h, g0, w_in.astype(BF16), cos, sin_lo, sin_hi)
            lam_init = 0.8 - 0.6 * math.exp(-0.3 * i)
            o = _diff_attn(q, k, v, kmax, lam_vecs, row2(subln_w), lam_init, lp)
            h = _odd_out(h, o, w_out.astype(BF16), g1, g2, g3, w1, w2)
    return h[N_META:length]


def kernel(x, meta, norm_g, mlp_w1, mlp_w2, ev_w_in, ev_mu, ev_w0, ev_w_up, ev_a0, ev_a_up, ev_g_up, ev_k_k,
           ev_k_a, ev_r_k, ev_ln_w, ev_ln_b, ev_pool_w, ev_pool_scale, ev_w_out, od_w_in, od_lambda,
           od_subln_w, od_w_out):
    ev = (ev_w_in, ev_mu, ev_w0, ev_w_up, ev_a0, ev_a_up, ev_g_up, ev_k_k, ev_k_a, ev_r_k, ev_ln_w, ev_ln_b,
          ev_pool_w, ev_pool_scale, ev_w_out)
    od = (od_w_in, od_lambda, od_subln_w, od_w_out)
    outs = [_forward(x[bi], meta, norm_g, mlp_w1, mlp_w2, ev, od) for bi in range(x.shape[0])]
    return jnp.stack(outs, axis=0)
```

```python
import functools
import math

import jax
import jax.numpy as jnp
from jax import lax
from jax.experimental import pallas as pl
from jax.experimental.pallas import tpu as pltpu

F32, BF16 = jnp.float32, jnp.bfloat16

D_MODEL = 1024
N_META = 16
RMS_EPS = 1e-6
D_FF = 4 * D_MODEL
RWKV_HEAD = 64
RWKV_WIDTH = D_MODEL // 2
DECAY_RANK = 64
ICLR_RANK = 64
GATE_RANK = 128
GN_EPS = RWKV_HEAD * 1e-5
POOL_WIDTH = D_MODEL - RWKV_WIDTH
POOL_WINDOWS = (2, 4, 8, 16)
POOL_GROUP_W = POOL_WIDTH // len(POOL_WINDOWS)
POOL_CARRY = 16
SHIFT_WIDTH = 3 * RWKV_WIDTH + DECAY_RANK + ICLR_RANK + GATE_RANK
EVEN_IN = SHIFT_WIDTH + POOL_WIDTH
DIFF_HEADS = 8
DIFF_HEAD = D_MODEL // (2 * DIFF_HEADS)
SUBLN_EPS = 1e-5
ROPE_THETA = 10000.0

LANES = 128
SUBLANES = 8
HALF = LANES // 2
ROW_ALIGN = 256
ROW_TILES = (640, ROW_ALIGN)
KK_NORM_FLOOR = 1e-12
CHUNK = 64
SOLVE_BLOCK = 8
SCAN_CHUNKS = 4
ATT_Q_BLOCK = 256
ATT_K_BLOCK = 1024
ATT_HEADS = 2
FF_CHUNK = 1024
NEG_BIG = -1e30
BOUND_SLACK = 1.02
MIN_ROW_SUM = 2.0 ** -100
VMEM_LIMIT = 56 * 1024 * 1024


def _pick_tile(n, candidates):
    for c in candidates:
        if n % c == 0:
            return c
    raise ValueError(f"no tile in {candidates} divides {n}")


def _rms(t, g, eps):
    return t * lax.rsqrt(jnp.mean(t * t, axis=-1, keepdims=True) + eps) * g


def _split2(x):
    hi = x.astype(BF16)
    lo = (x - hi.astype(F32)).astype(BF16)
    return hi, lo


def _dot(a, b):
    return jnp.dot(a.astype(BF16), b.astype(BF16), preferred_element_type=F32)


def _dot2(a, b):
    ab = a.astype(BF16)
    bh, bl = _split2(b)
    d = functools.partial(jnp.dot, preferred_element_type=F32)
    return d(ab, bh) + d(ab, bl)


def _dot_nt(a, b):
    return lax.dot_general(a.astype(BF16), b.astype(BF16), (((1,), (1,)), ((), ())),
                           preferred_element_type=F32)


def _head_sum(x, seg_ref):
    xh, xl = _split2(x)
    d = functools.partial(jnp.dot, preferred_element_type=F32)
    return d(xh, seg_ref[...]) + d(xl, seg_ref[...])


def _sigmoid(x):
    return 1.0 / (1.0 + jnp.exp(-x))


def _softplus(x):
    return jnp.maximum(x, 0.0) + jnp.log(1.0 + jnp.exp(-jnp.abs(x)))


def _mlp_residual(hm, g2, g3, w1_ref, w2_ref):
    n = _rms(hm, g2, RMS_EPS).astype(BF16)
    acc = jnp.zeros(hm.shape, F32)
    for c in range(D_FF // FF_CHUNK):
        cols = slice(c * FF_CHUNK, (c + 1) * FF_CHUNK)
        a = jnp.dot(n, w1_ref[:, cols], preferred_element_type=F32)
        a = jnp.square(jnp.maximum(a, 0.0)).astype(BF16)
        acc = acc + jnp.dot(a, w2_ref[cols, :], preferred_element_type=F32)
    return hm + _rms(acc, g3, RMS_EPS)


def _even_in_kernel(h_ref, g0_ref, win_ref, mu_ref, w0_ref, lora_ref, a0_ref, gup_ref,
                    kk_ref, ka_ref, seg_ref, poolw_ref, pscale_ref,
                    r_out, k_out, v_out, na_out, b_out, lw_out, g_out, z_out,
                    ycarry, ucarry, *, tm):
    i = pl.program_id(0)

    @pl.when(i == 0)
    def _():
        ycarry[...] = jnp.zeros(ycarry.shape, F32)
        ucarry[...] = jnp.zeros(ucarry.shape, F32)

    hn = _rms(h_ref[...], g0_ref[...], RMS_EPS).astype(BF16)
    y = jnp.dot(hn, win_ref[...], preferred_element_type=F32)

    ysh = y[:, :SHIFT_WIDTH]
    row = lax.broadcasted_iota(jnp.int32, (tm, 1), 0)
    prev = jnp.where(row == 0, ycarry[SUBLANES - 1:SUBLANES, :], pltpu.roll(ysh, 1, axis=0))
    ycarry[...] = ysh[tm - SUBLANES:, :]
    ys = ysh + (prev - ysh) * mu_ref[...]

    rw = RWKV_WIDTH
    r = ys[:, 0:rw]
    k = ys[:, rw:2 * rw]
    v = ys[:, 2 * rw:3 * rw]
    wa = ys[:, 3 * rw:3 * rw + LANES]
    gd = ys[:, 3 * rw + LANES:SHIFT_WIDTH]

    lane = lax.broadcasted_iota(jnp.int32, (1, LANES), 1)
    lora_in = jnp.where(lane < DECAY_RANK, jnp.tanh(wa), wa)
    lora = _dot(lora_in, lora_ref[...])
    wlog = -_softplus(-(w0_ref[...] + lora[:, :rw])) - 0.5
    logw = -jnp.exp(wlog)
    a = _sigmoid(a0_ref[...] + lora[:, rw:])
    g = _dot(_sigmoid(gd), gup_ref[...])

    kk = k * kk_ref[...]
    kk = kk * lax.rsqrt(jnp.maximum(_head_sum(kk * kk, seg_ref), KK_NORM_FLOOR ** 2))
    k2 = k * (1.0 + (a - 1.0) * ka_ref[...])

    r_out[...] = r
    k_out[...] = k2
    v_out[...] = v
    na_out[...] = -kk
    b_out[...] = kk * a
    lw_out[...] = logw
    g_out[...] = g

    u = y[:, SHIFT_WIDTH:]
    ext = jnp.concatenate([ucarry[...], u], axis=0)
    ucarry[...] = u[tm - POOL_CARRY:, :]
    t_idx = i * tm + row
    for gi, win in enumerate(POOL_WINDOWS):
        cols = slice(gi * POOL_GROUP_W, (gi + 1) * POOL_GROUP_W)
        s = ext[:, cols]
        span = 1
        while span < win:
            s = s + pltpu.roll(s, span, axis=0)
            span *= 2
        cnt = jnp.minimum(t_idx + 1, win).astype(F32)
        d = s[POOL_CARRY:, :] / cnt - u[:, cols]
        z_out[:, cols] = _dot(d, poolw_ref[gi]) * pscale_ref[:, cols]


def _even_in(h, g0, w_in, mu, w0, lora_w, a0, g_up, k_k, k_a, seg, pool_w, pool_scale):
    lp = h.shape[0]
    tm = _pick_tile(lp, ROW_TILES)
    rw = RWKV_WIDTH
    row_spec = lambda width: pl.BlockSpec((tm, width), lambda i: (i, 0))
    full = lambda arr: pl.BlockSpec(arr.shape, lambda i: (0,) * arr.ndim)
    out_sds = jax.ShapeDtypeStruct((lp, rw), F32)
    return pl.pallas_call(
        functools.partial(_even_in_kernel, tm=tm),
        grid=(lp // tm,),
        in_specs=[row_spec(D_MODEL), full(g0), full(w_in), full(mu), full(w0), full(lora_w), full(a0),
                  full(g_up), full(k_k), full(k_a), full(seg), full(pool_w), full(pool_scale)],
        out_specs=[row_spec(rw)] * 8,
        out_shape=[out_sds] * 8,
        scratch_shapes=[pltpu.VMEM((SUBLANES, SHIFT_WIDTH), F32), pltpu.VMEM((POOL_CARRY, POOL_WIDTH), F32)],
        compiler_params=pltpu.CompilerParams(dimension_semantics=("arbitrary",),
                                             vmem_limit_bytes=VMEM_LIMIT),
        name="even_in",
    )(h, g0, w_in, mu, w0, lora_w, a0, g_up, k_k, k_a, seg, pool_w, pool_scale)


def _scan_kernel(r_ref, k_ref, v_ref, na_ref, b_ref, lw_ref, o_ref, h_scr):
    @pl.when(pl.program_id(0) == 0)
    def _():
        h_scr[...] = jnp.zeros(h_scr.shape, F32)

    c = CHUNK
    n_chunks = r_ref.shape[0] // c
    row = lax.broadcasted_iota(jnp.int32, (c, c), 0)
    col = lax.broadcasted_iota(jnp.int32, (c, c), 1)
    tri = jnp.where(col <= row, 1.0, 0.0).astype(BF16)
    lane = lax.broadcasted_iota(jnp.int32, (1, LANES), 1)
    mlo = lane < HALF
    prow = lax.broadcasted_iota(jnp.int32, (LANES, LANES), 0)
    pcol = lax.broadcasted_iota(jnp.int32, (LANES, LANES), 1)
    same_head = (prow < HALF) == (pcol < HALF)
    diag = prow == pcol
    trow = lax.broadcasted_iota(jnp.int32, (c, LANES), 0)
    tcol = lax.broadcasted_iota(jnp.int32, (c, LANES), 1)
    tcol = jnp.where(tcol >= HALF, tcol - HALF, tcol)
    strict2 = tcol < trow
    incl2 = jnp.concatenate([tcol <= trow] * 2, axis=0)
    same_blk = (tcol // SOLVE_BLOCK) == (trow // SOLVE_BLOCK)
    lane2 = lax.broadcasted_iota(jnp.int32, (1, 2 * LANES), 1)
    zeros_c = jnp.zeros((c, LANES), F32)
    zeros_2c = jnp.zeros((c, 2 * LANES), F32)
    n_pairs = RWKV_WIDTH // LANES
    pair_cols = [slice(p * LANES, (p + 1) * LANES) for p in range(n_pairs)]
    d = functools.partial(jnp.dot, preferred_element_type=F32)

    prep = []
    for ci in range(n_chunks):
        rows = slice(ci * c, (ci + 1) * c)
        lw = lw_ref[rows, :]
        lw_hi = lw.astype(BF16)
        lw_r = lw - lw_hi.astype(F32)
        lw_mid = lw_r.astype(BF16)
        lw_lo = (lw_r - lw_mid.astype(F32)).astype(BF16)
        cum = d(tri, lw_hi) + (d(tri, lw_mid) + d(tri, lw_lo))
        cum_end = cum[c - 1:c, :]
        e_neg = jnp.exp(-cum)
        e_end = jnp.exp(cum_end - cum)
        b_all = b_ref[rows, :]
        k_all = k_ref[rows, :]
        prep.append(dict(r_t=r_ref[rows, :] * jnp.exp(cum), a_t=na_ref[rows, :] * jnp.exp(cum - lw),
                         b_t=b_all * e_neg, k_t=k_all * e_neg, b_h=b_all * e_end, k_h=k_all * e_end,
                         v=v_ref[rows, :], p_end=jnp.exp(cum_end)))
    units = [(ci, p) for ci in range(n_chunks) for p in range(n_pairs)]

    a_all = {}
    for ci, p in units:
        cols = pair_cols[p]
        rt, at = prep[ci]["r_t"][:, cols], prep[ci]["a_t"][:, cols]
        lhs4 = jnp.concatenate([jnp.where(mlo, at, zeros_c), jnp.where(mlo, zeros_c, at),
                                jnp.where(mlo, rt, zeros_c), jnp.where(mlo, zeros_c, rt)], axis=0)
        a_all[ci, p] = _dot_nt(lhs4, jnp.concatenate([prep[ci]["b_t"][:, cols], prep[ci]["k_t"][:, cols]], axis=0))

    heads = []
    for ci, p in units:
        cols = pair_cols[p]
        at, vp = prep[ci]["a_t"][:, cols], prep[ci]["v"][:, cols]
        at_sw = pltpu.roll(at, HALF, axis=1)
        vp_sw = pltpu.roll(vp, HALF, axis=1)
        for hh in range(2):
            nk = jnp.where(strict2, a_all[ci, p][hh * c:(hh + 1) * c], 0.0)
            av = _dot(nk, jnp.concatenate([zeros_c, vp_sw if hh == 0 else vp], axis=0))
            x0 = jnp.where(mlo, at if hh == 0 else at_sw, av)
            nk_sw = pltpu.roll(nk, HALF, axis=1)
            n_split = jnp.where(mlo, jnp.where(same_blk, 0.0, nk), jnp.where(same_blk, nk_sw, 0.0))
            heads.append(jnp.concatenate([x0, n_split], axis=1))

    for _ in range(int(math.log2(SOLVE_BLOCK))):
        nxt = []
        for y in heads:
            prod = _dot2(y[:, LANES:], jnp.concatenate([zeros_2c, y], axis=0))
            nxt.append(jnp.where(lane2 >= LANES + HALF, prod, y + prod))
        heads = nxt
    for _ in range(int(math.log2(c // SOLVE_BLOCK))):
        nxt = []
        for y in heads:
            prod = _dot2(y[:, LANES:], jnp.concatenate([y, zeros_2c], axis=0))
            nxt.append(jnp.where(lane2 < LANES, y + prod, prod))
        heads = nxt

    big = {}
    for n, (ci, p) in enumerate(units):
        cols = pair_cols[p]
        x_lo, x_hi = heads[2 * n][:, :LANES], heads[2 * n + 1][:, :LANES]
        w_p = jnp.where(mlo, x_lo, pltpu.roll(x_hi, HALF, axis=1))
        u0_p = jnp.where(mlo, pltpu.roll(x_lo, HALF, axis=1), x_hi)
        rhs = jnp.concatenate([jnp.concatenate([w_p, u0_p], axis=1),
                               jnp.concatenate([zeros_c, prep[ci]["v"][:, cols]], axis=1)], axis=0)
        a_r = jnp.where(incl2, a_all[ci, p][2 * c:], 0.0)
        bk_t = jnp.concatenate([prep[ci]["b_h"][:, cols], prep[ci]["k_h"][:, cols]], axis=0).T
        big[ci, p] = _dot(jnp.concatenate([a_r, bk_t], axis=0), rhs)

    state = [h_scr[p] for p in range(n_pairs)]
    for ci in range(n_chunks):
        rows = slice(ci * c, (ci + 1) * c)
        for p, cols in enumerate(pair_cols):
            res = big[ci, p]
            q_hat = prep[ci]["r_t"][:, cols] + jnp.where(mlo, res[:c, :LANES], res[c:2 * c, :LANES])
            o_hat = jnp.where(mlo, res[:c, LANES:], res[c:2 * c, LANES:])
            g_mat = (jnp.where(same_head, res[2 * c:, :LANES], 0.0)
                     + jnp.where(diag, prep[ci]["p_end"][:, cols], 0.0))
            j_mat = jnp.where(same_head, res[2 * c:, LANES:], 0.0)
            st = _dot2(jnp.concatenate([q_hat, g_mat], axis=0), state[p])
            o_ref[rows, cols] = st[:c] + o_hat
            state[p] = st[c:] + j_mat
    for p in range(n_pairs):
        h_scr[p] = state[p]


def _rwkv_scan(r, k2, v, na, b, logw):
    lp, rw = r.shape
    rows = CHUNK * SCAN_CHUNKS
    assert lp % rows == 0
    spec = pl.BlockSpec((rows, rw), lambda i: (i, 0))
    return pl.pallas_call(
        _scan_kernel,
        grid=(lp // rows,),
        in_specs=[spec] * 6,
        out_specs=spec,
        out_shape=jax.ShapeDtypeStruct((lp, rw), F32),
        scratch_shapes=[pltpu.VMEM((rw // LANES, LANES, LANES), F32)],
        compiler_params=pltpu.CompilerParams(dimension_semantics=("arbitrary",),
                                             vmem_limit_bytes=VMEM_LIMIT),
        name="rwkv_scan",
    )(r, k2, v, na, b, logw)


def _even_out_kernel(h_ref, o_ref, r_ref, k_ref, v_ref, g_ref, z_ref, lnw_ref, lnb_ref, rk_ref, seg_ref,
                     wout_ref, g1_ref, g2_ref, g3_ref, w1_ref, w2_ref, out_ref):
    inv_n = 1.0 / RWKV_HEAD
    o = o_ref[...]
    mean = _head_sum(o, seg_ref) * inv_n
    dev = o - mean
    var = _head_sum(dev * dev, seg_ref) * inv_n
    on = dev * lax.rsqrt(var + GN_EPS) * lnw_ref[...] + lnb_ref[...]
    bonus = _head_sum(r_ref[...] * k_ref[...] * rk_ref[...], seg_ref) * v_ref[...]
    om = (on + bonus) * g_ref[...]
    rw = RWKV_WIDTH
    m = _dot(om, wout_ref[:rw, :]) + _dot(z_ref[...], wout_ref[rw:, :])
    hm = h_ref[...] + _rms(m, g1_ref[...], RMS_EPS)
    out_ref[...] = _mlp_residual(hm, g2_ref[...], g3_ref[...], w1_ref, w2_ref)


def _weight_spec(arr):
    return pl.BlockSpec(arr.shape, lambda i: (0,) * arr.ndim, pipeline_mode=pl.Buffered(1))


def _even_out(h, o, r, k2, v, g, z, ln_w, ln_b, r_k, seg, w_out, g1, g2, g3, w1, w2):
    lp = h.shape[0]
    tm = _pick_tile(lp, ROW_TILES)
    row_spec = lambda width: pl.BlockSpec((tm, width), lambda i: (i, 0))
    full = lambda arr: pl.BlockSpec(arr.shape, lambda i: (0,) * arr.ndim)
    rw = RWKV_WIDTH
    return pl.pallas_call(
        _even_out_kernel,
        grid=(lp // tm,),
        in_specs=[row_spec(D_MODEL)] + [row_spec(rw)] * 6 +
                 [full(ln_w), full(ln_b), full(r_k), full(seg), _weight_spec(w_out), full(g1), full(g2),
                  full(g3), _weight_spec(w1), _weight_spec(w2)],
        out_specs=row_spec(D_MODEL),
        out_shape=jax.ShapeDtypeStruct((lp, D_MODEL), F32),
        compiler_params=pltpu.CompilerParams(dimension_semantics=("parallel",),
                                             vmem_limit_bytes=VMEM_LIMIT),
        name="even_out",
    )(h, o, r, k2, v, g, z, ln_w, ln_b, r_k, seg, w_out, g1, g2, g3, w1, w2)


def _odd_in_kernel(h_ref, g0_ref, w_ref, cos_ref, sin_lo_ref, sin_hi_ref, q_out, k_out, v_out, kmax_out):
    @pl.when(pl.program_id(0) == 0)
    def _():
        kmax_out[...] = jnp.zeros(kmax_out.shape, F32)

    hn = _rms(h_ref[...], g0_ref[...], RMS_EPS).astype(BF16)
    y = jnp.dot(hn, w_ref[...], preferred_element_type=F32)
    cos = cos_ref[...]
    sin_lo = sin_lo_ref[...]
    sin_hi = sin_hi_ref[...]
    half = DIFF_HEAD // 2

    def rope(t):
        return (t * cos + pltpu.roll(t, LANES - half, axis=1) * sin_lo + pltpu.roll(t, half, axis=1) * sin_hi)

    scale = DIFF_HEAD ** -0.5 * math.log2(math.e)
    tm = y.shape[0]
    lane = lax.broadcasted_iota(jnp.int32, (tm, LANES), 1)
    minus_one = jnp.where(lane == 0, -1.0, 0.0).astype(BF16)
    ones = jnp.ones((tm, LANES), BF16)
    for j in range(D_MODEL // LANES):
        cols = slice(j * LANES, (j + 1) * LANES)
        kcols = slice(D_MODEL + j * LANES, D_MODEL + (j + 1) * LANES)
        vcols = slice(2 * D_MODEL + j * LANES, 2 * D_MODEL + (j + 1) * LANES)
        wide = slice(2 * j * LANES, (2 * j + 1) * LANES)
        wide_hi = slice((2 * j + 1) * LANES, (2 * j + 2) * LANES)
        q_out[:, cols] = (rope(y[:, cols]) * scale).astype(BF16)
        kb = rope(y[:, kcols]).astype(BF16)
        k_out[:, wide] = kb
        k_out[:, wide_hi] = minus_one
        v_out[:, wide] = y[:, vcols].astype(BF16)
        v_out[:, wide_hi] = ones
        kf = kb.astype(F32)
        knorm = jnp.sqrt(jnp.max(jnp.sum(kf * kf, axis=-1, keepdims=True), axis=0, keepdims=True))
        kmax_out[:, cols] = jnp.maximum(kmax_out[:, cols], jnp.broadcast_to(knorm, (SUBLANES, LANES)))


def _odd_in(h, g0, w, cos, sin_lo, sin_hi):
    lp, lpk = h.shape[0], cos.shape[0]
    tm = ROW_ALIGN
    last = lp // tm - 1
    row_spec = lambda width: pl.BlockSpec((tm, width), lambda i: (i, 0))
    full = lambda arr: pl.BlockSpec(arr.shape, lambda i: (0,) * arr.ndim)
    wide = jax.ShapeDtypeStruct((lpk, 2 * D_MODEL), BF16)
    return pl.pallas_call(
        _odd_in_kernel,
        grid=(lpk // tm,),
        in_specs=[pl.BlockSpec((tm, D_MODEL), lambda i: (jnp.minimum(i, last), 0)), full(g0),
                  _weight_spec(w), row_spec(LANES), row_spec(LANES), row_spec(LANES)],
        out_specs=[row_spec(D_MODEL), row_spec(2 * D_MODEL), row_spec(2 * D_MODEL),
                   pl.BlockSpec((SUBLANES, D_MODEL), lambda i: (0, 0))],
        out_shape=[jax.ShapeDtypeStruct((lpk, D_MODEL), BF16), wide, wide,
                   jax.ShapeDtypeStruct((SUBLANES, D_MODEL), F32)],
        compiler_params=pltpu.CompilerParams(dimension_semantics=("arbitrary",),
                                             vmem_limit_bytes=VMEM_LIMIT),
        name="odd_in",
    )(h, g0, w, cos, sin_lo, sin_hi)


def _attn_kernel(lam_ref, sw_ref, kmax_ref, q_ref, k_ref, v_ref, o_ref, q2_scr, p_scr, acc_scr,
                 *, lam_init, tq, tk, nh):
    i = pl.program_id(1)
    lane = lax.broadcasted_iota(jnp.int32, (1, LANES), 1)
    mlo = lane < HALF
    nt = (((1,), (1,)), ((), ()))
    n_blocks = (i * tq) // tk + 1
    wl = 2 * LANES

    def diagonal_mask(off):
        qrow = lax.broadcasted_iota(jnp.int32, (2 * tq, tk), 0)
        qrow = jnp.where(qrow >= tq, qrow - tq, qrow)
        kcol = lax.broadcasted_iota(jnp.int32, (2 * tq, tk), 1)
        return kcol - qrow <= i * tq - off

    for hd in range(nh):
        q = q_ref[:, hd * LANES:(hd + 1) * LANES]
        zq = jnp.zeros_like(q)
        q2 = jnp.concatenate([jnp.where(mlo, q, zq), jnp.where(mlo, zq, q)], axis=0)
        q2f = q2.astype(F32)
        bound = (jnp.sqrt(jnp.sum(q2f * q2f, axis=-1, keepdims=True))
                 * kmax_ref[0:1, hd * LANES:hd * LANES + 1] * BOUND_SLACK)
        q2_scr[hd] = jnp.concatenate([q2, jnp.where(lane == 0, bound, 0.0).astype(BF16)], axis=1)

    def weights_into(hd, slot, t, diagonal):
        off = pl.multiple_of(t * tk, tk)
        s = lax.dot_general(q2_scr[hd], k_ref[pl.ds(off, tk), hd * wl:(hd + 1) * wl], nt,
                            preferred_element_type=F32)
        if diagonal:
            s = jnp.where(diagonal_mask(off), s, NEG_BIG)
        p_scr[hd, slot] = jnp.exp2(s).astype(BF16)

    def add_weighted_values(hd, slot, t):
        off = pl.multiple_of(t * tk, tk)
        acc_scr[hd] += jnp.dot(p_scr[hd, slot], v_ref[pl.ds(off, tk), hd * wl:(hd + 1) * wl],
                               preferred_element_type=F32)

    def tick(t, slot, diagonal):
        for hd in range(nh):
            weights_into(hd, slot, t, diagonal)
            add_weighted_values(hd, 1 - slot, t - 1)

    acc_scr[...] = jnp.zeros(acc_scr.shape, F32)
    for hd in range(nh):
        weights_into(hd, 0, 0, True)

    def pair(u, carry):
        tick(2 * u + 1, 1, False)
        tick(2 * u + 2, 0, False)
        return carry

    n_pairs = jnp.maximum(n_blocks - 2, 0) // 2
    lax.fori_loop(0, n_pairs, pair, 0)
    t1 = 2 * n_pairs + 1

    lv = lam_ref[...]
    lam = (jnp.exp(jnp.sum(lv[0:1] * lv[1:2], axis=-1, keepdims=True))
           - jnp.exp(jnp.sum(lv[2:3] * lv[3:4], axis=-1, keepdims=True)) + lam_init)

    def write_out(hd, acc):
        o = acc[:, :LANES] / acc[:, LANES:]
        o = o[:tq] - lam * o[tq:]
        o = _rms(o, sw_ref[...], SUBLN_EPS) * (1.0 - lam_init)
        o_ref[:, hd * LANES:(hd + 1) * LANES] = o.astype(BF16)

    def drain_and_write(slot, t):
        for hd in range(nh):
            add_weighted_values(hd, slot, t)
            write_out(hd, acc_scr[hd])

    @pl.when(n_blocks == 1)
    def _():
        drain_and_write(0, 0)

    @pl.when(jnp.logical_and(n_blocks > 1, n_blocks % 2 == 0))
    def _():
        tick(t1, 1, True)
        drain_and_write(1, t1)

    @pl.when(jnp.logical_and(n_blocks > 1, n_blocks % 2 == 1))
    def _():
        tick(t1, 1, False)
        tick(t1 + 1, 0, True)
        drain_and_write(0, t1 + 1)

    smallest = jnp.min(acc_scr[:, :, LANES:LANES + 1])
    @pl.when(jnp.logical_not(smallest >= MIN_ROW_SUM))
    def _():
        for hd in range(nh):
            q2 = q2_scr[hd, :, :LANES]

            def block(t, carry, diagonal):
                m, acc = carry
                off = pl.multiple_of(t * tk, tk)
                s = lax.dot_general(q2, k_ref[pl.ds(off, tk), hd * wl:hd * wl + LANES], nt,
                                    preferred_element_type=F32)
                if diagonal:
                    s = jnp.where(diagonal_mask(off), s, NEG_BIG)
                m_new = jnp.maximum(m, jnp.max(s, axis=-1, keepdims=True))
                p = jnp.exp2(s - m_new).astype(BF16)
                pv = jnp.dot(p, v_ref[pl.ds(off, tk), hd * wl:(hd + 1) * wl], preferred_element_type=F32)
                return m_new, jnp.exp2(m - m_new) * acc + pv

            init = (jnp.full((2 * tq, 1), NEG_BIG, F32), jnp.zeros((2 * tq, wl), F32))
            carry = lax.fori_loop(0, n_blocks - 1, lambda t, c: block(t, c, False), init)
            write_out(hd, block(n_blocks - 1, carry, True)[1])


def _diff_attn(q, k, v, kmax, lam_vecs, subln_w, lam_init, lp):
    tq, tk, nh = ATT_Q_BLOCK, ATT_K_BLOCK, ATT_HEADS
    lpk = k.shape[0]
    assert lp % tq == 0 and tk % tq == 0 and lpk % tk == 0 and lpk >= lp and DIFF_HEADS % nh == 0
    blk = pl.BlockSpec((tq, nh * LANES), lambda h, i: (i, h))
    resident = pl.BlockSpec((lpk, 2 * nh * LANES), lambda h, i: (0, h), pipeline_mode=pl.Buffered(1))
    full = lambda arr: pl.BlockSpec(arr.shape, lambda h, i: (0,) * arr.ndim)
    return pl.pallas_call(
        functools.partial(_attn_kernel, lam_init=lam_init, tq=tq, tk=tk, nh=nh),
        grid=(DIFF_HEADS // nh, lp // tq),
        in_specs=[full(lam_vecs), full(subln_w), pl.BlockSpec((SUBLANES, nh * LANES), lambda h, i: (0, h)),
                  blk, resident, resident],
        out_specs=blk,
        out_shape=jax.ShapeDtypeStruct((lp, D_MODEL), BF16),
        scratch_shapes=[pltpu.VMEM((nh, 2 * tq, 2 * LANES), BF16), pltpu.VMEM((nh, 2, 2 * tq, tk), BF16),
                        pltpu.VMEM((nh, 2 * tq, 2 * LANES), F32)],
        compiler_params=pltpu.CompilerParams(dimension_semantics=("parallel", "arbitrary"),
                                             vmem_limit_bytes=VMEM_LIMIT),
        name="diff_attn",
    )(lam_vecs, subln_w, kmax, q, k, v)


def _odd_out_kernel(h_ref, o_ref, wout_ref, g1_ref, g2_ref, g3_ref, w1_ref, w2_ref, out_ref):
    m = jnp.dot(o_ref[...], wout_ref[...], preferred_element_type=F32)
    hm = h_ref[...] + _rms(m, g1_ref[...], RMS_EPS)
    out_ref[...] = _mlp_residual(hm, g2_ref[...], g3_ref[...], w1_ref, w2_ref)


def _odd_out(h, o, w_out, g1, g2, g3, w1, w2):
    lp = h.shape[0]
    tm = _pick_tile(lp, ROW_TILES)
    row_spec = lambda width: pl.BlockSpec((tm, width), lambda i: (i, 0))
    full = lambda arr: pl.BlockSpec(arr.shape, lambda i: (0,) * arr.ndim)
    return pl.pallas_call(
        _odd_out_kernel,
        grid=(lp // tm,),
        in_specs=[row_spec(D_MODEL), row_spec(D_MODEL), _weight_spec(w_out), full(g1), full(g2), full(g3),
                  _weight_spec(w1), _weight_spec(w2)],
        out_specs=row_spec(D_MODEL),
        out_shape=jax.ShapeDtypeStruct((lp, D_MODEL), F32),
        compiler_params=pltpu.CompilerParams(dimension_semantics=("parallel",),
                                             vmem_limit_bytes=VMEM_LIMIT),
        name="odd_out",
    )(h, o, w_out, g1, g2, g3, w1, w2)


def _forward(x, meta, norm_g, mlp_w1, mlp_w2, ev, od):
    seq = x.shape[0]
    length = N_META + seq
    lp = -(-length // ROW_ALIGN) * ROW_ALIGN
    h = jnp.concatenate([meta.astype(x.dtype), x, jnp.zeros((lp - length, D_MODEL), x.dtype)], axis=0)

    lpk = -(-lp // ATT_K_BLOCK) * ATT_K_BLOCK
    pos = jnp.arange(lpk, dtype=F32)
    inv = ROPE_THETA ** (-jnp.arange(0, DIFF_HEAD, 2, dtype=F32) / DIFF_HEAD)
    ang = pos[:, None] * inv[None, :]
    ang = jnp.concatenate([ang, ang, ang, ang], axis=-1)
    cos, sin = jnp.cos(ang), jnp.sin(ang)
    first_half = (jnp.arange(LANES) % DIFF_HEAD) < DIFF_HEAD // 2
    sin_lo = jnp.where(first_half, -sin, 0.0)
    sin_hi = jnp.where(first_half, 0.0, sin)

    head_id = jnp.arange(RWKV_WIDTH) // RWKV_HEAD
    seg = (head_id[:, None] == head_id[None, :]).astype(BF16)
    row2 = lambda t: t.reshape(1, -1)

    depth = norm_g.shape[0]
    for i in range(depth):
        g = norm_g[i]
        g0, g1, g2, g3 = (row2(g[n]) for n in range(4))
        w1 = mlp_w1[i].astype(BF16)
        w2 = mlp_w2[i].astype(BF16)
        j = i // 2
        if i % 2 == 0:
            (w_in, mu, w0, w_up, a0, a_up, g_up, k_k, k_a, r_k, ln_w, ln_b, pool_w, pool_scale,
             w_out) = (t[j] for t in ev)
            zeros = jnp.zeros((DECAY_RANK, RWKV_WIDTH), F32)
            lora_w = jnp.concatenate([jnp.concatenate([w_up, zeros], axis=1),
                                      jnp.concatenate([zeros, a_up], axis=1)], axis=0).astype(BF16)
            r, k2, v, na, b, logw, gate, z = _even_in(
                h, g0, w_in.astype(BF16), row2(mu), row2(w0), lora_w, row2(a0), g_up.astype(BF16),
                row2(k_k), row2(k_a), seg, pool_w.astype(BF16), row2(pool_scale))
            o = _rwkv_scan(r, k2, v, na, b, logw)
            h = _even_out(h, o, r, k2, v, gate, z, row2(ln_w), row2(ln_b), row2(r_k), seg,
                          w_out.astype(BF16), g1, g2, g3, w1, w2)
        else:
            w_in, lam_vecs, subln_w, w_out = (t[j] for t in od)
            q, k, v, kmax = _odd_in(h, g0, w_in.astype(BF16), cos, sin_lo, sin_hi)
            lam_init = 0.8 - 0.6 * math.exp(-0.3 * i)
            o = _diff_attn(q, k, v, kmax, lam_vecs, row2(subln_w), lam_init, lp)
            h = _odd_out(h, o, w_out.astype(BF16), g1, g2, g3, w1, w2)
    return h[N_META:length]


def kernel(x, meta, norm_g, mlp_w1, mlp_w2, ev_w_in, ev_mu, ev_w0, ev_w_up, ev_a0, ev_a_up, ev_g_up, ev_k_k,
           ev_k_a, ev_r_k, ev_ln_w, ev_ln_b, ev_pool_w, ev_pool_scale, ev_w_out, od_w_in, od_lambda,
           od_subln_w, od_w_out):
    ev = (ev_w_in, ev_mu, ev_w0, ev_w_up, ev_a0, ev_a_up, ev_g_up, ev_k_k, ev_k_a, ev_r_k, ev_ln_w, ev_ln_b,
          ev_pool_w, ev_pool_scale, ev_w_out)
    od = (od_w_in, od_lambda, od_subln_w, od_w_out)
    outs = [_forward(x[bi], meta, norm_g, mlp_w1, mlp_w2, ev, od) for bi in range(x.shape[0])]
    return jnp.stack(outs, axis=0)
```

```python
import functools
import math

import jax
import jax.numpy as jnp
from jax import lax
from jax.experimental import pallas as pl
from jax.experimental.pallas import tpu as pltpu

F32, BF16 = jnp.float32, jnp.bfloat16

D_MODEL = 1024
N_META = 16
RMS_EPS = 1e-6
D_FF = 4 * D_MODEL
RWKV_HEAD = 64
RWKV_WIDTH = D_MODEL // 2
DECAY_RANK = 64
ICLR_RANK = 64
GATE_RANK = 128
GN_EPS = RWKV_HEAD * 1e-5
POOL_WIDTH = D_MODEL - RWKV_WIDTH
POOL_WINDOWS = (2, 4, 8, 16)
POOL_GROUP_W = POOL_WIDTH // len(POOL_WINDOWS)
POOL_CARRY = 16
SHIFT_WIDTH = 3 * RWKV_WIDTH + DECAY_RANK + ICLR_RANK + GATE_RANK
EVEN_IN = SHIFT_WIDTH + POOL_WIDTH
DIFF_HEADS = 8
DIFF_HEAD = D_MODEL // (2 * DIFF_HEADS)
SUBLN_EPS = 1e-5
ROPE_THETA = 10000.0

LANES = 128
SUBLANES = 8
HALF = LANES // 2
ROW_ALIGN = 256
ROW_TILES = (640, ROW_ALIGN)
KK_NORM_FLOOR = 1e-12
CHUNK = 64
SOLVE_BLOCK = 8
SCAN_CHUNKS = 4
ATT_Q_BLOCK = 256
ATT_K_BLOCK = 1024
ATT_HEADS = 2
FF_CHUNK = 1024
NEG_BIG = -1e30
BOUND_SLACK = 1.02
MIN_ROW_SUM = 2.0 ** -100
VMEM_LIMIT = 56 * 1024 * 1024


def _pick_tile(n, candidates):
    for c in candidates:
        if n % c == 0:
            return c
    raise ValueError(f"no tile in {candidates} divides {n}")


def _rms(t, g, eps):
    return t * lax.rsqrt(jnp.mean(t * t, axis=-1, keepdims=True) + eps) * g


def _split2(x):
    hi = x.astype(BF16)
    lo = (x - hi.astype(F32)).astype(BF16)
    return hi, lo


def _dot(a, b):
    return jnp.dot(a.astype(BF16), b.astype(BF16), preferred_element_type=F32)


def _dot2(a, b):
    ab = a.astype(BF16)
    bh, bl = _split2(b)
    d = functools.partial(jnp.dot, preferred_element_type=F32)
    return d(ab, bh) + d(ab, bl)


def _dot_nt(a, b):
    return lax.dot_general(a.astype(BF16), b.astype(BF16), (((1,), (1,)), ((), ())),
                           preferred_element_type=F32)


def _head_sum(x, seg_ref):
    xh, xl = _split2(x)
    d = functools.partial(jnp.dot, preferred_element_type=F32)
    return d(xh, seg_ref[...]) + d(xl, seg_ref[...])


def _sigmoid(x):
    return 1.0 / (1.0 + jnp.exp(-x))


def _softplus(x):
    return jnp.maximum(x, 0.0) + jnp.log(1.0 + jnp.exp(-jnp.abs(x)))


def _mlp_residual(hm, g2, g3, w1_ref, w2_ref):
    n = _rms(hm, g2, RMS_EPS).astype(BF16)
    acc = jnp.zeros(hm.shape, F32)
    for c in range(D_FF // FF_CHUNK):
        cols = slice(c * FF_CHUNK, (c + 1) * FF_CHUNK)
        a = jnp.dot(n, w1_ref[:, cols], preferred_element_type=F32)
        a = jnp.square(jnp.maximum(a, 0.0)).astype(BF16)
        acc = acc + jnp.dot(a, w2_ref[cols, :], preferred_element_type=F32)
    return hm + _rms(acc, g3, RMS_EPS)


def _even_in_kernel(h_ref, g0_ref, win_ref, mu_ref, w0_ref, lora_ref, a0_ref, gup_ref,
                    kk_ref, ka_ref, seg_ref, poolw_ref, pscale_ref,
                    r_out, k_out, v_out, na_out, b_out, lw_out, g_out, z_out,
                    ycarry, ucarry, *, tm):
    i = pl.program_id(0)

    @pl.when(i == 0)
    def _():
        ycarry[...] = jnp.zeros(ycarry.shape, F32)
        ucarry[...] = jnp.zeros(ucarry.shape, F32)

    hn = _rms(h_ref[...], g0_ref[...], RMS_EPS).astype(BF16)
    y = jnp.dot(hn, win_ref[...], preferred_element_type=F32)

    ysh = y[:, :SHIFT_WIDTH]
    row = lax.broadcasted_iota(jnp.int32, (tm, 1), 0)
    prev = jnp.where(row == 0, ycarry[SUBLANES - 1:SUBLANES, :], pltpu.roll(ysh, 1, axis=0))
    ycarry[...] = ysh[tm - SUBLANES:, :]
    ys = ysh + (prev - ysh) * mu_ref[...]

    rw = RWKV_WIDTH
    r = ys[:, 0:rw]
    k = ys[:, rw:2 * rw]
    v = ys[:, 2 * rw:3 * rw]
    wa = ys[:, 3 * rw:3 * rw + LANES]
    gd = ys[:, 3 * rw + LANES:SHIFT_WIDTH]

    lane = lax.broadcasted_iota(jnp.int32, (1, LANES), 1)
    lora_in = jnp.where(lane < DECAY_RANK, jnp.tanh(wa), wa)
    lora = _dot(lora_in, lora_ref[...])
    wlog = -_softplus(-(w0_ref[...] + lora[:, :rw])) - 0.5
    logw = -jnp.exp(wlog)
    a = _sigmoid(a0_ref[...] + lora[:, rw:])
    g = _dot(_sigmoid(gd), gup_ref[...])

    kk = k * kk_ref[...]
    kk = kk * lax.rsqrt(jnp.maximum(_head_sum(kk * kk, seg_ref), KK_NORM_FLOOR ** 2))
    k2 = k * (1.0 + (a - 1.0) * ka_ref[...])

    r_out[...] = r
    k_out[...] = k2
    v_out[...] = v
    na_out[...] = -kk
    b_out[...] = kk * a
    lw_out[...] = logw
    g_out[...] = g

    u = y[:, SHIFT_WIDTH:]
    ext = jnp.concatenate([ucarry[...], u], axis=0)
    ucarry[...] = u[tm - POOL_CARRY:, :]
    t_idx = i * tm + row
    for gi, win in enumerate(POOL_WINDOWS):
        cols = slice(gi * POOL_GROUP_W, (gi + 1) * POOL_GROUP_W)
        s = ext[:, cols]
        span = 1
        while span < win:
            s = s + pltpu.roll(s, span, axis=0)
            span *= 2
        cnt = jnp.minimum(t_idx + 1, win).astype(F32)
        d = s[POOL_CARRY:, :] / cnt - u[:, cols]
        z_out[:, cols] = _dot(d, poolw_ref[gi]) * pscale_ref[:, cols]


def _even_in(h, g0, w_in, mu, w0, lora_w, a0, g_up, k_k, k_a, seg, pool_w, pool_scale):
    lp = h.shape[0]
    tm = _pick_tile(lp, ROW_TILES)
    rw = RWKV_WIDTH
    row_spec = lambda width: pl.BlockSpec((tm, width), lambda i: (i, 0))
    full = lambda arr: pl.BlockSpec(arr.shape, lambda i: (0,) * arr.ndim)
    out_sds = jax.ShapeDtypeStruct((lp, rw), F32)
    return pl.pallas_call(
        functools.partial(_even_in_kernel, tm=tm),
        grid=(lp // tm,),
        in_specs=[row_spec(D_MODEL), full(g0), full(w_in), full(mu), full(w0), full(lora_w), full(a0),
                  full(g_up), full(k_k), full(k_a), full(seg), full(pool_w), full(pool_scale)],
        out_specs=[row_spec(rw)] * 8,
        out_shape=[out_sds] * 8,
        scratch_shapes=[pltpu.VMEM((SUBLANES, SHIFT_WIDTH), F32), pltpu.VMEM((POOL_CARRY, POOL_WIDTH), F32)],
        compiler_params=pltpu.CompilerParams(dimension_semantics=("arbitrary",),
                                             vmem_limit_bytes=VMEM_LIMIT),
        name="even_in",
    )(h, g0, w_in, mu, w0, lora_w, a0, g_up, k_k, k_a, seg, pool_w, pool_scale)


def _scan_kernel(r_ref, k_ref, v_ref, na_ref, b_ref, lw_ref, o_ref, h_scr):
    @pl.when(pl.program_id(0) == 0)
    def _():
        h_scr[...] = jnp.zeros(h_scr.shape, F32)

    c = CHUNK
    n_chunks = r_ref.shape[0] // c
    row = lax.broadcasted_iota(jnp.int32, (c, c), 0)
    col = lax.broadcasted_iota(jnp.int32, (c, c), 1)
    tri = jnp.where(col <= row, 1.0, 0.0).astype(BF16)
    lane = lax.broadcasted_iota(jnp.int32, (1, LANES), 1)
    mlo = lane < HALF
    prow = lax.broadcasted_iota(jnp.int32, (LANES, LANES), 0)
    pcol = lax.broadcasted_iota(jnp.int32, (LANES, LANES), 1)
    same_head = (prow < HALF) == (pcol < HALF)
    diag = prow == pcol
    trow = lax.broadcasted_iota(jnp.int32, (c, LANES), 0)
    tcol = lax.broadcasted_iota(jnp.int32, (c, LANES), 1)
    tcol = jnp.where(tcol >= HALF, tcol - HALF, tcol)
    strict2 = tcol < trow
    incl2 = jnp.concatenate([tcol <= trow] * 2, axis=0)
    same_blk = (tcol // SOLVE_BLOCK) == (trow // SOLVE_BLOCK)
    lane2 = lax.broadcasted_iota(jnp.int32, (1, 2 * LANES), 1)
    zeros_c = jnp.zeros((c, LANES), F32)
    zeros_2c = jnp.zeros((c, 2 * LANES), F32)
    n_pairs = RWKV_WIDTH // LANES
    pair_cols = [slice(p * LANES, (p + 1) * LANES) for p in range(n_pairs)]
    d = functools.partial(jnp.dot, preferred_element_type=F32)

    prep = []
    for ci in range(n_chunks):
        rows = slice(ci * c, (ci + 1) * c)
        lw = lw_ref[rows, :]
        lw_hi = lw.astype(BF16)
        lw_r = lw - lw_hi.astype(F32)
        lw_mid = lw_r.astype(BF16)
        lw_lo = (lw_r - lw_mid.astype(F32)).astype(BF16)
        cum = d(tri, lw_hi) + (d(tri, lw_mid) + d(tri, lw_lo))
        cum_end = cum[c - 1:c, :]
        e_neg = jnp.exp(-cum)
        e_end = jnp.exp(cum_end - cum)
        b_all = b_ref[rows, :]
        k_all = k_ref[rows, :]
        prep.append(dict(r_t=r_ref[rows, :] * jnp.exp(cum), a_t=na_ref[rows, :] * jnp.exp(cum - lw),
                         b_t=b_all * e_neg, k_t=k_all * e_neg, b_h=b_all * e_end, k_h=k_all * e_end,
                         v=v_ref[rows, :], p_end=jnp.exp(cum_end)))
    units = [(ci, p) for ci in range(n_chunks) for p in range(n_pairs)]

    a_all = {}
    for ci, p in units:
        cols = pair_cols[p]
        rt, at = prep[ci]["r_t"][:, cols], prep[ci]["a_t"][:, cols]
        lhs4 = jnp.concatenate([jnp.where(mlo, at, zeros_c), jnp.where(mlo, zeros_c, at),
                                jnp.where(mlo, rt, zeros_c), jnp.where(mlo, zeros_c, rt)], axis=0)
        a_all[ci, p] = _dot_nt(lhs4, jnp.concatenate([prep[ci]["b_t"][:, cols], prep[ci]["k_t"][:, cols]], axis=0))

    heads = []
    for ci, p in units:
        cols = pair_cols[p]
        at, vp = prep[ci]["a_t"][:, cols], prep[ci]["v"][:, cols]
        at_sw = pltpu.roll(at, HALF, axis=1)
        vp_sw = pltpu.roll(vp, HALF, axis=1)
        for hh in range(2):
            nk = jnp.where(strict2, a_all[ci, p][hh * c:(hh + 1) * c], 0.0)
            av = _dot(nk, jnp.concatenate([zeros_c, vp_sw if hh == 0 else vp], axis=0))
            x0 = jnp.where(mlo, at if hh == 0 else at_sw, av)
            nk_sw = pltpu.roll(nk, HALF, axis=1)
            n_split = jnp.where(mlo, jnp.where(same_blk, 0.0, nk), jnp.where(same_blk, nk_sw, 0.0))
            heads.append(jnp.concatenate([x0, n_split], axis=1))

    for _ in range(int(math.log2(SOLVE_BLOCK))):
        nxt = []
        for y in heads:
            prod = _dot2(y[:, LANES:], jnp.concatenate([zeros_2c, y], axis=0))
            nxt.append(jnp.where(lane2 >= LANES + HALF, prod, y + prod))
        heads = nxt
    for _ in range(int(math.log2(c // SOLVE_BLOCK))):
        nxt = []
        for y in heads:
            prod = _dot2(y[:, LANES:], jnp.concatenate([y, zeros_2c], axis=0))
            nxt.append(jnp.where(lane2 < LANES, y + prod, prod))
        heads = nxt

    big = {}
    for n, (ci, p) in enumerate(units):
        cols = pair_cols[p]
        x_lo, x_hi = heads[2 * n][:, :LANES], heads[2 * n + 1][:, :LANES]
        w_p = jnp.where(mlo, x_lo, pltpu.roll(x_hi, HALF, axis=1))
        u0_p = jnp.where(mlo, pltpu.roll(x_lo, HALF, axis=1), x_hi)
        rhs = jnp.concatenate([jnp.concatenate([w_p, u0_p], axis=1),
                               jnp.concatenate([zeros_c, prep[ci]["v"][:, cols]], axis=1)], axis=0)
        a_r = jnp.where(incl2, a_all[ci, p][2 * c:], 0.0)
        bk_t = jnp.concatenate([prep[ci]["b_h"][:, cols], prep[ci]["k_h"][:, cols]], axis=0).T
        big[ci, p] = _dot(jnp.concatenate([a_r, bk_t], axis=0), rhs)

    state = [h_scr[p] for p in range(n_pairs)]
    for ci in range(n_chunks):
        rows = slice(ci * c, (ci + 1) * c)
        for p, cols in enumerate(pair_cols):
            res = big[ci, p]
            q_hat = prep[ci]["r_t"][:, cols] + jnp.where(mlo, res[:c, :LANES], res[c:2 * c, :LANES])
            o_hat = jnp.where(mlo, res[:c, LANES:], res[c:2 * c, LANES:])
            g_mat = (jnp.where(same_head, res[2 * c:, :LANES], 0.0)
                     + jnp.where(diag, prep[ci]["p_end"][:, cols], 0.0))
            j_mat = jnp.where(same_head, res[2 * c:, LANES:], 0.0)
            st = _dot2(jnp.concatenate([q_hat, g_mat], axis=0), state[p])
            o_ref[rows, cols] = st[:c] + o_hat
            state[p] = st[c:] + j_mat
    for p in range(n_pairs):
        h_scr[p] = state[p]


def _rwkv_scan(r, k2, v, na, b, logw):
    lp, rw = r.shape
    rows = CHUNK * SCAN_CHUNKS
    assert lp % rows == 0
    spec = pl.BlockSpec((rows, rw), lambda i: (i, 0))
    return pl.pallas_call(
        _scan_kernel,
        grid=(lp // rows,),
        in_specs=[spec] * 6,
        out_specs=spec,
        out_shape=jax.ShapeDtypeStruct((lp, rw), F32),
        scratch_shapes=[pltpu.VMEM((rw // LANES, LANES, LANES), F32)],
        compiler_params=pltpu.CompilerParams(dimension_semantics=("arbitrary",),
                                             vmem_limit_bytes=VMEM_LIMIT),
        name="rwkv_scan",
    )(r, k2, v, na, b, logw)


def _even_out_kernel(h_ref, o_ref, r_ref, k_ref, v_ref, g_ref, z_ref, lnw_ref, lnb_ref, rk_ref, seg_ref,
                     wout_ref, g1_ref, g2_ref, g3_ref, w1_ref, w2_ref, out_ref):
    inv_n = 1.0 / RWKV_HEAD
    o = o_ref[...]
    mean = _head_sum(o, seg_ref) * inv_n
    dev = o - mean
    var = _head_sum(dev * dev, seg_ref) * inv_n
    on = dev * lax.rsqrt(var + GN_EPS) * lnw_ref[...] + lnb_ref[...]
    bonus = _head_sum(r_ref[...] * k_ref[...] * rk_ref[...], seg_ref) * v_ref[...]
    om = (on + bonus) * g_ref[...]
    rw = RWKV_WIDTH
    m = _dot(om, wout_ref[:rw, :]) + _dot(z_ref[...], wout_ref[rw:, :])
    hm = h_ref[...] + _rms(m, g1_ref[...], RMS_EPS)
    out_ref[...] = _mlp_residual(hm, g2_ref[...], g3_ref[...], w1_ref, w2_ref)


def _weight_spec(arr):
    return pl.BlockSpec(arr.shape, lambda i: (0,) * arr.ndim, pipeline_mode=pl.Buffered(1))


def _even_out(h, o, r, k2, v, g, z, ln_w, ln_b, r_k, seg, w_out, g1, g2, g3, w1, w2):
    lp = h.shape[0]
    tm = _pick_tile(lp, ROW_TILES)
    row_spec = lambda width: pl.BlockSpec((tm, width), lambda i: (i, 0))
    full = lambda arr: pl.BlockSpec(arr.shape, lambda i: (0,) * arr.ndim)
    rw = RWKV_WIDTH
    return pl.pallas_call(
        _even_out_kernel,
        grid=(lp // tm,),
        in_specs=[row_spec(D_MODEL)] + [row_spec(rw)] * 6 +
                 [full(ln_w), full(ln_b), full(r_k), full(seg), _weight_spec(w_out), full(g1), full(g2),
                  full(g3), _weight_spec(w1), _weight_spec(w2)],
        out_specs=row_spec(D_MODEL),
        out_shape=jax.ShapeDtypeStruct((lp, D_MODEL), F32),
        compiler_params=pltpu.CompilerParams(dimension_semantics=("parallel",),
                                             vmem_limit_bytes=VMEM_LIMIT),
        name="even_out",
    )(h, o, r, k2, v, g, z, ln_w, ln_b, r_k, seg, w_out, g1, g2, g3, w1, w2)


def _odd_in_kernel(h_ref, g0_ref, w_ref, cos_ref, sin_lo_ref, sin_hi_ref, q_out, k_out, v_out, kmax_out):
    @pl.when(pl.program_id(0) == 0)
    def _():
        kmax_out[...] = jnp.zeros(kmax_out.shape, F32)

    hn = _rms(h_ref[...], g0_ref[...], RMS_EPS).astype(BF16)
    y = jnp.dot(hn, w_ref[...], preferred_element_type=F32)
    cos = cos_ref[...]
    sin_lo = sin_lo_ref[...]
    sin_hi = sin_hi_ref[...]
    half = DIFF_HEAD // 2

    def rope(t):
        return (t * cos + pltpu.roll(t, LANES - half, axis=1) * sin_lo + pltpu.roll(t, half, axis=1) * sin_hi)

    scale = DIFF_HEAD ** -0.5 * math.log2(math.e)
    tm = y.shape[0]
    lane = lax.broadcasted_iota(jnp.int32, (tm, LANES), 1)
    minus_one = jnp.where(lane == 0, -1.0, 0.0).astype(BF16)
    ones = jnp.ones((tm, LANES), BF16)
    for j in range(D_MODEL // LANES):
        cols = slice(j * LANES, (j + 1) * LANES)
        kcols = slice(D_MODEL + j * LANES, D_MODEL + (j + 1) * LANES)
        vcols = slice(2 * D_MODEL + j * LANES, 2 * D_MODEL + (j + 1) * LANES)
        wide = slice(2 * j * LANES, (2 * j + 1) * LANES)
        wide_hi = slice((2 * j + 1) * LANES, (2 * j + 2) * LANES)
        q_out[:, cols] = (rope(y[:, cols]) * scale).astype(BF16)
        kb = rope(y[:, kcols]).astype(BF16)
        k_out[:, wide] = kb
        k_out[:, wide_hi] = minus_one
        v_out[:, wide] = y[:, vcols].astype(BF16)
        v_out[:, wide_hi] = ones
        kf = kb.astype(F32)
        knorm = jnp.sqrt(jnp.max(jnp.sum(kf * kf, axis=-1, keepdims=True), axis=0, keepdims=True))
        kmax_out[:, cols] = jnp.maximum(kmax_out[:, cols], jnp.broadcast_to(knorm, (SUBLANES, LANES)))


def _odd_in(h, g0, w, cos, sin_lo, sin_hi):
    lp, lpk = h.shape[0], cos.shape[0]
    tm = ROW_ALIGN
    last = lp // tm - 1
    row_spec = lambda width: pl.BlockSpec((tm, width), lambda i: (i, 0))
    full = lambda arr: pl.BlockSpec(arr.shape, lambda i: (0,) * arr.ndim)
    wide = jax.ShapeDtypeStruct((lpk, 2 * D_MODEL), BF16)
    return pl.pallas_call(
        _odd_in_kernel,
        grid=(lpk // tm,),
        in_specs=[pl.BlockSpec((tm, D_MODEL), lambda i: (jnp.minimum(i, last), 0)), full(g0),
                  _weight_spec(w), row_spec(LANES), row_spec(LANES), row_spec(LANES)],
        out_specs=[row_spec(D_MODEL), row_spec(2 * D_MODEL), row_spec(2 * D_MODEL),
                   pl.BlockSpec((SUBLANES, D_MODEL), lambda i: (0, 0))],
        out_shape=[jax.ShapeDtypeStruct((lpk, D_MODEL), BF16), wide, wide,
                   jax.ShapeDtypeStruct((SUBLANES, D_MODEL), F32)],
        compiler_params=pltpu.CompilerParams(dimension_semantics=("arbitrary",),
                                             vmem_limit_bytes=VMEM_LIMIT),
        name="odd_in",
    )(h, g0, w, cos, sin_lo, sin_hi)


def _attn_kernel(lam_ref, sw_ref, kmax_ref, q_ref, k_ref, v_ref, o_ref, q2_scr, p_scr, acc_scr,
                 *, lam_init, tq, tk, nh):
    i = pl.program_id(1)
    lane = lax.broadcasted_iota(jnp.int32, (1, LANES), 1)
    mlo = lane < HALF
    nt = (((1,), (1,)), ((), ()))
    n_blocks = (i * tq) // tk + 1
    wl = 2 * LANES

    def diagonal_mask(off, width):
        qrow = lax.broadcasted_iota(jnp.int32, (2 * tq, width), 0)
        qrow = jnp.where(qrow >= tq, qrow - tq, qrow)
        kcol = lax.broadcasted_iota(jnp.int32, (2 * tq, width), 1)
        return kcol - qrow <= i * tq - off

    for hd in range(nh):
        q = q_ref[:, hd * LANES:(hd + 1) * LANES]
        zq = jnp.zeros_like(q)
        q2 = jnp.concatenate([jnp.where(mlo, q, zq), jnp.where(mlo, zq, q)], axis=0)
        q2f = q2.astype(F32)
        bound = (jnp.sqrt(jnp.sum(q2f * q2f, axis=-1, keepdims=True))
                 * kmax_ref[0:1, hd * LANES:hd * LANES + 1] * BOUND_SLACK)
        q2_scr[hd] = jnp.concatenate([q2, jnp.where(lane == 0, bound, 0.0).astype(BF16)], axis=1)

    def weights_into(hd, slot, t, width):
        off = pl.multiple_of(t * tk, tk)
        w = tk if width is None else width
        s = lax.dot_general(q2_scr[hd], k_ref[pl.ds(off, w), hd * wl:(hd + 1) * wl], nt,
                            preferred_element_type=F32)
        if width is not None:
            s = jnp.where(diagonal_mask(off, w), s, NEG_BIG)
        p_scr[hd, slot, :, :w] = jnp.exp2(s).astype(BF16)

    def add_weighted_values(hd, slot, t, width):
        off = pl.multiple_of(t * tk, tk)
        w = tk if width is None else width
        acc_scr[hd] += jnp.dot(p_scr[hd, slot, :, :w], v_ref[pl.ds(off, w), hd * wl:(hd + 1) * wl],
                               preferred_element_type=F32)

    def tick(t, slot, width):
        for hd in range(nh):
            weights_into(hd, slot, t, width)
            add_weighted_values(hd, 1 - slot, t - 1, None)

    def drain(slot, t, width):
        for hd in range(nh):
            add_weighted_values(hd, slot, t, width)

    acc_scr[...] = jnp.zeros(acc_scr.shape, F32)

    @pl.when(n_blocks > 1)
    def _():
        for hd in range(nh):
            weights_into(hd, 0, 0, None)

    def pair(u, carry):
        tick(2 * u + 1, 1, None)
        tick(2 * u + 2, 0, None)
        return carry

    n_pairs = jnp.maximum(n_blocks - 2, 0) // 2
    lax.fori_loop(0, n_pairs, pair, 0)
    t1 = 2 * n_pairs + 1

    sub = tk // tq
    for r in range(sub):
        width = (r + 1) * tq
        here = i % sub == r

        @pl.when(jnp.logical_and(n_blocks == 1, here))
        def _():
            for hd in range(nh):
                weights_into(hd, 0, 0, width)
            drain(0, 0, width)

        @pl.when(jnp.logical_and(jnp.logical_and(n_blocks > 1, n_blocks % 2 == 0), here))
        def _():
            tick(t1, 1, width)
            drain(1, t1, width)

        @pl.when(jnp.logical_and(jnp.logical_and(n_blocks > 1, n_blocks % 2 == 1), here))
        def _():
            tick(t1, 1, None)
            tick(t1 + 1, 0, width)
            drain(0, t1 + 1, width)

    lv = lam_ref[...]
    lam = (jnp.exp(jnp.sum(lv[0:1] * lv[1:2], axis=-1, keepdims=True))
           - jnp.exp(jnp.sum(lv[2:3] * lv[3:4], axis=-1, keepdims=True)) + lam_init)

    def write_out(hd, acc):
        o = acc[:, :LANES] / acc[:, LANES:]
        o = o[:tq] - lam * o[tq:]
        o = _rms(o, sw_ref[...], SUBLN_EPS) * (1.0 - lam_init)
        o_ref[:, hd * LANES:(hd + 1) * LANES] = o.astype(BF16)

    for hd in range(nh):
        write_out(hd, acc_scr[hd])

    smallest = jnp.min(acc_scr[:, :, LANES:LANES + 1])
    @pl.when(jnp.logical_not(smallest >= MIN_ROW_SUM))
    def _():
        for hd in range(nh):
            q2 = q2_scr[hd, :, :LANES]

            def block(t, carry, diagonal):
                m, acc = carry
                off = pl.multiple_of(t * tk, tk)
                s = lax.dot_general(q2, k_ref[pl.ds(off, tk), hd * wl:hd * wl + LANES], nt,
                                    preferred_element_type=F32)
                if diagonal:
                    s = jnp.where(diagonal_mask(off, tk), s, NEG_BIG)
                m_new = jnp.maximum(m, jnp.max(s, axis=-1, keepdims=True))
                p = jnp.exp2(s - m_new).astype(BF16)
                pv = jnp.dot(p, v_ref[pl.ds(off, tk), hd * wl:(hd + 1) * wl], preferred_element_type=F32)
                return m_new, jnp.exp2(m - m_new) * acc + pv

            init = (jnp.full((2 * tq, 1), NEG_BIG, F32), jnp.zeros((2 * tq, wl), F32))
            carry = lax.fori_loop(0, n_blocks - 1, lambda t, c: block(t, c, False), init)
            write_out(hd, block(n_blocks - 1, carry, True)[1])


def _diff_attn(q, k, v, kmax, lam_vecs, subln_w, lam_init, lp):
    tq, tk, nh = ATT_Q_BLOCK, ATT_K_BLOCK, ATT_HEADS
    lpk = k.shape[0]
    assert lp % tq == 0 and tk % tq == 0 and lpk % tk == 0 and lpk >= lp and DIFF_HEADS % nh == 0
    blk = pl.BlockSpec((tq, nh * LANES), lambda h, i: (i, h))
    resident = pl.BlockSpec((lpk, 2 * nh * LANES), lambda h, i: (0, h), pipeline_mode=pl.Buffered(1))
    full = lambda arr: pl.BlockSpec(arr.shape, lambda h, i: (0,) * arr.ndim)
    return pl.pallas_call(
        functools.partial(_attn_kernel, lam_init=lam_init, tq=tq, tk=tk, nh=nh),
        grid=(DIFF_HEADS // nh, lp // tq),
        in_specs=[full(lam_vecs), full(subln_w), pl.BlockSpec((SUBLANES, nh * LANES), lambda h, i: (0, h)),
                  blk, resident, resident],
        out_specs=blk,
        out_shape=jax.ShapeDtypeStruct((lp, D_MODEL), BF16),
        scratch_shapes=[pltpu.VMEM((nh, 2 * tq, 2 * LANES), BF16), pltpu.VMEM((nh, 2, 2 * tq, tk), BF16),
                        pltpu.VMEM((nh, 2 * tq, 2 * LANES), F32)],
        compiler_params=pltpu.CompilerParams(dimension_semantics=("parallel", "arbitrary"),
                                             vmem_limit_bytes=VMEM_LIMIT),
        name="diff_attn",
    )(lam_vecs, subln_w, kmax, q, k, v)


def _odd_out_kernel(h_ref, o_ref, wout_ref, g1_ref, g2_ref, g3_ref, w1_ref, w2_ref, out_ref):
    m = jnp.dot(o_ref[...], wout_ref[...], preferred_element_type=F32)
    hm = h_ref[...] + _rms(m, g1_ref[...], RMS_EPS)
    out_ref[...] = _mlp_residual(hm, g2_ref[...], g3_ref[...], w1_ref, w2_ref)


def _odd_out(h, o, w_out, g1, g2, g3, w1, w2):
    lp = h.shape[0]
    tm = _pick_tile(lp, ROW_TILES)
    row_spec = lambda width: pl.BlockSpec((tm, width), lambda i: (i, 0))
    full = lambda arr: pl.BlockSpec(arr.shape, lambda i: (0,) * arr.ndim)
    return pl.pallas_call(
        _odd_out_kernel,
        grid=(lp // tm,),
        in_specs=[row_spec(D_MODEL), row_spec(D_MODEL), _weight_spec(w_out), full(g1), full(g2), full(g3),
                  _weight_spec(w1), _weight_spec(w2)],
        out_specs=row_spec(D_MODEL),
        out_shape=jax.ShapeDtypeStruct((lp, D_MODEL), F32),
        compiler_params=pltpu.CompilerParams(dimension_semantics=("parallel",),
                                             vmem_limit_bytes=VMEM_LIMIT),
        name="odd_out",
    )(h, o, w_out, g1, g2, g3, w1, w2)


def _forward(x, meta, norm_g, mlp_w1, mlp_w2, ev, od):
    seq = x.shape[0]
    length = N_META + seq
    lp = -(-length // ROW_ALIGN) * ROW_ALIGN
    h = jnp.concatenate([meta.astype(x.dtype), x, jnp.zeros((lp - length, D_MODEL), x.dtype)], axis=0)

    lpk = -(-lp // ATT_K_BLOCK) * ATT_K_BLOCK
    pos = jnp.arange(lpk, dtype=F32)
    inv = ROPE_THETA ** (-jnp.arange(0, DIFF_HEAD, 2, dtype=F32) / DIFF_HEAD)
    ang = pos[:, None] * inv[None, :]
    ang = jnp.concatenate([ang, ang, ang, ang], axis=-1)
    cos, sin = jnp.cos(ang), jnp.sin(ang)
    first_half = (jnp.arange(LANES) % DIFF_HEAD) < DIFF_HEAD // 2
    sin_lo = jnp.where(first_half, -sin, 0.0)
    sin_hi = jnp.where(first_half, 0.0, sin)

    head_id = jnp.arange(RWKV_WIDTH) // RWKV_HEAD
    seg = (head_id[:, None] == head_id[None, :]).astype(BF16)
    row2 = lambda t: t.reshape(1, -1)

    depth = norm_g.shape[0]
    for i in range(depth):
        g = norm_g[i]
        g0, g1, g2, g3 = (row2(g[n]) for n in range(4))
        w1 = mlp_w1[i].astype(BF16)
        w2 = mlp_w2[i].astype(BF16)
        j = i // 2
        if i % 2 == 0:
            (w_in, mu, w0, w_up, a0, a_up, g_up, k_k, k_a, r_k, ln_w, ln_b, pool_w, pool_scale,
             w_out) = (t[j] for t in ev)
            zeros = jnp.zeros((DECAY_RANK, RWKV_WIDTH), F32)
            lora_w = jnp.concatenate([jnp.concatenate([w_up, zeros], axis=1),
                                      jnp.concatenate([zeros, a_up], axis=1)], axis=0).astype(BF16)
            r, k2, v, na, b, logw, gate, z = _even_in(
                h, g0, w_in.astype(BF16), row2(mu), row2(w0), lora_w, row2(a0), g_up.astype(BF16),
                row2(k_k), row2(k_a), seg, pool_w.astype(BF16), row2(pool_scale))
            o = _rwkv_scan(r, k2, v, na, b, logw)
            h = _even_out(h, o, r, k2, v, gate, z, row2(ln_w), row2(ln_b), row2(r_k), seg,
                          w_out.astype(BF16), g1, g2, g3, w1, w2)
        else:
            w_in, lam_vecs, subln_w, w_out = (t[j] for t in od)
            q, k, v, kmax = _odd_in(h, g0, w_in.astype(BF16), cos, sin_lo, sin_hi)
            lam_init = 0.8 - 0.6 * math.exp(-0.3 * i)
            o = _diff_attn(q, k, v, kmax, lam_vecs, row2(subln_w), lam_init, lp)
            h = _odd_out(h, o, w_out.astype(BF16), g1, g2, g3, w1, w2)
    return h[N_META:length]


def kernel(x, meta, norm_g, mlp_w1, mlp_w2, ev_w_in, ev_mu, ev_w0, ev_w_up, ev_a0, ev_a_up, ev_g_up, ev_k_k,
           ev_k_a, ev_r_k, ev_ln_w, ev_ln_b, ev_pool_w, ev_pool_scale, ev_w_out, od_w_in, od_lambda,
           od_subln_w, od_w_out):
    ev = (ev_w_in, ev_mu, ev_w0, ev_w_up, ev_a0, ev_a_up, ev_g_up, ev_k_k, ev_k_a, ev_r_k, ev_ln_w, ev_ln_b,
          ev_pool_w, ev_pool_scale, ev_w_out)
    od = (od_w_in, od_lambda, od_subln_w, od_w_out)
    outs = [_forward(x[bi], meta, norm_g, mlp_w1, mlp_w2, ev, od) for bi in range(x.shape[0])]
    return jnp.stack(outs, axis=0)
```

```python
import functools
import math

import jax
import jax.numpy as jnp
from jax import lax
from jax.experimental import pallas as pl
from jax.experimental.pallas import tpu as pltpu

F32, BF16 = jnp.float32, jnp.bfloat16

D_MODEL = 1024
N_META = 16
RMS_EPS = 1e-6
D_FF = 4 * D_MODEL
RWKV_HEAD = 64
RWKV_WIDTH = D_MODEL // 2
DECAY_RANK = 64
ICLR_RANK = 64
GATE_RANK = 128
GN_EPS = RWKV_HEAD * 1e-5
POOL_WIDTH = D_MODEL - RWKV_WIDTH
POOL_WINDOWS = (2, 4, 8, 16)
POOL_GROUP_W = POOL_WIDTH // len(POOL_WINDOWS)
POOL_CARRY = 16
SHIFT_WIDTH = 3 * RWKV_WIDTH + DECAY_RANK + ICLR_RANK + GATE_RANK
EVEN_IN = SHIFT_WIDTH + POOL_WIDTH
DIFF_HEADS = 8
DIFF_HEAD = D_MODEL // (2 * DIFF_HEADS)
SUBLN_EPS = 1e-5
ROPE_THETA = 10000.0

LANES = 128
SUBLANES = 8
HALF = LANES // 2
ROW_ALIGN = 256
ROW_TILES = (640, ROW_ALIGN)
KK_NORM_FLOOR = 1e-12
CHUNK = 64
SOLVE_BLOCK = 8
SCAN_CHUNKS = 4
ATT_Q_BLOCK = 256
ATT_K_BLOCK = 1024
ATT_HEADS = 2
FF_CHUNK = 1024
NEG_BIG = -1e30
BOUND_SLACK = 1.02
MIN_ROW_SUM = 2.0 ** -100
VMEM_LIMIT = 56 * 1024 * 1024


def _pick_tile(n, candidates):
    for c in candidates:
        if n % c == 0:
            return c
    raise ValueError(f"no tile in {candidates} divides {n}")


def _rms(t, g, eps):
    return t * lax.rsqrt(jnp.mean(t * t, axis=-1, keepdims=True) + eps) * g


def _split2(x):
    hi = x.astype(BF16)
    lo = (x - hi.astype(F32)).astype(BF16)
    return hi, lo


def _dot(a, b):
    return jnp.dot(a.astype(BF16), b.astype(BF16), preferred_element_type=F32)


def _dot2(a, b):
    ab = a.astype(BF16)
    bh, bl = _split2(b)
    d = functools.partial(jnp.dot, preferred_element_type=F32)
    return d(ab, bh) + d(ab, bl)


def _dot_nt(a, b):
    return lax.dot_general(a.astype(BF16), b.astype(BF16), (((1,), (1,)), ((), ())),
                           preferred_element_type=F32)


def _head_sum(x, seg_ref):
    xh, xl = _split2(x)
    d = functools.partial(jnp.dot, preferred_element_type=F32)
    return d(xh, seg_ref[...]) + d(xl, seg_ref[...])


def _sigmoid(x):
    return 1.0 / (1.0 + jnp.exp(-x))


def _softplus(x):
    return jnp.maximum(x, 0.0) + jnp.log(1.0 + jnp.exp(-jnp.abs(x)))


def _mlp_residual(hm, g2, g3, w1_ref, w2_ref):
    n = _rms(hm, g2, RMS_EPS).astype(BF16)
    acc = jnp.zeros(hm.shape, F32)
    for c in range(D_FF // FF_CHUNK):
        cols = slice(c * FF_CHUNK, (c + 1) * FF_CHUNK)
        a = jnp.dot(n, w1_ref[:, cols], preferred_element_type=F32)
        a = jnp.square(jnp.maximum(a, 0.0)).astype(BF16)
        acc = acc + jnp.dot(a, w2_ref[cols, :], preferred_element_type=F32)
    return hm + _rms(acc, g3, RMS_EPS)


def _even_in_kernel(h_ref, g0_ref, win_ref, mu_ref, w0_ref, lora_ref, a0_ref, gup_ref,
                    kk_ref, ka_ref, seg_ref, poolw_ref, pscale_ref,
                    r_out, k_out, v_out, na_out, b_out, lw_out, g_out, z_out,
                    ycarry, ucarry, *, tm):
    i = pl.program_id(0)

    @pl.when(i == 0)
    def _():
        ycarry[...] = jnp.zeros(ycarry.shape, F32)
        ucarry[...] = jnp.zeros(ucarry.shape, F32)

    hn = _rms(h_ref[...], g0_ref[...], RMS_EPS).astype(BF16)
    y = jnp.dot(hn, win_ref[...], preferred_element_type=F32)

    ysh = y[:, :SHIFT_WIDTH]
    row = lax.broadcasted_iota(jnp.int32, (tm, 1), 0)
    prev = jnp.where(row == 0, ycarry[SUBLANES - 1:SUBLANES, :], pltpu.roll(ysh, 1, axis=0))
    ycarry[...] = ysh[tm - SUBLANES:, :]
    ys = ysh + (prev - ysh) * mu_ref[...]

    rw = RWKV_WIDTH
    r = ys[:, 0:rw]
    k = ys[:, rw:2 * rw]
    v = ys[:, 2 * rw:3 * rw]
    wa = ys[:, 3 * rw:3 * rw + LANES]
    gd = ys[:, 3 * rw + LANES:SHIFT_WIDTH]

    lane = lax.broadcasted_iota(jnp.int32, (1, LANES), 1)
    lora_in = jnp.where(lane < DECAY_RANK, jnp.tanh(wa), wa)
    lora = _dot(lora_in, lora_ref[...])
    wlog = -_softplus(-(w0_ref[...] + lora[:, :rw])) - 0.5
    logw = -jnp.exp(wlog)
    a = _sigmoid(a0_ref[...] + lora[:, rw:])
    g = _dot(_sigmoid(gd), gup_ref[...])

    kk = k * kk_ref[...]
    kk = kk * lax.rsqrt(jnp.maximum(_head_sum(kk * kk, seg_ref), KK_NORM_FLOOR ** 2))
    k2 = k * (1.0 + (a - 1.0) * ka_ref[...])

    r_out[...] = r
    k_out[...] = k2
    v_out[...] = v
    na_out[...] = -kk
    b_out[...] = kk * a
    lw_out[...] = logw
    g_out[...] = g

    u = y[:, SHIFT_WIDTH:]
    ext = jnp.concatenate([ucarry[...], u], axis=0)
    ucarry[...] = u[tm - POOL_CARRY:, :]
    t_idx = i * tm + row
    for gi, win in enumerate(POOL_WINDOWS):
        cols = slice(gi * POOL_GROUP_W, (gi + 1) * POOL_GROUP_W)
        s = ext[:, cols]
        span = 1
        while span < win:
            s = s + pltpu.roll(s, span, axis=0)
            span *= 2
        cnt = jnp.minimum(t_idx + 1, win).astype(F32)
        d = s[POOL_CARRY:, :] / cnt - u[:, cols]
        z_out[:, cols] = _dot(d, poolw_ref[gi]) * pscale_ref[:, cols]


def _even_in(h, g0, w_in, mu, w0, lora_w, a0, g_up, k_k, k_a, seg, pool_w, pool_scale):
    lp = h.shape[0]
    tm = _pick_tile(lp, ROW_TILES)
    rw = RWKV_WIDTH
    row_spec = lambda width: pl.BlockSpec((tm, width), lambda i: (i, 0))
    full = lambda arr: pl.BlockSpec(arr.shape, lambda i: (0,) * arr.ndim)
    out_sds = jax.ShapeDtypeStruct((lp, rw), F32)
    return pl.pallas_call(
        functools.partial(_even_in_kernel, tm=tm),
        grid=(lp // tm,),
        in_specs=[row_spec(D_MODEL), full(g0), full(w_in), full(mu), full(w0), full(lora_w), full(a0),
                  full(g_up), full(k_k), full(k_a), full(seg), full(pool_w), full(pool_scale)],
        out_specs=[row_spec(rw)] * 8,
        out_shape=[out_sds] * 8,
        scratch_shapes=[pltpu.VMEM((SUBLANES, SHIFT_WIDTH), F32), pltpu.VMEM((POOL_CARRY, POOL_WIDTH), F32)],
        compiler_params=pltpu.CompilerParams(dimension_semantics=("arbitrary",),
                                             vmem_limit_bytes=VMEM_LIMIT),
        name="even_in",
    )(h, g0, w_in, mu, w0, lora_w, a0, g_up, k_k, k_a, seg, pool_w, pool_scale)


def _scan_kernel(r_ref, k_ref, v_ref, na_ref, b_ref, lw_ref, o_ref, h_scr):
    @pl.when(pl.program_id(0) == 0)
    def _():
        h_scr[...] = jnp.zeros(h_scr.shape, F32)

    c = CHUNK
    n_chunks = r_ref.shape[0] // c
    row = lax.broadcasted_iota(jnp.int32, (c, c), 0)
    col = lax.broadcasted_iota(jnp.int32, (c, c), 1)
    tri = jnp.where(col <= row, 1.0, 0.0).astype(BF16)
    lane = lax.broadcasted_iota(jnp.int32, (1, LANES), 1)
    mlo = lane < HALF
    prow = lax.broadcasted_iota(jnp.int32, (LANES, LANES), 0)
    pcol = lax.broadcasted_iota(jnp.int32, (LANES, LANES), 1)
    same_head = (prow < HALF) == (pcol < HALF)
    diag = prow == pcol
    trow = lax.broadcasted_iota(jnp.int32, (c, LANES), 0)
    tcol = lax.broadcasted_iota(jnp.int32, (c, LANES), 1)
    tcol = jnp.where(tcol >= HALF, tcol - HALF, tcol)
    strict2 = tcol < trow
    incl2 = jnp.concatenate([tcol <= trow] * 2, axis=0)
    same_blk = (tcol // SOLVE_BLOCK) == (trow // SOLVE_BLOCK)
    lane2 = lax.broadcasted_iota(jnp.int32, (1, 2 * LANES), 1)
    zeros_c = jnp.zeros((c, LANES), F32)
    zeros_2c = jnp.zeros((c, 2 * LANES), F32)
    n_pairs = RWKV_WIDTH // LANES
    pair_cols = [slice(p * LANES, (p + 1) * LANES) for p in range(n_pairs)]
    d = functools.partial(jnp.dot, preferred_element_type=F32)

    prep = []
    for ci in range(n_chunks):
        rows = slice(ci * c, (ci + 1) * c)
        lw = lw_ref[rows, :]
        lw_hi = lw.astype(BF16)
        lw_r = lw - lw_hi.astype(F32)
        lw_mid = lw_r.astype(BF16)
        lw_lo = (lw_r - lw_mid.astype(F32)).astype(BF16)
        cum = d(tri, lw_hi) + (d(tri, lw_mid) + d(tri, lw_lo))
        cum_end = cum[c - 1:c, :]
        e_neg = jnp.exp(-cum)
        e_end = jnp.exp(cum_end - cum)
        b_all = b_ref[rows, :]
        k_all = k_ref[rows, :]
        prep.append(dict(r_t=r_ref[rows, :] * jnp.exp(cum), a_t=na_ref[rows, :] * jnp.exp(cum - lw),
                         b_t=b_all * e_neg, k_t=k_all * e_neg, b_h=b_all * e_end, k_h=k_all * e_end,
                         v=v_ref[rows, :], p_end=jnp.exp(cum_end)))
    units = [(ci, p) for ci in range(n_chunks) for p in range(n_pairs)]

    a_all = {}
    for ci, p in units:
        cols = pair_cols[p]
        rt, at = prep[ci]["r_t"][:, cols], prep[ci]["a_t"][:, cols]
        lhs4 = jnp.concatenate([jnp.where(mlo, at, zeros_c), jnp.where(mlo, zeros_c, at),
                                jnp.where(mlo, rt, zeros_c), jnp.where(mlo, zeros_c, rt)], axis=0)
        a_all[ci, p] = _dot_nt(lhs4, jnp.concatenate([prep[ci]["b_t"][:, cols], prep[ci]["k_t"][:, cols]], axis=0))

    heads = []
    for ci, p in units:
        cols = pair_cols[p]
        at, vp = prep[ci]["a_t"][:, cols], prep[ci]["v"][:, cols]
        at_sw = pltpu.roll(at, HALF, axis=1)
        vp_sw = pltpu.roll(vp, HALF, axis=1)
        for hh in range(2):
            nk = jnp.where(strict2, a_all[ci, p][hh * c:(hh + 1) * c], 0.0)
            av = _dot(nk, jnp.concatenate([zeros_c, vp_sw if hh == 0 else vp], axis=0))
            x0 = jnp.where(mlo, at if hh == 0 else at_sw, av)
            nk_sw = pltpu.roll(nk, HALF, axis=1)
            n_split = jnp.where(mlo, jnp.where(same_blk, 0.0, nk), jnp.where(same_blk, nk_sw, 0.0))
            heads.append(jnp.concatenate([x0, n_split], axis=1))

    for _ in range(int(math.log2(SOLVE_BLOCK))):
        nxt = []
        for y in heads:
            prod = _dot(y[:, LANES:], jnp.concatenate([zeros_2c, y], axis=0))
            nxt.append(jnp.where(lane2 >= LANES + HALF, prod, y + prod))
        heads = nxt
    for _ in range(int(math.log2(c // SOLVE_BLOCK))):
        nxt = []
        for y in heads:
            prod = _dot(y[:, LANES:], jnp.concatenate([y, zeros_2c], axis=0))
            nxt.append(jnp.where(lane2 < LANES, y + prod, prod))
        heads = nxt

    big = {}
    for n, (ci, p) in enumerate(units):
        cols = pair_cols[p]
        x_lo, x_hi = heads[2 * n][:, :LANES], heads[2 * n + 1][:, :LANES]
        w_p = jnp.where(mlo, x_lo, pltpu.roll(x_hi, HALF, axis=1))
        u0_p = jnp.where(mlo, pltpu.roll(x_lo, HALF, axis=1), x_hi)
        rhs = jnp.concatenate([jnp.concatenate([w_p, u0_p], axis=1),
                               jnp.concatenate([zeros_c, prep[ci]["v"][:, cols]], axis=1)], axis=0)
        a_r = jnp.where(incl2, a_all[ci, p][2 * c:], 0.0)
        bk_t = jnp.concatenate([prep[ci]["b_h"][:, cols], prep[ci]["k_h"][:, cols]], axis=0).T
        big[ci, p] = _dot(jnp.concatenate([a_r, bk_t], axis=0), rhs)

    state = [h_scr[p] for p in range(n_pairs)]
    for ci in range(n_chunks):
        rows = slice(ci * c, (ci + 1) * c)
        for p, cols in enumerate(pair_cols):
            res = big[ci, p]
            q_hat = prep[ci]["r_t"][:, cols] + jnp.where(mlo, res[:c, :LANES], res[c:2 * c, :LANES])
            o_hat = jnp.where(mlo, res[:c, LANES:], res[c:2 * c, LANES:])
            g_mat = (jnp.where(same_head, res[2 * c:, :LANES], 0.0)
                     + jnp.where(diag, prep[ci]["p_end"][:, cols], 0.0))
            j_mat = jnp.where(same_head, res[2 * c:, LANES:], 0.0)
            st = _dot2(jnp.concatenate([q_hat, g_mat], axis=0), state[p])
            o_ref[rows, cols] = st[:c] + o_hat
            state[p] = st[c:] + j_mat
    for p in range(n_pairs):
        h_scr[p] = state[p]


def _rwkv_scan(r, k2, v, na, b, logw):
    lp, rw = r.shape
    rows = CHUNK * SCAN_CHUNKS
    assert lp % rows == 0
    spec = pl.BlockSpec((rows, rw), lambda i: (i, 0))
    return pl.pallas_call(
        _scan_kernel,
        grid=(lp // rows,),
        in_specs=[spec] * 6,
        out_specs=spec,
        out_shape=jax.ShapeDtypeStruct((lp, rw), F32),
        scratch_shapes=[pltpu.VMEM((rw // LANES, LANES, LANES), F32)],
        compiler_params=pltpu.CompilerParams(dimension_semantics=("arbitrary",),
                                             vmem_limit_bytes=VMEM_LIMIT),
        name="rwkv_scan",
    )(r, k2, v, na, b, logw)


def _even_out_kernel(h_ref, o_ref, r_ref, k_ref, v_ref, g_ref, z_ref, lnw_ref, lnb_ref, rk_ref, seg_ref,
                     wout_ref, g1_ref, g2_ref, g3_ref, w1_ref, w2_ref, out_ref):
    inv_n = 1.0 / RWKV_HEAD
    o = o_ref[...]
    mean = _head_sum(o, seg_ref) * inv_n
    dev = o - mean
    var = _head_sum(dev * dev, seg_ref) * inv_n
    on = dev * lax.rsqrt(var + GN_EPS) * lnw_ref[...] + lnb_ref[...]
    bonus = _head_sum(r_ref[...] * k_ref[...] * rk_ref[...], seg_ref) * v_ref[...]
    om = (on + bonus) * g_ref[...]
    rw = RWKV_WIDTH
    m = _dot(om, wout_ref[:rw, :]) + _dot(z_ref[...], wout_ref[rw:, :])
    hm = h_ref[...] + _rms(m, g1_ref[...], RMS_EPS)
    out_ref[...] = _mlp_residual(hm, g2_ref[...], g3_ref[...], w1_ref, w2_ref)


def _weight_spec(arr):
    return pl.BlockSpec(arr.shape, lambda i: (0,) * arr.ndim, pipeline_mode=pl.Buffered(1))


def _even_out(h, o, r, k2, v, g, z, ln_w, ln_b, r_k, seg, w_out, g1, g2, g3, w1, w2):
    lp = h.shape[0]
    tm = _pick_tile(lp, ROW_TILES)
    row_spec = lambda width: pl.BlockSpec((tm, width), lambda i: (i, 0))
    full = lambda arr: pl.BlockSpec(arr.shape, lambda i: (0,) * arr.ndim)
    rw = RWKV_WIDTH
    return pl.pallas_call(
        _even_out_kernel,
        grid=(lp // tm,),
        in_specs=[row_spec(D_MODEL)] + [row_spec(rw)] * 6 +
                 [full(ln_w), full(ln_b), full(r_k), full(seg), _weight_spec(w_out), full(g1), full(g2),
                  full(g3), _weight_spec(w1), _weight_spec(w2)],
        out_specs=row_spec(D_MODEL),
        out_shape=jax.ShapeDtypeStruct((lp, D_MODEL), F32),
        compiler_params=pltpu.CompilerParams(dimension_semantics=("parallel",),
                                             vmem_limit_bytes=VMEM_LIMIT),
        name="even_out",
    )(h, o, r, k2, v, g, z, ln_w, ln_b, r_k, seg, w_out, g1, g2, g3, w1, w2)


def _odd_in_kernel(h_ref, g0_ref, w_ref, cos_ref, sin_lo_ref, sin_hi_ref, q_out, k_out, v_out, kmax_out):
    @pl.when(pl.program_id(0) == 0)
    def _():
        kmax_out[...] = jnp.zeros(kmax_out.shape, F32)

    hn = _rms(h_ref[...], g0_ref[...], RMS_EPS).astype(BF16)
    y = jnp.dot(hn, w_ref[...], preferred_element_type=F32)
    cos = cos_ref[...]
    sin_lo = sin_lo_ref[...]
    sin_hi = sin_hi_ref[...]
    half = DIFF_HEAD // 2

    def rope(t):
        return (t * cos + pltpu.roll(t, LANES - half, axis=1) * sin_lo + pltpu.roll(t, half, axis=1) * sin_hi)

    scale = DIFF_HEAD ** -0.5 * math.log2(math.e)
    tm = y.shape[0]
    lane = lax.broadcasted_iota(jnp.int32, (tm, LANES), 1)
    minus_one = jnp.where(lane == 0, -1.0, 0.0).astype(BF16)
    ones = jnp.ones((tm, LANES), BF16)
    for j in range(D_MODEL // LANES):
        cols = slice(j * LANES, (j + 1) * LANES)
        kcols = slice(D_MODEL + j * LANES, D_MODEL + (j + 1) * LANES)
        vcols = slice(2 * D_MODEL + j * LANES, 2 * D_MODEL + (j + 1) * LANES)
        wide = slice(2 * j * LANES, (2 * j + 1) * LANES)
        wide_hi = slice((2 * j + 1) * LANES, (2 * j + 2) * LANES)
        q_out[:, cols] = (rope(y[:, cols]) * scale).astype(BF16)
        kb = rope(y[:, kcols]).astype(BF16)
        k_out[:, wide] = kb
        k_out[:, wide_hi] = minus_one
        v_out[:, wide] = y[:, vcols].astype(BF16)
        v_out[:, wide_hi] = ones
        kf = kb.astype(F32)
        knorm = jnp.sqrt(jnp.max(jnp.sum(kf * kf, axis=-1, keepdims=True), axis=0, keepdims=True))
        kmax_out[:, cols] = jnp.maximum(kmax_out[:, cols], jnp.broadcast_to(knorm, (SUBLANES, LANES)))


def _odd_in(h, g0, w, cos, sin_lo, sin_hi):
    lp, lpk = h.shape[0], cos.shape[0]
    tm = ROW_ALIGN
    last = lp // tm - 1
    row_spec = lambda width: pl.BlockSpec((tm, width), lambda i: (i, 0))
    full = lambda arr: pl.BlockSpec(arr.shape, lambda i: (0,) * arr.ndim)
    wide = jax.ShapeDtypeStruct((lpk, 2 * D_MODEL), BF16)
    return pl.pallas_call(
        _odd_in_kernel,
        grid=(lpk // tm,),
        in_specs=[pl.BlockSpec((tm, D_MODEL), lambda i: (jnp.minimum(i, last), 0)), full(g0),
                  _weight_spec(w), row_spec(LANES), row_spec(LANES), row_spec(LANES)],
        out_specs=[row_spec(D_MODEL), row_spec(2 * D_MODEL), row_spec(2 * D_MODEL),
                   pl.BlockSpec((SUBLANES, D_MODEL), lambda i: (0, 0))],
        out_shape=[jax.ShapeDtypeStruct((lpk, D_MODEL), BF16), wide, wide,
                   jax.ShapeDtypeStruct((SUBLANES, D_MODEL), F32)],
        compiler_params=pltpu.CompilerParams(dimension_semantics=("arbitrary",),
                                             vmem_limit_bytes=VMEM_LIMIT),
        name="odd_in",
    )(h, g0, w, cos, sin_lo, sin_hi)


def _attn_kernel(lam_ref, sw_ref, kmax_ref, q_ref, k_ref, v_ref, o_ref, q2_scr, p_scr, acc_scr,
                 *, lam_init, tq, tk, nh):
    i = pl.program_id(1)
    lane = lax.broadcasted_iota(jnp.int32, (1, LANES), 1)
    mlo = lane < HALF
    nt = (((1,), (1,)), ((), ()))
    n_blocks = (i * tq) // tk + 1
    wl = 2 * LANES

    def diagonal_mask(off, width):
        qrow = lax.broadcasted_iota(jnp.int32, (2 * tq, width), 0)
        qrow = jnp.where(qrow >= tq, qrow - tq, qrow)
        kcol = lax.broadcasted_iota(jnp.int32, (2 * tq, width), 1)
        return kcol - qrow <= i * tq - off

    for hd in range(nh):
        q = q_ref[:, hd * LANES:(hd + 1) * LANES]
        zq = jnp.zeros_like(q)
        q2 = jnp.concatenate([jnp.where(mlo, q, zq), jnp.where(mlo, zq, q)], axis=0)
        q2f = q2.astype(F32)
        bound = (jnp.sqrt(jnp.sum(q2f * q2f, axis=-1, keepdims=True))
                 * kmax_ref[0:1, hd * LANES:hd * LANES + 1] * BOUND_SLACK)
        q2_scr[hd] = jnp.concatenate([q2, jnp.where(lane == 0, bound, 0.0).astype(BF16)], axis=1)

    def weights_into(hd, slot, t, width):
        off = pl.multiple_of(t * tk, tk)
        w = tk if width is None else width
        s = lax.dot_general(q2_scr[hd], k_ref[pl.ds(off, w), hd * wl:(hd + 1) * wl], nt,
                            preferred_element_type=F32)
        if width is not None:
            s = jnp.where(diagonal_mask(off, w), s, NEG_BIG)
        p_scr[hd, slot, :, :w] = jnp.exp2(s).astype(BF16)

    def add_weighted_values(hd, slot, t, width):
        off = pl.multiple_of(t * tk, tk)
        w = tk if width is None else width
        acc_scr[hd] += jnp.dot(p_scr[hd, slot, :, :w], v_ref[pl.ds(off, w), hd * wl:(hd + 1) * wl],
                               preferred_element_type=F32)

    def tick(t, slot, width):
        for hd in range(nh):
            weights_into(hd, slot, t, width)
            add_weighted_values(hd, 1 - slot, t - 1, None)

    def drain(slot, t, width):
        for hd in range(nh):
            add_weighted_values(hd, slot, t, width)

    acc_scr[...] = jnp.zeros(acc_scr.shape, F32)

    @pl.when(n_blocks > 1)
    def _():
        for hd in range(nh):
            weights_into(hd, 0, 0, None)

    def pair(u, carry):
        tick(2 * u + 1, 1, None)
        tick(2 * u + 2, 0, None)
        return carry

    n_pairs = jnp.maximum(n_blocks - 2, 0) // 2
    lax.fori_loop(0, n_pairs, pair, 0)
    t1 = 2 * n_pairs + 1

    sub = tk // tq
    for r in range(sub):
        width = (r + 1) * tq
        here = i % sub == r

        @pl.when(jnp.logical_and(n_blocks == 1, here))
        def _():
            for hd in range(nh):
                weights_into(hd, 0, 0, width)
            drain(0, 0, width)

        @pl.when(jnp.logical_and(jnp.logical_and(n_blocks > 1, n_blocks % 2 == 0), here))
        def _():
            tick(t1, 1, width)
            drain(1, t1, width)

        @pl.when(jnp.logical_and(jnp.logical_and(n_blocks > 1, n_blocks % 2 == 1), here))
        def _():
            tick(t1, 1, None)
            tick(t1 + 1, 0, width)
            drain(0, t1 + 1, width)

    lv = lam_ref[...]
    lam = (jnp.exp(jnp.sum(lv[0:1] * lv[1:2], axis=-1, keepdims=True))
           - jnp.exp(jnp.sum(lv[2:3] * lv[3:4], axis=-1, keepdims=True)) + lam_init)

    def write_out(hd, acc):
        o = acc[:, :LANES] / acc[:, LANES:]
        o = o[:tq] - lam * o[tq:]
        o = _rms(o, sw_ref[...], SUBLN_EPS) * (1.0 - lam_init)
        o_ref[:, hd * LANES:(hd + 1) * LANES] = o.astype(BF16)

    for hd in range(nh):
        write_out(hd, acc_scr[hd])

    smallest = jnp.min(acc_scr[:, :, LANES:LANES + 1])
    @pl.when(jnp.logical_not(smallest >= MIN_ROW_SUM))
    def _():
        for hd in range(nh):
            q2 = q2_scr[hd, :, :LANES]

            def block(t, carry, diagonal):
                m, acc = carry
                off = pl.multiple_of(t * tk, tk)
                s = lax.dot_general(q2, k_ref[pl.ds(off, tk), hd * wl:hd * wl + LANES], nt,
                                    preferred_element_type=F32)
                if diagonal:
                    s = jnp.where(diagonal_mask(off, tk), s, NEG_BIG)
                m_new = jnp.maximum(m, jnp.max(s, axis=-1, keepdims=True))
                p = jnp.exp2(s - m_new).astype(BF16)
                pv = jnp.dot(p, v_ref[pl.ds(off, tk), hd * wl:(hd + 1) * wl], preferred_element_type=F32)
                return m_new, jnp.exp2(m - m_new) * acc + pv

            init = (jnp.full((2 * tq, 1), NEG_BIG, F32), jnp.zeros((2 * tq, wl), F32))
            carry = lax.fori_loop(0, n_blocks - 1, lambda t, c: block(t, c, False), init)
            write_out(hd, block(n_blocks - 1, carry, True)[1])


def _diff_attn(q, k, v, kmax, lam_vecs, subln_w, lam_init, lp):
    tq, tk, nh = ATT_Q_BLOCK, ATT_K_BLOCK, ATT_HEADS
    lpk = k.shape[0]
    assert lp % tq == 0 and tk % tq == 0 and lpk % tk == 0 and lpk >= lp and DIFF_HEADS % nh == 0
    blk = pl.BlockSpec((tq, nh * LANES), lambda h, i: (i, h))
    resident = pl.BlockSpec((lpk, 2 * nh * LANES), lambda h, i: (0, h), pipeline_mode=pl.Buffered(1))
    full = lambda arr: pl.BlockSpec(arr.shape, lambda h, i: (0,) * arr.ndim)
    return pl.pallas_call(
        functools.partial(_attn_kernel, lam_init=lam_init, tq=tq, tk=tk, nh=nh),
        grid=(DIFF_HEADS // nh, lp // tq),
        in_specs=[full(lam_vecs), full(subln_w), pl.BlockSpec((SUBLANES, nh * LANES), lambda h, i: (0, h)),
                  blk, resident, resident],
        out_specs=blk,
        out_shape=jax.ShapeDtypeStruct((lp, D_MODEL), BF16),
        scratch_shapes=[pltpu.VMEM((nh, 2 * tq, 2 * LANES), BF16), pltpu.VMEM((nh, 2, 2 * tq, tk), BF16),
                        pltpu.VMEM((nh, 2 * tq, 2 * LANES), F32)],
        compiler_params=pltpu.CompilerParams(dimension_semantics=("parallel", "arbitrary"),
                                             vmem_limit_bytes=VMEM_LIMIT),
        name="diff_attn",
    )(lam_vecs, subln_w, kmax, q, k, v)


def _odd_out_kernel(h_ref, o_ref, wout_ref, g1_ref, g2_ref, g3_ref, w1_ref, w2_ref, out_ref):
    m = jnp.dot(o_ref[...], wout_ref[...], preferred_element_type=F32)
    hm = h_ref[...] + _rms(m, g1_ref[...], RMS_EPS)
    out_ref[...] = _mlp_residual(hm, g2_ref[...], g3_ref[...], w1_ref, w2_ref)


def _odd_out(h, o, w_out, g1, g2, g3, w1, w2):
    lp = h.shape[0]
    tm = _pick_tile(lp, ROW_TILES)
    row_spec = lambda width: pl.BlockSpec((tm, width), lambda i: (i, 0))
    full = lambda arr: pl.BlockSpec(arr.shape, lambda i: (0,) * arr.ndim)
    return pl.pallas_call(
        _odd_out_kernel,
        grid=(lp // tm,),
        in_specs=[row_spec(D_MODEL), row_spec(D_MODEL), _weight_spec(w_out), full(g1), full(g2), full(g3),
                  _weight_spec(w1), _weight_spec(w2)],
        out_specs=row_spec(D_MODEL),
        out_shape=jax.ShapeDtypeStruct((lp, D_MODEL), F32),
        compiler_params=pltpu.CompilerParams(dimension_semantics=("parallel",),
                                             vmem_limit_bytes=VMEM_LIMIT),
        name="odd_out",
    )(h, o, w_out, g1, g2, g3, w1, w2)


def _forward(x, meta, norm_g, mlp_w1, mlp_w2, ev, od):
    seq = x.shape[0]
    length = N_META + seq
    lp = -(-length // ROW_ALIGN) * ROW_ALIGN
    h = jnp.concatenate([meta.astype(x.dtype), x, jnp.zeros((lp - length, D_MODEL), x.dtype)], axis=0)

    lpk = -(-lp // ATT_K_BLOCK) * ATT_K_BLOCK
    pos = jnp.arange(lpk, dtype=F32)
    inv = ROPE_THETA ** (-jnp.arange(0, DIFF_HEAD, 2, dtype=F32) / DIFF_HEAD)
    ang = pos[:, None] * inv[None, :]
    ang = jnp.concatenate([ang, ang, ang, ang], axis=-1)
    cos, sin = jnp.cos(ang), jnp.sin(ang)
    first_half = (jnp.arange(LANES) % DIFF_HEAD) < DIFF_HEAD // 2
    sin_lo = jnp.where(first_half, -sin, 0.0)
    sin_hi = jnp.where(first_half, 0.0, sin)

    head_id = jnp.arange(RWKV_WIDTH) // RWKV_HEAD
    seg = (head_id[:, None] == head_id[None, :]).astype(BF16)
    row2 = lambda t: t.reshape(1, -1)

    depth = norm_g.shape[0]
    for i in range(depth):
        g = norm_g[i]
        g0, g1, g2, g3 = (row2(g[n]) for n in range(4))
        w1 = mlp_w1[i].astype(BF16)
        w2 = mlp_w2[i].astype(BF16)
        j = i // 2
        if i % 2 == 0:
            (w_in, mu, w0, w_up, a0, a_up, g_up, k_k, k_a, r_k, ln_w, ln_b, pool_w, pool_scale,
             w_out) = (t[j] for t in ev)
            zeros = jnp.zeros((DECAY_RANK, RWKV_WIDTH), F32)
            lora_w = jnp.concatenate([jnp.concatenate([w_up, zeros], axis=1),
                                      jnp.concatenate([zeros, a_up], axis=1)], axis=0).astype(BF16)
            r, k2, v, na, b, logw, gate, z = _even_in(
                h, g0, w_in.astype(BF16), row2(mu), row2(w0), lora_w, row2(a0), g_up.astype(BF16),
                row2(k_k), row2(k_a), seg, pool_w.astype(BF16), row2(pool_scale))
            o = _rwkv_scan(r, k2, v, na, b, logw)
            h = _even_out(h, o, r, k2, v, gate, z, row2(ln_w), row2(ln_b), row2(r_k), seg,
                          w_out.astype(BF16), g1, g2, g3, w1, w2)
        else:
            w_in, lam_vecs, subln_w, w_out = (t[j] for t in od)
            q, k, v, kmax = _odd_in(h, g0, w_in.astype(BF16), cos, sin_lo, sin_hi)
            lam_init = 0.8 - 0.6 * math.exp(-0.3 * i)
            o = _diff_attn(q, k, v, kmax, lam_vecs, row2(subln_w), lam_init, lp)
            h = _odd_out(h, o, w_out.astype(BF16), g1, g2, g3, w1, w2)
    return h[N_META:length]


def kernel(x, meta, norm_g, mlp_w1, mlp_w2, ev_w_in, ev_mu, ev_w0, ev_w_up, ev_a0, ev_a_up, ev_g_up, ev_k_k,
           ev_k_a, ev_r_k, ev_ln_w, ev_ln_b, ev_pool_w, ev_pool_scale, ev_w_out, od_w_in, od_lambda,
           od_subln_w, od_w_out):
    ev = (ev_w_in, ev_mu, ev_w0, ev_w_up, ev_a0, ev_a_up, ev_g_up, ev_k_k, ev_k_a, ev_r_k, ev_ln_w, ev_ln_b,
          ev_pool_w, ev_pool_scale, ev_w_out)
    od = (od_w_in, od_lambda, od_subln_w, od_w_out)
    outs = [_forward(x[bi], meta, norm_g, mlp_w1, mlp_w2, ev, od) for bi in range(x.shape[0])]
    return jnp.stack(outs, axis=0)
```

```python
import functools
import math

import jax
import jax.numpy as jnp
from jax import lax
from jax.experimental import pallas as pl
from jax.experimental.pallas import tpu as pltpu

F32, BF16 = jnp.float32, jnp.bfloat16

D_MODEL = 1024
N_META = 16
RMS_EPS = 1e-6
D_FF = 4 * D_MODEL
RWKV_HEAD = 64
RWKV_WIDTH = D_MODEL // 2
DECAY_RANK = 64
ICLR_RANK = 64
GATE_RANK = 128
GN_EPS = RWKV_HEAD * 1e-5
POOL_WIDTH = D_MODEL - RWKV_WIDTH
POOL_WINDOWS = (2, 4, 8, 16)
POOL_GROUP_W = POOL_WIDTH // len(POOL_WINDOWS)
POOL_CARRY = 16
SHIFT_WIDTH = 3 * RWKV_WIDTH + DECAY_RANK + ICLR_RANK + GATE_RANK
EVEN_IN = SHIFT_WIDTH + POOL_WIDTH
DIFF_HEADS = 8
DIFF_HEAD = D_MODEL // (2 * DIFF_HEADS)
SUBLN_EPS = 1e-5
ROPE_THETA = 10000.0

LANES = 128
SUBLANES = 8
HALF = LANES // 2
ROW_ALIGN = 256
ROW_TILES = (640, ROW_ALIGN)
KK_NORM_FLOOR = 1e-12
CHUNK = 64
SOLVE_BLOCK = 8
SCAN_CHUNKS = 4
ATT_Q_BLOCK = 256
ATT_K_BLOCK = 1024
ATT_HEADS = 2
ATT_Q_PER_STEP = 2
FF_CHUNK = 1024
NEG_BIG = -1e30
BOUND_SLACK = 1.02
MIN_ROW_SUM = 2.0 ** -100
VMEM_LIMIT = 56 * 1024 * 1024


def _pick_tile(n, candidates):
    for c in candidates:
        if n % c == 0:
            return c
    raise ValueError(f"no tile in {candidates} divides {n}")


def _rms(t, g, eps):
    return t * lax.rsqrt(jnp.mean(t * t, axis=-1, keepdims=True) + eps) * g


def _split2(x):
    hi = x.astype(BF16)
    lo = (x - hi.astype(F32)).astype(BF16)
    return hi, lo


def _dot(a, b):
    return jnp.dot(a.astype(BF16), b.astype(BF16), preferred_element_type=F32)


def _dot2(a, b):
    ab = a.astype(BF16)
    bh, bl = _split2(b)
    d = functools.partial(jnp.dot, preferred_element_type=F32)
    return d(ab, bh) + d(ab, bl)


def _dot_nt(a, b):
    return lax.dot_general(a.astype(BF16), b.astype(BF16), (((1,), (1,)), ((), ())),
                           preferred_element_type=F32)


def _head_sum(x, seg_ref):
    xh, xl = _split2(x)
    d = functools.partial(jnp.dot, preferred_element_type=F32)
    return d(xh, seg_ref[...]) + d(xl, seg_ref[...])


def _sigmoid(x):
    return 1.0 / (1.0 + jnp.exp(-x))


def _softplus(x):
    return jnp.maximum(x, 0.0) + jnp.log(1.0 + jnp.exp(-jnp.abs(x)))


def _mlp_residual(hm, g2, g3, w1_ref, w2_ref):
    n = _rms(hm, g2, RMS_EPS).astype(BF16)
    acc = jnp.zeros(hm.shape, F32)
    for c in range(D_FF // FF_CHUNK):
        cols = slice(c * FF_CHUNK, (c + 1) * FF_CHUNK)
        a = jnp.dot(n, w1_ref[:, cols], preferred_element_type=F32)
        a = jnp.square(jnp.maximum(a, 0.0)).astype(BF16)
        acc = acc + jnp.dot(a, w2_ref[cols, :], preferred_element_type=F32)
    return hm + _rms(acc, g3, RMS_EPS)


def _even_in_kernel(h_ref, g0_ref, win_ref, mu_ref, w0_ref, lora_ref, a0_ref, gup_ref,
                    kk_ref, ka_ref, seg_ref, poolw_ref, pscale_ref,
                    r_out, k_out, v_out, na_out, b_out, lw_out, g_out, z_out,
                    ycarry, ucarry, *, tm):
    i = pl.program_id(0)

    @pl.when(i == 0)
    def _():
        ycarry[...] = jnp.zeros(ycarry.shape, F32)
        ucarry[...] = jnp.zeros(ucarry.shape, F32)

    hn = _rms(h_ref[...], g0_ref[...], RMS_EPS).astype(BF16)
    y = jnp.dot(hn, win_ref[...], preferred_element_type=F32)

    ysh = y[:, :SHIFT_WIDTH]
    row = lax.broadcasted_iota(jnp.int32, (tm, 1), 0)
    prev = jnp.where(row == 0, ycarry[SUBLANES - 1:SUBLANES, :], pltpu.roll(ysh, 1, axis=0))
    ycarry[...] = ysh[tm - SUBLANES:, :]
    ys = ysh + (prev - ysh) * mu_ref[...]

    rw = RWKV_WIDTH
    r = ys[:, 0:rw]
    k = ys[:, rw:2 * rw]
    v = ys[:, 2 * rw:3 * rw]
    wa = ys[:, 3 * rw:3 * rw + LANES]
    gd = ys[:, 3 * rw + LANES:SHIFT_WIDTH]

    lane = lax.broadcasted_iota(jnp.int32, (1, LANES), 1)
    lora_in = jnp.where(lane < DECAY_RANK, jnp.tanh(wa), wa)
    lora = _dot(lora_in, lora_ref[...])
    wlog = -_softplus(-(w0_ref[...] + lora[:, :rw])) - 0.5
    logw = -jnp.exp(wlog)
    a = _sigmoid(a0_ref[...] + lora[:, rw:])
    g = _dot(_sigmoid(gd), gup_ref[...])

    kk = k * kk_ref[...]
    kk = kk * lax.rsqrt(jnp.maximum(_head_sum(kk * kk, seg_ref), KK_NORM_FLOOR ** 2))
    k2 = k * (1.0 + (a - 1.0) * ka_ref[...])

    r_out[...] = r
    k_out[...] = k2
    v_out[...] = v
    na_out[...] = -kk
    b_out[...] = kk * a
    lw_out[...] = logw
    g_out[...] = g

    u = y[:, SHIFT_WIDTH:]
    ext = jnp.concatenate([ucarry[...], u], axis=0)
    ucarry[...] = u[tm - POOL_CARRY:, :]
    t_idx = i * tm + row
    for gi, win in enumerate(POOL_WINDOWS):
        cols = slice(gi * POOL_GROUP_W, (gi + 1) * POOL_GROUP_W)
        s = ext[:, cols]
        span = 1
        while span < win:
            s = s + pltpu.roll(s, span, axis=0)
            span *= 2
        cnt = jnp.minimum(t_idx + 1, win).astype(F32)
        d = s[POOL_CARRY:, :] / cnt - u[:, cols]
        z_out[:, cols] = _dot(d, poolw_ref[gi]) * pscale_ref[:, cols]


def _even_in(h, g0, w_in, mu, w0, lora_w, a0, g_up, k_k, k_a, seg, pool_w, pool_scale):
    lp = h.shape[0]
    tm = _pick_tile(lp, ROW_TILES)
    rw = RWKV_WIDTH
    row_spec = lambda width: pl.BlockSpec((tm, width), lambda i: (i, 0))
    full = lambda arr: pl.BlockSpec(arr.shape, lambda i: (0,) * arr.ndim)
    out_sds = jax.ShapeDtypeStruct((lp, rw), F32)
    return pl.pallas_call(
        functools.partial(_even_in_kernel, tm=tm),
        grid=(lp // tm,),
        in_specs=[row_spec(D_MODEL), full(g0), full(w_in), full(mu), full(w0), full(lora_w), full(a0),
                  full(g_up), full(k_k), full(k_a), full(seg), full(pool_w), full(pool_scale)],
        out_specs=[row_spec(rw)] * 8,
        out_shape=[out_sds] * 8,
        scratch_shapes=[pltpu.VMEM((SUBLANES, SHIFT_WIDTH), F32), pltpu.VMEM((POOL_CARRY, POOL_WIDTH), F32)],
        compiler_params=pltpu.CompilerParams(dimension_semantics=("arbitrary",),
                                             vmem_limit_bytes=VMEM_LIMIT),
        name="even_in",
    )(h, g0, w_in, mu, w0, lora_w, a0, g_up, k_k, k_a, seg, pool_w, pool_scale)


def _scan_kernel(r_ref, k_ref, v_ref, na_ref, b_ref, lw_ref, o_ref, h_scr):
    @pl.when(pl.program_id(0) == 0)
    def _():
        h_scr[...] = jnp.zeros(h_scr.shape, F32)

    c = CHUNK
    n_chunks = r_ref.shape[0] // c
    row = lax.broadcasted_iota(jnp.int32, (c, c), 0)
    col = lax.broadcasted_iota(jnp.int32, (c, c), 1)
    tri = jnp.where(col <= row, 1.0, 0.0).astype(BF16)
    lane = lax.broadcasted_iota(jnp.int32, (1, LANES), 1)
    mlo = lane < HALF
    prow = lax.broadcasted_iota(jnp.int32, (LANES, LANES), 0)
    pcol = lax.broadcasted_iota(jnp.int32, (LANES, LANES), 1)
    same_head = (prow < HALF) == (pcol < HALF)
    diag = prow == pcol
    trow = lax.broadcasted_iota(jnp.int32, (c, LANES), 0)
    tcol = lax.broadcasted_iota(jnp.int32, (c, LANES), 1)
    tcol = jnp.where(tcol >= HALF, tcol - HALF, tcol)
    strict2 = tcol < trow
    incl2 = jnp.concatenate([tcol <= trow] * 2, axis=0)
    same_blk = (tcol // SOLVE_BLOCK) == (trow // SOLVE_BLOCK)
    lane2 = lax.broadcasted_iota(jnp.int32, (1, 2 * LANES), 1)
    zeros_c = jnp.zeros((c, LANES), F32)
    zeros_2c = jnp.zeros((c, 2 * LANES), F32)
    n_pairs = RWKV_WIDTH // LANES
    pair_cols = [slice(p * LANES, (p + 1) * LANES) for p in range(n_pairs)]
    d = functools.partial(jnp.dot, preferred_element_type=F32)

    prep = []
    for ci in range(n_chunks):
        rows = slice(ci * c, (ci + 1) * c)
        lw = lw_ref[rows, :]
        lw_hi = lw.astype(BF16)
        lw_r = lw - lw_hi.astype(F32)
        lw_mid = lw_r.astype(BF16)
        lw_lo = (lw_r - lw_mid.astype(F32)).astype(BF16)
        cum = d(tri, lw_hi) + (d(tri, lw_mid) + d(tri, lw_lo))
        cum_end = cum[c - 1:c, :]
        e_neg = jnp.exp(-cum)
        e_end = jnp.exp(cum_end - cum)
        b_all = b_ref[rows, :]
        k_all = k_ref[rows, :]
        prep.append(dict(r_t=r_ref[rows, :] * jnp.exp(cum), a_t=na_ref[rows, :] * jnp.exp(cum - lw),
                         b_t=b_all * e_neg, k_t=k_all * e_neg, b_h=b_all * e_end, k_h=k_all * e_end,
                         v=v_ref[rows, :], p_end=jnp.exp(cum_end)))
    units = [(ci, p) for ci in range(n_chunks) for p in range(n_pairs)]

    a_all = {}
    for ci, p in units:
        cols = pair_cols[p]
        rt, at = prep[ci]["r_t"][:, cols], prep[ci]["a_t"][:, cols]
        lhs4 = jnp.concatenate([jnp.where(mlo, at, zeros_c), jnp.where(mlo, zeros_c, at),
                                jnp.where(mlo, rt, zeros_c), jnp.where(mlo, zeros_c, rt)], axis=0)
        a_all[ci, p] = _dot_nt(lhs4, jnp.concatenate([prep[ci]["b_t"][:, cols], prep[ci]["k_t"][:, cols]], axis=0))

    heads = []
    for ci, p in units:
        cols = pair_cols[p]
        at, vp = prep[ci]["a_t"][:, cols], prep[ci]["v"][:, cols]
        at_sw = pltpu.roll(at, HALF, axis=1)
        vp_sw = pltpu.roll(vp, HALF, axis=1)
        for hh in range(2):
            nk = jnp.where(strict2, a_all[ci, p][hh * c:(hh + 1) * c], 0.0)
            av = _dot(nk, jnp.concatenate([zeros_c, vp_sw if hh == 0 else vp], axis=0))
            x0 = jnp.where(mlo, at if hh == 0 else at_sw, av)
            nk_sw = pltpu.roll(nk, HALF, axis=1)
            n_split = jnp.where(mlo, jnp.where(same_blk, 0.0, nk), jnp.where(same_blk, nk_sw, 0.0))
            heads.append(jnp.concatenate([x0, n_split], axis=1))

    for _ in range(int(math.log2(SOLVE_BLOCK))):
        nxt = []
        for y in heads:
            prod = _dot(y[:, LANES:], jnp.concatenate([zeros_2c, y], axis=0))
            nxt.append(jnp.where(lane2 >= LANES + HALF, prod, y + prod))
        heads = nxt
    for _ in range(int(math.log2(c // SOLVE_BLOCK))):
        nxt = []
        for y in heads:
            prod = _dot(y[:, LANES:], jnp.concatenate([y, zeros_2c], axis=0))
            nxt.append(jnp.where(lane2 < LANES, y + prod, prod))
        heads = nxt

    big = {}
    for n, (ci, p) in enumerate(units):
        cols = pair_cols[p]
        x_lo, x_hi = heads[2 * n][:, :LANES], heads[2 * n + 1][:, :LANES]
        w_p = jnp.where(mlo, x_lo, pltpu.roll(x_hi, HALF, axis=1))
        u0_p = jnp.where(mlo, pltpu.roll(x_lo, HALF, axis=1), x_hi)
        rhs = jnp.concatenate([jnp.concatenate([w_p, u0_p], axis=1),
                               jnp.concatenate([zeros_c, prep[ci]["v"][:, cols]], axis=1)], axis=0)
        a_r = jnp.where(incl2, a_all[ci, p][2 * c:], 0.0)
        bk_t = jnp.concatenate([prep[ci]["b_h"][:, cols], prep[ci]["k_h"][:, cols]], axis=0).T
        big[ci, p] = _dot(jnp.concatenate([a_r, bk_t], axis=0), rhs)

    state = [h_scr[p] for p in range(n_pairs)]
    for ci in range(n_chunks):
        rows = slice(ci * c, (ci + 1) * c)
        for p, cols in enumerate(pair_cols):
            res = big[ci, p]
            q_hat = prep[ci]["r_t"][:, cols] + jnp.where(mlo, res[:c, :LANES], res[c:2 * c, :LANES])
            o_hat = jnp.where(mlo, res[:c, LANES:], res[c:2 * c, LANES:])
            g_mat = (jnp.where(same_head, res[2 * c:, :LANES], 0.0)
                     + jnp.where(diag, prep[ci]["p_end"][:, cols], 0.0))
            j_mat = jnp.where(same_head, res[2 * c:, LANES:], 0.0)
            st = _dot2(jnp.concatenate([q_hat, g_mat], axis=0), state[p])
            o_ref[rows, cols] = st[:c] + o_hat
            state[p] = st[c:] + j_mat
    for p in range(n_pairs):
        h_scr[p] = state[p]


def _rwkv_scan(r, k2, v, na, b, logw):
    lp, rw = r.shape
    rows = CHUNK * SCAN_CHUNKS
    assert lp % rows == 0
    spec = pl.BlockSpec((rows, rw), lambda i: (i, 0))
    return pl.pallas_call(
        _scan_kernel,
        grid=(lp // rows,),
        in_specs=[spec] * 6,
        out_specs=spec,
        out_shape=jax.ShapeDtypeStruct((lp, rw), F32),
        scratch_shapes=[pltpu.VMEM((rw // LANES, LANES, LANES), F32)],
        compiler_params=pltpu.CompilerParams(dimension_semantics=("arbitrary",),
                                             vmem_limit_bytes=VMEM_LIMIT),
        name="rwkv_scan",
    )(r, k2, v, na, b, logw)


def _even_out_kernel(h_ref, o_ref, r_ref, k_ref, v_ref, g_ref, z_ref, lnw_ref, lnb_ref, rk_ref, seg_ref,
                     wout_ref, g1_ref, g2_ref, g3_ref, w1_ref, w2_ref, out_ref):
    inv_n = 1.0 / RWKV_HEAD
    o = o_ref[...]
    mean = _head_sum(o, seg_ref) * inv_n
    dev = o - mean
    var = _head_sum(dev * dev, seg_ref) * inv_n
    on = dev * lax.rsqrt(var + GN_EPS) * lnw_ref[...] + lnb_ref[...]
    bonus = _head_sum(r_ref[...] * k_ref[...] * rk_ref[...], seg_ref) * v_ref[...]
    om = (on + bonus) * g_ref[...]
    rw = RWKV_WIDTH
    m = _dot(om, wout_ref[:rw, :]) + _dot(z_ref[...], wout_ref[rw:, :])
    hm = h_ref[...] + _rms(m, g1_ref[...], RMS_EPS)
    out_ref[...] = _mlp_residual(hm, g2_ref[...], g3_ref[...], w1_ref, w2_ref)


def _weight_spec(arr):
    return pl.BlockSpec(arr.shape, lambda i: (0,) * arr.ndim, pipeline_mode=pl.Buffered(1))


def _even_out(h, o, r, k2, v, g, z, ln_w, ln_b, r_k, seg, w_out, g1, g2, g3, w1, w2):
    lp = h.shape[0]
    tm = _pick_tile(lp, ROW_TILES)
    row_spec = lambda width: pl.BlockSpec((tm, width), lambda i: (i, 0))
    full = lambda arr: pl.BlockSpec(arr.shape, lambda i: (0,) * arr.ndim)
    rw = RWKV_WIDTH
    return pl.pallas_call(
        _even_out_kernel,
        grid=(lp // tm,),
        in_specs=[row_spec(D_MODEL)] + [row_spec(rw)] * 6 +
                 [full(ln_w), full(ln_b), full(r_k), full(seg), _weight_spec(w_out), full(g1), full(g2),
                  full(g3), _weight_spec(w1), _weight_spec(w2)],
        out_specs=row_spec(D_MODEL),
        out_shape=jax.ShapeDtypeStruct((lp, D_MODEL), F32),
        compiler_params=pltpu.CompilerParams(dimension_semantics=("parallel",),
                                             vmem_limit_bytes=VMEM_LIMIT),
        name="even_out",
    )(h, o, r, k2, v, g, z, ln_w, ln_b, r_k, seg, w_out, g1, g2, g3, w1, w2)


def _odd_in_kernel(h_ref, g0_ref, w_ref, cos_ref, sin_lo_ref, sin_hi_ref, q_out, k_out, v_out, kmax_out):
    @pl.when(pl.program_id(0) == 0)
    def _():
        kmax_out[...] = jnp.zeros(kmax_out.shape, F32)

    hn = _rms(h_ref[...], g0_ref[...], RMS_EPS).astype(BF16)
    y = jnp.dot(hn, w_ref[...], preferred_element_type=F32)
    cos = cos_ref[...]
    sin_lo = sin_lo_ref[...]
    sin_hi = sin_hi_ref[...]
    half = DIFF_HEAD // 2

    def rope(t):
        return (t * cos + pltpu.roll(t, LANES - half, axis=1) * sin_lo + pltpu.roll(t, half, axis=1) * sin_hi)

    scale = DIFF_HEAD ** -0.5 * math.log2(math.e)
    tm = y.shape[0]
    lane = lax.broadcasted_iota(jnp.int32, (tm, LANES), 1)
    minus_one = jnp.where(lane == 0, -1.0, 0.0).astype(BF16)
    ones = jnp.ones((tm, LANES), BF16)
    for j in range(D_MODEL // LANES):
        cols = slice(j * LANES, (j + 1) * LANES)
        kcols = slice(D_MODEL + j * LANES, D_MODEL + (j + 1) * LANES)
        vcols = slice(2 * D_MODEL + j * LANES, 2 * D_MODEL + (j + 1) * LANES)
        wide = slice(2 * j * LANES, (2 * j + 1) * LANES)
        wide_hi = slice((2 * j + 1) * LANES, (2 * j + 2) * LANES)
        q_out[:, cols] = (rope(y[:, cols]) * scale).astype(BF16)
        kb = rope(y[:, kcols]).astype(BF16)
        k_out[:, wide] = kb
        k_out[:, wide_hi] = minus_one
        v_out[:, wide] = y[:, vcols].astype(BF16)
        v_out[:, wide_hi] = ones
        kf = kb.astype(F32)
        knorm = jnp.sqrt(jnp.max(jnp.sum(kf * kf, axis=-1, keepdims=True), axis=0, keepdims=True))
        kmax_out[:, cols] = jnp.maximum(kmax_out[:, cols], jnp.broadcast_to(knorm, (SUBLANES, LANES)))


def _odd_in(h, g0, w, cos, sin_lo, sin_hi):
    lp, lpk = h.shape[0], cos.shape[0]
    tm = ROW_ALIGN
    last = lp // tm - 1
    row_spec = lambda width: pl.BlockSpec((tm, width), lambda i: (i, 0))
    full = lambda arr: pl.BlockSpec(arr.shape, lambda i: (0,) * arr.ndim)
    wide = jax.ShapeDtypeStruct((lpk, 2 * D_MODEL), BF16)
    return pl.pallas_call(
        _odd_in_kernel,
        grid=(lpk // tm,),
        in_specs=[pl.BlockSpec((tm, D_MODEL), lambda i: (jnp.minimum(i, last), 0)), full(g0),
                  _weight_spec(w), row_spec(LANES), row_spec(LANES), row_spec(LANES)],
        out_specs=[row_spec(D_MODEL), row_spec(2 * D_MODEL), row_spec(2 * D_MODEL),
                   pl.BlockSpec((SUBLANES, D_MODEL), lambda i: (0, 0))],
        out_shape=[jax.ShapeDtypeStruct((lpk, D_MODEL), BF16), wide, wide,
                   jax.ShapeDtypeStruct((SUBLANES, D_MODEL), F32)],
        compiler_params=pltpu.CompilerParams(dimension_semantics=("arbitrary",),
                                             vmem_limit_bytes=VMEM_LIMIT),
        name="odd_in",
    )(h, g0, w, cos, sin_lo, sin_hi)


def _attn_kernel(lam_ref, sw_ref, kmax_ref, q_ref, k_ref, v_ref, o_ref, q2_scr, p_scr, acc_scr,
                 *, lam_init, tq, tk, nh, nq):
    i0 = pl.program_id(1) * nq
    lane = lax.broadcasted_iota(jnp.int32, (1, LANES), 1)
    mlo = lane < HALF
    nt = (((1,), (1,)), ((), ()))
    sub = tk // tq
    n_blocks = (i0 * tq) // tk + 1
    wl = 2 * LANES
    units = [(qb, hd) for qb in range(nq) for hd in range(nh)]

    def diagonal_mask(qb, off, width):
        qrow = lax.broadcasted_iota(jnp.int32, (2 * tq, width), 0)
        qrow = jnp.where(qrow >= tq, qrow - tq, qrow)
        kcol = lax.broadcasted_iota(jnp.int32, (2 * tq, width), 1)
        return kcol - qrow <= (i0 + qb) * tq - off

    for u, (qb, hd) in enumerate(units):
        q = q_ref[qb * tq:(qb + 1) * tq, hd * LANES:(hd + 1) * LANES]
        zq = jnp.zeros_like(q)
        q2 = jnp.concatenate([jnp.where(mlo, q, zq), jnp.where(mlo, zq, q)], axis=0)
        q2f = q2.astype(F32)
        bound = (jnp.sqrt(jnp.sum(q2f * q2f, axis=-1, keepdims=True))
                 * kmax_ref[0:1, hd * LANES:hd * LANES + 1] * BOUND_SLACK)
        q2_scr[u] = jnp.concatenate([q2, jnp.where(lane == 0, bound, 0.0).astype(BF16)], axis=1)

    def weights_into(u, slot, t, width):
        qb, hd = units[u]
        off = pl.multiple_of(t * tk, tk)
        w = tk if width is None else width
        s = lax.dot_general(q2_scr[u], k_ref[pl.ds(off, w), hd * wl:(hd + 1) * wl], nt,
                            preferred_element_type=F32)
        if width is not None:
            s = jnp.where(diagonal_mask(qb, off, w), s, NEG_BIG)
        p_scr[u, slot, :, :w] = jnp.exp2(s).astype(BF16)

    def add_weighted_values(u, slot, t, width):
        hd = units[u][1]
        off = pl.multiple_of(t * tk, tk)
        w = tk if width is None else width
        acc_scr[u] += jnp.dot(p_scr[u, slot, :, :w], v_ref[pl.ds(off, w), hd * wl:(hd + 1) * wl],
                              preferred_element_type=F32)

    def tick(t, slot, widths):
        for u, (qb, _) in enumerate(units):
            weights_into(u, slot, t, None if widths is None else widths[qb])
            add_weighted_values(u, 1 - slot, t - 1, None)

    def drain(slot, t, widths):
        for u, (qb, _) in enumerate(units):
            add_weighted_values(u, slot, t, widths[qb])

    acc_scr[...] = jnp.zeros(acc_scr.shape, F32)

    @pl.when(n_blocks > 1)
    def _():
        for u in range(len(units)):
            weights_into(u, 0, 0, None)

    def pair(u, carry):
        tick(2 * u + 1, 1, None)
        tick(2 * u + 2, 0, None)
        return carry

    n_pairs = jnp.maximum(n_blocks - 2, 0) // 2
    lax.fori_loop(0, n_pairs, pair, 0)
    t1 = 2 * n_pairs + 1

    for r0 in range(0, sub, nq):
        widths = [(r0 + qb + 1) * tq for qb in range(nq)]
        here = i0 % sub == r0

        @pl.when(jnp.logical_and(n_blocks == 1, here))
        def _():
            for u, (qb, _) in enumerate(units):
                weights_into(u, 0, 0, widths[qb])
            drain(0, 0, widths)

        @pl.when(jnp.logical_and(jnp.logical_and(n_blocks > 1, n_blocks % 2 == 0), here))
        def _():
            tick(t1, 1, widths)
            drain(1, t1, widths)

        @pl.when(jnp.logical_and(jnp.logical_and(n_blocks > 1, n_blocks % 2 == 1), here))
        def _():
            tick(t1, 1, None)
            tick(t1 + 1, 0, widths)
            drain(0, t1 + 1, widths)

    lv = lam_ref[...]
    lam = (jnp.exp(jnp.sum(lv[0:1] * lv[1:2], axis=-1, keepdims=True))
           - jnp.exp(jnp.sum(lv[2:3] * lv[3:4], axis=-1, keepdims=True)) + lam_init)

    def write_out(u, acc):
        qb, hd = units[u]
        o = acc[:, :LANES] / acc[:, LANES:]
        o = o[:tq] - lam * o[tq:]
        o = _rms(o, sw_ref[...], SUBLN_EPS) * (1.0 - lam_init)
        o_ref[qb * tq:(qb + 1) * tq, hd * LANES:(hd + 1) * LANES] = o.astype(BF16)

    for u in range(len(units)):
        write_out(u, acc_scr[u])

    smallest = jnp.min(acc_scr[:, :, LANES:LANES + 1])
    @pl.when(jnp.logical_not(smallest >= MIN_ROW_SUM))
    def _():
        for u, (qb, hd) in enumerate(units):
            q2 = q2_scr[u, :, :LANES]

            def block(t, carry, diagonal):
                m, acc = carry
                off = pl.multiple_of(t * tk, tk)
                s = lax.dot_general(q2, k_ref[pl.ds(off, tk), hd * wl:hd * wl + LANES], nt,
                                    preferred_element_type=F32)
                if diagonal:
                    s = jnp.where(diagonal_mask(qb, off, tk), s, NEG_BIG)
                m_new = jnp.maximum(m, jnp.max(s, axis=-1, keepdims=True))
                p = jnp.exp2(s - m_new).astype(BF16)
                pv = jnp.dot(p, v_ref[pl.ds(off, tk), hd * wl:(hd + 1) * wl], preferred_element_type=F32)
                return m_new, jnp.exp2(m - m_new) * acc + pv

            init = (jnp.full((2 * tq, 1), NEG_BIG, F32), jnp.zeros((2 * tq, wl), F32))
            carry = lax.fori_loop(0, n_blocks - 1, lambda t, c: block(t, c, False), init)
            write_out(u, block(n_blocks - 1, carry, True)[1])


def _diff_attn(q, k, v, kmax, lam_vecs, subln_w, lam_init, lp):
    tq, tk, nh, nq = ATT_Q_BLOCK, ATT_K_BLOCK, ATT_HEADS, ATT_Q_PER_STEP
    lpk = k.shape[0]
    assert tk % (nq * tq) == 0 and lpk % tk == 0 and lpk >= lp and DIFF_HEADS % nh == 0
    blk = pl.BlockSpec((nq * tq, nh * LANES), lambda h, i: (i, h))
    resident = pl.BlockSpec((lpk, 2 * nh * LANES), lambda h, i: (0, h), pipeline_mode=pl.Buffered(1))
    full = lambda arr: pl.BlockSpec(arr.shape, lambda h, i: (0,) * arr.ndim)
    n_units = nq * nh
    return pl.pallas_call(
        functools.partial(_attn_kernel, lam_init=lam_init, tq=tq, tk=tk, nh=nh, nq=nq),
        grid=(DIFF_HEADS // nh, -(-lp // (nq * tq))),
        in_specs=[full(lam_vecs), full(subln_w), pl.BlockSpec((SUBLANES, nh * LANES), lambda h, i: (0, h)),
                  blk, resident, resident],
        out_specs=blk,
        out_shape=jax.ShapeDtypeStruct((lpk, D_MODEL), BF16),
        scratch_shapes=[pltpu.VMEM((n_units, 2 * tq, 2 * LANES), BF16),
                        pltpu.VMEM((n_units, 2, 2 * tq, tk), BF16),
                        pltpu.VMEM((n_units, 2 * tq, 2 * LANES), F32)],
        compiler_params=pltpu.CompilerParams(dimension_semantics=("parallel", "arbitrary"),
                                             vmem_limit_bytes=VMEM_LIMIT),
        name="diff_attn",
    )(lam_vecs, subln_w, kmax, q, k, v)


def _odd_out_kernel(h_ref, o_ref, wout_ref, g1_ref, g2_ref, g3_ref, w1_ref, w2_ref, out_ref):
    m = jnp.dot(o_ref[...], wout_ref[...], preferred_element_type=F32)
    hm = h_ref[...] + _rms(m, g1_ref[...], RMS_EPS)
    out_ref[...] = _mlp_residual(hm, g2_ref[...], g3_ref[...], w1_ref, w2_ref)


def _odd_out(h, o, w_out, g1, g2, g3, w1, w2):
    lp = h.shape[0]
    tm = _pick_tile(lp, ROW_TILES)
    row_spec = lambda width: pl.BlockSpec((tm, width), lambda i: (i, 0))
    full = lambda arr: pl.BlockSpec(arr.shape, lambda i: (0,) * arr.ndim)
    return pl.pallas_call(
        _odd_out_kernel,
        grid=(lp // tm,),
        in_specs=[row_spec(D_MODEL), row_spec(D_MODEL), _weight_spec(w_out), full(g1), full(g2), full(g3),
                  _weight_spec(w1), _weight_spec(w2)],
        out_specs=row_spec(D_MODEL),
        out_shape=jax.ShapeDtypeStruct((lp, D_MODEL), F32),
        compiler_params=pltpu.CompilerParams(dimension_semantics=("parallel",),
                                             vmem_limit_bytes=VMEM_LIMIT),
        name="odd_out",
    )(h, o, w_out, g1, g2, g3, w1, w2)


def _forward(x, meta, norm_g, mlp_w1, mlp_w2, ev, od):
    seq = x.shape[0]
    length = N_META + seq
    lp = -(-length // ROW_ALIGN) * ROW_ALIGN
    h = jnp.concatenate([meta.astype(x.dtype), x, jnp.zeros((lp - length, D_MODEL), x.dtype)], axis=0)

    lpk = -(-lp // ATT_K_BLOCK) * ATT_K_BLOCK
    pos = jnp.arange(lpk, dtype=F32)
    inv = ROPE_THETA ** (-jnp.arange(0, DIFF_HEAD, 2, dtype=F32) / DIFF_HEAD)
    ang = pos[:, None] * inv[None, :]
    ang = jnp.concatenate([ang, ang, ang, ang], axis=-1)
    cos, sin = jnp.cos(ang), jnp.sin(ang)
    first_half = (jnp.arange(LANES) % DIFF_HEAD) < DIFF_HEAD // 2
    sin_lo = jnp.where(first_half, -sin, 0.0)
    sin_hi = jnp.where(first_half, 0.0, sin)

    head_id = jnp.arange(RWKV_WIDTH) // RWKV_HEAD
    seg = (head_id[:, None] == head_id[None, :]).astype(BF16)
    row2 = lambda t: t.reshape(1, -1)

    depth = norm_g.shape[0]
    for i in range(depth):
        g = norm_g[i]
        g0, g1, g2, g3 = (row2(g[n]) for n in range(4))
        w1 = mlp_w1[i].astype(BF16)
        w2 = mlp_w2[i].astype(BF16)
        j = i // 2
        if i % 2 == 0:
            (w_in, mu, w0, w_up, a0, a_up, g_up, k_k, k_a, r_k, ln_w, ln_b, pool_w, pool_scale,
             w_out) = (t[j] for t in ev)
            zeros = jnp.zeros((DECAY_RANK, RWKV_WIDTH), F32)
            lora_w = jnp.concatenate([jnp.concatenate([w_up, zeros], axis=1),
                                      jnp.concatenate([zeros, a_up], axis=1)], axis=0).astype(BF16)
            r, k2, v, na, b, logw, gate, z = _even_in(
                h, g0, w_in.astype(BF16), row2(mu), row2(w0), lora_w, row2(a0), g_up.astype(BF16),
                row2(k_k), row2(k_a), seg, pool_w.astype(BF16), row2(pool_scale))
            o = _rwkv_scan(r, k2, v, na, b, logw)
            h = _even_out(h, o, r, k2, v, gate, z, row2(ln_w), row2(ln_b), row2(r_k), seg,
                          w_out.astype(BF16), g1, g2, g3, w1, w2)
        else:
            w_in, lam_vecs, subln_w, w_out = (t[j] for t in od)
            q, k, v, kmax = _odd_in(h, g0, w_in.astype(BF16), cos, sin_lo, sin_hi)
            lam_init = 0.8 - 0.6 * math.exp(-0.3 * i)
            o = _diff_attn(q, k, v, kmax, lam_vecs, row2(subln_w), lam_init, lp)
            h = _odd_out(h, o, w_out.astype(BF16), g1, g2, g3, w1, w2)
    return h[N_META:length]


def kernel(x, meta, norm_g, mlp_w1, mlp_w2, ev_w_in, ev_mu, ev_w0, ev_w_up, ev_a0, ev_a_up, ev_g_up, ev_k_k,
           ev_k_a, ev_r_k, ev_ln_w, ev_ln_b, ev_pool_w, ev_pool_scale, ev_w_out, od_w_in, od_lambda,
           od_subln_w, od_w_out):
    ev = (ev_w_in, ev_mu, ev_w0, ev_w_up, ev_a0, ev_a_up, ev_g_up, ev_k_k, ev_k_a, ev_r_k, ev_ln_w, ev_ln_b,
          ev_pool_w, ev_pool_scale, ev_w_out)
    od = (od_w_in, od_lambda, od_subln_w, od_w_out)
    outs = [_forward(x[bi], meta, norm_g, mlp_w1, mlp_w2, ev, od) for bi in range(x.shape[0])]
    return jnp.stack(outs, axis=0)
```

```python
import functools
import math

import jax
import jax.numpy as jnp
import numpy as np
from jax import lax
from jax.experimental import pallas as pl
from jax.experimental.pallas import tpu as pltpu

F32, BF16 = jnp.float32, jnp.bfloat16

D_MODEL = 1024
N_META = 16
RMS_EPS = 1e-6
D_FF = 4 * D_MODEL
RWKV_HEAD = 64
RWKV_WIDTH = D_MODEL // 2
DECAY_RANK = 64
ICLR_RANK = 64
GATE_RANK = 128
GN_EPS = RWKV_HEAD * 1e-5
POOL_WIDTH = D_MODEL - RWKV_WIDTH
POOL_WINDOWS = (2, 4, 8, 16)
POOL_GROUP_W = POOL_WIDTH // len(POOL_WINDOWS)
POOL_CARRY = 16
SHIFT_WIDTH = 3 * RWKV_WIDTH + DECAY_RANK + ICLR_RANK + GATE_RANK
EVEN_IN = SHIFT_WIDTH + POOL_WIDTH
DIFF_HEADS = 8
DIFF_HEAD = D_MODEL // (2 * DIFF_HEADS)
SUBLN_EPS = 1e-5
ROPE_THETA = 10000.0

LANES = 128
SUBLANES = 8
HALF = LANES // 2
ROW_ALIGN = 256
ROW_TILES = (640, ROW_ALIGN)
KK_NORM_FLOOR = 1e-12
CHUNK = 64
SOLVE_BLOCK = 8
SCAN_CHUNKS = 4
ATT_Q_BLOCK = 256
ATT_K_BLOCK = 1024
ATT_HEADS = 2
ATT_Q_PER_STEP = 2
FF_CHUNK = 1024
NEG_BIG = -1e30
BOUND_SLACK = 1.02
MIN_ROW_SUM = 2.0 ** -100
VMEM_LIMIT = 56 * 1024 * 1024


def _pick_tile(n, candidates):
    for c in candidates:
        if n % c == 0:
            return c
    raise ValueError(f"no tile in {candidates} divides {n}")


def _rms(t, g, eps):
    return t * lax.rsqrt(jnp.mean(t * t, axis=-1, keepdims=True) + eps) * g


def _split2(x):
    hi = x.astype(BF16)
    lo = (x - hi.astype(F32)).astype(BF16)
    return hi, lo


def _dot(a, b):
    return jnp.dot(a.astype(BF16), b.astype(BF16), preferred_element_type=F32)


def _dot_nt(a, b):
    return lax.dot_general(a.astype(BF16), b.astype(BF16), (((1,), (1,)), ((), ())),
                           preferred_element_type=F32)


def _head_sum(x, seg_ref):
    xh, xl = _split2(x)
    d = functools.partial(jnp.dot, preferred_element_type=F32)
    return d(xh, seg_ref[...]) + d(xl, seg_ref[...])


def _sigmoid(x):
    return 1.0 / (1.0 + jnp.exp(-x))


def _softplus(x):
    return jnp.maximum(x, 0.0) + jnp.log(1.0 + jnp.exp(-jnp.abs(x)))


def _mlp_residual(hm, g2, g3, w1_ref, w2_ref):
    n = _rms(hm, g2, RMS_EPS).astype(BF16)
    acc = jnp.zeros(hm.shape, F32)
    for c in range(D_FF // FF_CHUNK):
        cols = slice(c * FF_CHUNK, (c + 1) * FF_CHUNK)
        a = jnp.dot(n, w1_ref[:, cols], preferred_element_type=F32)
        a = jnp.square(jnp.maximum(a, 0.0)).astype(BF16)
        acc = acc + jnp.dot(a, w2_ref[cols, :], preferred_element_type=F32)
    return hm + _rms(acc, g3, RMS_EPS)


def _even_in_kernel(h_ref, g0_ref, win_ref, mu_ref, w0_ref, lora_ref, a0_ref, gup_ref,
                    kk_ref, ka_ref, seg_ref, poolw_ref, pscale_ref,
                    r_out, k_out, v_out, na_out, b_out, lw_out, g_out, z_out,
                    ycarry, ucarry, *, tm):
    i = pl.program_id(0)

    @pl.when(i == 0)
    def _():
        ycarry[...] = jnp.zeros(ycarry.shape, F32)
        ucarry[...] = jnp.zeros(ucarry.shape, F32)

    hn = _rms(h_ref[...], g0_ref[...], RMS_EPS).astype(BF16)
    y = jnp.dot(hn, win_ref[...], preferred_element_type=F32)

    ysh = y[:, :SHIFT_WIDTH]
    row = lax.broadcasted_iota(jnp.int32, (tm, 1), 0)
    prev = jnp.where(row == 0, ycarry[SUBLANES - 1:SUBLANES, :], pltpu.roll(ysh, 1, axis=0))
    ycarry[...] = ysh[tm - SUBLANES:, :]
    ys = ysh + (prev - ysh) * mu_ref[...]

    rw = RWKV_WIDTH
    r = ys[:, 0:rw]
    k = ys[:, rw:2 * rw]
    v = ys[:, 2 * rw:3 * rw]
    wa = ys[:, 3 * rw:3 * rw + LANES]
    gd = ys[:, 3 * rw + LANES:SHIFT_WIDTH]

    lane = lax.broadcasted_iota(jnp.int32, (1, LANES), 1)
    lora_in = jnp.where(lane < DECAY_RANK, jnp.tanh(wa), wa)
    lora = _dot(lora_in, lora_ref[...])
    wlog = -_softplus(-(w0_ref[...] + lora[:, :rw])) - 0.5
    logw = -jnp.exp(wlog)
    a = _sigmoid(a0_ref[...] + lora[:, rw:])
    g = _dot(_sigmoid(gd), gup_ref[...])

    kk = k * kk_ref[...]
    kk = kk * lax.rsqrt(jnp.maximum(_head_sum(kk * kk, seg_ref), KK_NORM_FLOOR ** 2))
    k2 = k * (1.0 + (a - 1.0) * ka_ref[...])

    r_out[...] = r
    k_out[...] = k2
    v_out[...] = v
    na_out[...] = -kk
    b_out[...] = kk * a
    lw_out[...] = logw
    g_out[...] = g

    u = y[:, SHIFT_WIDTH:]
    ext = jnp.concatenate([ucarry[...], u], axis=0)
    ucarry[...] = u[tm - POOL_CARRY:, :]
    t_idx = i * tm + row
    for gi, win in enumerate(POOL_WINDOWS):
        cols = slice(gi * POOL_GROUP_W, (gi + 1) * POOL_GROUP_W)
        s = ext[:, cols]
        span = 1
        while span < win:
            s = s + pltpu.roll(s, span, axis=0)
            span *= 2
        cnt = jnp.minimum(t_idx + 1, win).astype(F32)
        d = s[POOL_CARRY:, :] / cnt - u[:, cols]
        z_out[:, cols] = _dot(d, poolw_ref[gi]) * pscale_ref[:, cols]


def _even_in(h, g0, w_in, mu, w0, lora_w, a0, g_up, k_k, k_a, seg, pool_w, pool_scale):
    lp = h.shape[0]
    tm = _pick_tile(lp, ROW_TILES)
    rw = RWKV_WIDTH
    row_spec = lambda width: pl.BlockSpec((tm, width), lambda i: (i, 0))
    full = lambda arr: pl.BlockSpec(arr.shape, lambda i: (0,) * arr.ndim)
    out_sds = jax.ShapeDtypeStruct((lp, rw), F32)
    return pl.pallas_call(
        functools.partial(_even_in_kernel, tm=tm),
        grid=(lp // tm,),
        in_specs=[row_spec(D_MODEL), full(g0), full(w_in), full(mu), full(w0), full(lora_w), full(a0),
                  full(g_up), full(k_k), full(k_a), full(seg), full(pool_w), full(pool_scale)],
        out_specs=[row_spec(rw)] * 8,
        out_shape=[out_sds] * 8,
        scratch_shapes=[pltpu.VMEM((SUBLANES, SHIFT_WIDTH), F32), pltpu.VMEM((POOL_CARRY, POOL_WIDTH), F32)],
        compiler_params=pltpu.CompilerParams(dimension_semantics=("arbitrary",),
                                             vmem_limit_bytes=VMEM_LIMIT),
        name="even_in",
    )(h, g0, w_in, mu, w0, lora_w, a0, g_up, k_k, k_a, seg, pool_w, pool_scale)


def _scan_kernel(r_ref, k_ref, v_ref, na_ref, b_ref, lw_ref, o_ref, h_scr):
    @pl.when(pl.program_id(0) == 0)
    def _():
        h_scr[...] = jnp.zeros(h_scr.shape, F32)

    c = CHUNK
    n_chunks = r_ref.shape[0] // c
    row = lax.broadcasted_iota(jnp.int32, (c, c), 0)
    col = lax.broadcasted_iota(jnp.int32, (c, c), 1)
    tri = jnp.where(col <= row, 1.0, 0.0).astype(BF16)
    lane = lax.broadcasted_iota(jnp.int32, (1, LANES), 1)
    mlo = lane < HALF
    prow = lax.broadcasted_iota(jnp.int32, (LANES, LANES), 0)
    pcol = lax.broadcasted_iota(jnp.int32, (LANES, LANES), 1)
    same_head = (prow < HALF) == (pcol < HALF)
    diag = prow == pcol
    trow = lax.broadcasted_iota(jnp.int32, (c, LANES), 0)
    tcol = lax.broadcasted_iota(jnp.int32, (c, LANES), 1)
    tcol = jnp.where(tcol >= HALF, tcol - HALF, tcol)
    strict2 = tcol < trow
    incl2 = jnp.concatenate([tcol <= trow] * 2, axis=0)
    same_blk = (tcol // SOLVE_BLOCK) == (trow // SOLVE_BLOCK)
    lane2 = lax.broadcasted_iota(jnp.int32, (1, 2 * LANES), 1)
    zeros_c = jnp.zeros((c, LANES), F32)
    zeros_2c = jnp.zeros((c, 2 * LANES), F32)
    n_pairs = RWKV_WIDTH // LANES
    pair_cols = [slice(p * LANES, (p + 1) * LANES) for p in range(n_pairs)]
    d = functools.partial(jnp.dot, preferred_element_type=F32)

    prep = []
    for ci in range(n_chunks):
        rows = slice(ci * c, (ci + 1) * c)
        lw = lw_ref[rows, :]
        lw_hi = lw.astype(BF16)
        lw_r = lw - lw_hi.astype(F32)
        lw_mid = lw_r.astype(BF16)
        lw_lo = (lw_r - lw_mid.astype(F32)).astype(BF16)
        cum = d(tri, lw_hi) + (d(tri, lw_mid) + d(tri, lw_lo))
        cum_end = cum[c - 1:c, :]
        e_neg = jnp.exp(-cum)
        e_end = jnp.exp(cum_end - cum)
        b_all = b_ref[rows, :]
        k_all = k_ref[rows, :]
        prep.append(dict(r_t=r_ref[rows, :] * jnp.exp(cum), a_t=na_ref[rows, :] * jnp.exp(cum - lw),
                         b_t=b_all * e_neg, k_t=k_all * e_neg, b_h=b_all * e_end, k_h=k_all * e_end,
                         v=v_ref[rows, :], p_end=jnp.exp(cum_end)))
    units = [(ci, p) for ci in range(n_chunks) for p in range(n_pairs)]

    a_all = {}
    for ci, p in units:
        cols = pair_cols[p]
        rt, at = prep[ci]["r_t"][:, cols], prep[ci]["a_t"][:, cols]
        lhs4 = jnp.concatenate([jnp.where(mlo, at, zeros_c), jnp.where(mlo, zeros_c, at),
                                jnp.where(mlo, rt, zeros_c), jnp.where(mlo, zeros_c, rt)], axis=0)
        a_all[ci, p] = _dot_nt(lhs4, jnp.concatenate([prep[ci]["b_t"][:, cols], prep[ci]["k_t"][:, cols]], axis=0))

    heads = []
    for ci, p in units:
        cols = pair_cols[p]
        at, vp = prep[ci]["a_t"][:, cols], prep[ci]["v"][:, cols]
        at_sw = pltpu.roll(at, HALF, axis=1)
        vp_sw = pltpu.roll(vp, HALF, axis=1)
        for hh in range(2):
            nk = jnp.where(strict2, a_all[ci, p][hh * c:(hh + 1) * c], 0.0)
            av = _dot(nk, jnp.concatenate([zeros_c, vp_sw if hh == 0 else vp], axis=0))
            x0 = jnp.where(mlo, at if hh == 0 else at_sw, av)
            nk_sw = pltpu.roll(nk, HALF, axis=1)
            n_split = jnp.where(mlo, jnp.where(same_blk, 0.0, nk), jnp.where(same_blk, nk_sw, 0.0))
            heads.append(jnp.concatenate([x0, n_split], axis=1))

    for _ in range(int(math.log2(SOLVE_BLOCK))):
        nxt = []
        for y in heads:
            prod = _dot(y[:, LANES:], jnp.concatenate([zeros_2c, y], axis=0))
            nxt.append(jnp.where(lane2 >= LANES + HALF, prod, y + prod))
        heads = nxt
    for _ in range(int(math.log2(c // SOLVE_BLOCK))):
        nxt = []
        for y in heads:
            prod = _dot(y[:, LANES:], jnp.concatenate([y, zeros_2c], axis=0))
            nxt.append(jnp.where(lane2 < LANES, y + prod, prod))
        heads = nxt

    big = {}
    for n, (ci, p) in enumerate(units):
        cols = pair_cols[p]
        x_lo, x_hi = heads[2 * n][:, :LANES], heads[2 * n + 1][:, :LANES]
        w_p = jnp.where(mlo, x_lo, pltpu.roll(x_hi, HALF, axis=1))
        u0_p = jnp.where(mlo, pltpu.roll(x_lo, HALF, axis=1), x_hi)
        rhs = jnp.concatenate([jnp.concatenate([w_p, u0_p], axis=1),
                               jnp.concatenate([zeros_c, prep[ci]["v"][:, cols]], axis=1)], axis=0)
        a_r = jnp.where(incl2, a_all[ci, p][2 * c:], 0.0)
        bk_t = jnp.concatenate([prep[ci]["b_h"][:, cols], prep[ci]["k_h"][:, cols]], axis=0).T
        big[ci, p] = _dot(jnp.concatenate([a_r, bk_t], axis=0), rhs)

    state = [h_scr[p] for p in range(n_pairs)]
    for ci in range(n_chunks):
        rows = slice(ci * c, (ci + 1) * c)
        for p, cols in enumerate(pair_cols):
            res = big[ci, p]
            q_hat = prep[ci]["r_t"][:, cols] + jnp.where(mlo, res[:c, :LANES], res[c:2 * c, :LANES])
            o_hat = jnp.where(mlo, res[:c, LANES:], res[c:2 * c, LANES:])
            g_mat = (jnp.where(same_head, res[2 * c:, :LANES], 0.0)
                     + jnp.where(diag, prep[ci]["p_end"][:, cols], 0.0))
            j_mat = jnp.where(same_head, res[2 * c:, LANES:], 0.0)
            st = _dot(jnp.concatenate([q_hat, g_mat], axis=0), state[p])
            o_ref[rows, cols] = st[:c] + o_hat
            state[p] = st[c:] + j_mat
    for p in range(n_pairs):
        h_scr[p] = state[p]


def _rwkv_scan(r, k2, v, na, b, logw):
    lp, rw = r.shape
    rows = CHUNK * SCAN_CHUNKS
    assert lp % rows == 0
    spec = pl.BlockSpec((rows, rw), lambda i: (i, 0))
    return pl.pallas_call(
        _scan_kernel,
        grid=(lp // rows,),
        in_specs=[spec] * 6,
        out_specs=spec,
        out_shape=jax.ShapeDtypeStruct((lp, rw), F32),
        scratch_shapes=[pltpu.VMEM((rw // LANES, LANES, LANES), F32)],
        compiler_params=pltpu.CompilerParams(dimension_semantics=("arbitrary",),
                                             vmem_limit_bytes=VMEM_LIMIT),
        name="rwkv_scan",
    )(r, k2, v, na, b, logw)


def _even_out_kernel(h_ref, o_ref, r_ref, k_ref, v_ref, g_ref, z_ref, lnw_ref, lnb_ref, rk_ref, seg_ref,
                     wout_ref, g1_ref, g2_ref, g3_ref, w1_ref, w2_ref, out_ref):
    inv_n = 1.0 / RWKV_HEAD
    o = o_ref[...]
    mean = _head_sum(o, seg_ref) * inv_n
    dev = o - mean
    var = _head_sum(dev * dev, seg_ref) * inv_n
    on = dev * lax.rsqrt(var + GN_EPS) * lnw_ref[...] + lnb_ref[...]
    bonus = _head_sum(r_ref[...] * k_ref[...] * rk_ref[...], seg_ref) * v_ref[...]
    om = (on + bonus) * g_ref[...]
    rw = RWKV_WIDTH
    m = _dot(om, wout_ref[:rw, :]) + _dot(z_ref[...], wout_ref[rw:, :])
    hm = h_ref[...] + _rms(m, g1_ref[...], RMS_EPS)
    out_ref[...] = _mlp_residual(hm, g2_ref[...], g3_ref[...], w1_ref, w2_ref)


def _weight_spec(arr):
    return pl.BlockSpec(arr.shape, lambda i: (0,) * arr.ndim, pipeline_mode=pl.Buffered(1))


def _even_out(h, o, r, k2, v, g, z, ln_w, ln_b, r_k, seg, w_out, g1, g2, g3, w1, w2):
    lp = h.shape[0]
    tm = _pick_tile(lp, ROW_TILES)
    row_spec = lambda width: pl.BlockSpec((tm, width), lambda i: (i, 0))
    full = lambda arr: pl.BlockSpec(arr.shape, lambda i: (0,) * arr.ndim)
    rw = RWKV_WIDTH
    return pl.pallas_call(
        _even_out_kernel,
        grid=(lp // tm,),
        in_specs=[row_spec(D_MODEL)] + [row_spec(rw)] * 6 +
                 [full(ln_w), full(ln_b), full(r_k), full(seg), _weight_spec(w_out), full(g1), full(g2),
                  full(g3), _weight_spec(w1), _weight_spec(w2)],
        out_specs=row_spec(D_MODEL),
        out_shape=jax.ShapeDtypeStruct((lp, D_MODEL), F32),
        compiler_params=pltpu.CompilerParams(dimension_semantics=("parallel",),
                                             vmem_limit_bytes=VMEM_LIMIT),
        name="even_out",
    )(h, o, r, k2, v, g, z, ln_w, ln_b, r_k, seg, w_out, g1, g2, g3, w1, w2)


def _odd_in_kernel(h_ref, g0_ref, w_ref, cos_ref, sin_lo_ref, sin_hi_ref, q_out, k_out, v_out, kmax_out):
    @pl.when(pl.program_id(0) == 0)
    def _():
        kmax_out[...] = jnp.zeros(kmax_out.shape, F32)

    hn = _rms(h_ref[...], g0_ref[...], RMS_EPS).astype(BF16)
    y = jnp.dot(hn, w_ref[...], preferred_element_type=F32)
    cos = cos_ref[...]
    sin_lo = sin_lo_ref[...]
    sin_hi = sin_hi_ref[...]
    half = DIFF_HEAD // 2

    def rope(t):
        return (t * cos + pltpu.roll(t, LANES - half, axis=1) * sin_lo + pltpu.roll(t, half, axis=1) * sin_hi)

    scale = DIFF_HEAD ** -0.5 * math.log2(math.e)
    tm = y.shape[0]
    lane = lax.broadcasted_iota(jnp.int32, (tm, LANES), 1)
    minus_one = jnp.where(lane == 0, -1.0, 0.0).astype(BF16)
    ones = jnp.ones((tm, LANES), BF16)
    for j in range(D_MODEL // LANES):
        cols = slice(j * LANES, (j + 1) * LANES)
        kcols = slice(D_MODEL + j * LANES, D_MODEL + (j + 1) * LANES)
        vcols = slice(2 * D_MODEL + j * LANES, 2 * D_MODEL + (j + 1) * LANES)
        wide = slice(2 * j * LANES, (2 * j + 1) * LANES)
        wide_hi = slice((2 * j + 1) * LANES, (2 * j + 2) * LANES)
        q_out[:, cols] = (rope(y[:, cols]) * scale).astype(BF16)
        kb = rope(y[:, kcols]).astype(BF16)
        k_out[:, wide] = kb
        k_out[:, wide_hi] = minus_one
        v_out[:, wide] = y[:, vcols].astype(BF16)
        v_out[:, wide_hi] = ones
        kf = kb.astype(F32)
        knorm = jnp.sqrt(jnp.max(jnp.sum(kf * kf, axis=-1, keepdims=True), axis=0, keepdims=True))
        kmax_out[:, cols] = jnp.maximum(kmax_out[:, cols], jnp.broadcast_to(knorm, (SUBLANES, LANES)))


def _odd_in(h, g0, w, cos, sin_lo, sin_hi):
    lp, lpk = h.shape[0], cos.shape[0]
    tm = ROW_ALIGN
    last = lp // tm - 1
    row_spec = lambda width: pl.BlockSpec((tm, width), lambda i: (i, 0))
    full = lambda arr: pl.BlockSpec(arr.shape, lambda i: (0,) * arr.ndim)
    wide = jax.ShapeDtypeStruct((lpk, 2 * D_MODEL), BF16)
    return pl.pallas_call(
        _odd_in_kernel,
        grid=(lpk // tm,),
        in_specs=[pl.BlockSpec((tm, D_MODEL), lambda i: (jnp.minimum(i, last), 0)), full(g0),
                  _weight_spec(w), row_spec(LANES), row_spec(LANES), row_spec(LANES)],
        out_specs=[row_spec(D_MODEL), row_spec(2 * D_MODEL), row_spec(2 * D_MODEL),
                   pl.BlockSpec((SUBLANES, D_MODEL), lambda i: (0, 0))],
        out_shape=[jax.ShapeDtypeStruct((lpk, D_MODEL), BF16), wide, wide,
                   jax.ShapeDtypeStruct((SUBLANES, D_MODEL), F32)],
        compiler_params=pltpu.CompilerParams(dimension_semantics=("arbitrary",),
                                             vmem_limit_bytes=VMEM_LIMIT),
        name="odd_in",
    )(h, g0, w, cos, sin_lo, sin_hi)


def _attn_kernel(lam_ref, sw_ref, kmax_ref, q_ref, k_ref, v_ref, o_ref, q2_scr, p_scr, acc_scr,
                 *, lam_init, tq, tk, nh, nq):
    i0 = pl.program_id(1) * nq
    lane = lax.broadcasted_iota(jnp.int32, (1, LANES), 1)
    mlo = lane < HALF
    nt = (((1,), (1,)), ((), ()))
    sub = tk // tq
    n_blocks = (i0 * tq) // tk + 1
    wl = 2 * LANES
    units = [(qb, hd) for qb in range(nq) for hd in range(nh)]

    def diagonal_mask(qb, off, width):
        qrow = lax.broadcasted_iota(jnp.int32, (2 * tq, width), 0)
        qrow = jnp.where(qrow >= tq, qrow - tq, qrow)
        kcol = lax.broadcasted_iota(jnp.int32, (2 * tq, width), 1)
        return kcol - qrow <= (i0 + qb) * tq - off

    for u, (qb, hd) in enumerate(units):
        q = q_ref[qb * tq:(qb + 1) * tq, hd * LANES:(hd + 1) * LANES]
        zq = jnp.zeros_like(q)
        q2 = jnp.concatenate([jnp.where(mlo, q, zq), jnp.where(mlo, zq, q)], axis=0)
        q2f = q2.astype(F32)
        bound = (jnp.sqrt(jnp.sum(q2f * q2f, axis=-1, keepdims=True))
                 * kmax_ref[0:1, hd * LANES:hd * LANES + 1] * BOUND_SLACK)
        q2_scr[u] = jnp.concatenate([q2, jnp.where(lane == 0, bound, 0.0).astype(BF16)], axis=1)

    def weights_into(u, slot, t, width):
        qb, hd = units[u]
        off = pl.multiple_of(t * tk, tk)
        w = tk if width is None else width
        s = lax.dot_general(q2_scr[u], k_ref[pl.ds(off, w), hd * wl:(hd + 1) * wl], nt,
                            preferred_element_type=F32)
        if width is not None:
            s = jnp.where(diagonal_mask(qb, off, w), s, NEG_BIG)
        p_scr[u, slot, :, :w] = jnp.exp2(s).astype(BF16)

    def add_weighted_values(u, slot, t, width):
        hd = units[u][1]
        off = pl.multiple_of(t * tk, tk)
        w = tk if width is None else width
        acc_scr[u] += jnp.dot(p_scr[u, slot, :, :w], v_ref[pl.ds(off, w), hd * wl:(hd + 1) * wl],
                              preferred_element_type=F32)

    def tick(t, slot, widths):
        for u, (qb, _) in enumerate(units):
            weights_into(u, slot, t, None if widths is None else widths[qb])
            add_weighted_values(u, 1 - slot, t - 1, None)

    def drain(slot, t, widths):
        for u, (qb, _) in enumerate(units):
            add_weighted_values(u, slot, t, widths[qb])

    acc_scr[...] = jnp.zeros(acc_scr.shape, F32)

    @pl.when(n_blocks > 1)
    def _():
        for u in range(len(units)):
            weights_into(u, 0, 0, None)

    def pair(u, carry):
        tick(2 * u + 1, 1, None)
        tick(2 * u + 2, 0, None)
        return carry

    n_pairs = jnp.maximum(n_blocks - 2, 0) // 2
    lax.fori_loop(0, n_pairs, pair, 0)
    t1 = 2 * n_pairs + 1

    for r0 in range(0, sub, nq):
        widths = [(r0 + qb + 1) * tq for qb in range(nq)]
        here = i0 % sub == r0

        @pl.when(jnp.logical_and(n_blocks == 1, here))
        def _():
            for u, (qb, _) in enumerate(units):
                weights_into(u, 0, 0, widths[qb])
            drain(0, 0, widths)

        @pl.when(jnp.logical_and(jnp.logical_and(n_blocks > 1, n_blocks % 2 == 0), here))
        def _():
            tick(t1, 1, widths)
            drain(1, t1, widths)

        @pl.when(jnp.logical_and(jnp.logical_and(n_blocks > 1, n_blocks % 2 == 1), here))
        def _():
            tick(t1, 1, None)
            tick(t1 + 1, 0, widths)
            drain(0, t1 + 1, widths)

    lv = lam_ref[...]
    lam = (jnp.exp(jnp.sum(lv[0:1] * lv[1:2], axis=-1, keepdims=True))
           - jnp.exp(jnp.sum(lv[2:3] * lv[3:4], axis=-1, keepdims=True)) + lam_init)

    def write_out(u, acc):
        qb, hd = units[u]
        o = acc[:, :LANES] / acc[:, LANES:]
        o = o[:tq] - lam * o[tq:]
        o = _rms(o, sw_ref[...], SUBLN_EPS) * (1.0 - lam_init)
        o_ref[qb * tq:(qb + 1) * tq, hd * LANES:(hd + 1) * LANES] = o.astype(BF16)

    for u in range(len(units)):
        write_out(u, acc_scr[u])

    smallest = jnp.min(acc_scr[:, :, LANES:LANES + 1])
    @pl.when(jnp.logical_not(smallest >= MIN_ROW_SUM))
    def _():
        for u, (qb, hd) in enumerate(units):
            q2 = q2_scr[u, :, :LANES]

            def block(t, carry, diagonal):
                m, acc = carry
                off = pl.multiple_of(t * tk, tk)
                s = lax.dot_general(q2, k_ref[pl.ds(off, tk), hd * wl:hd * wl + LANES], nt,
                                    preferred_element_type=F32)
                if diagonal:
                    s = jnp.where(diagonal_mask(qb, off, tk), s, NEG_BIG)
                m_new = jnp.maximum(m, jnp.max(s, axis=-1, keepdims=True))
                p = jnp.exp2(s - m_new).astype(BF16)
                pv = jnp.dot(p, v_ref[pl.ds(off, tk), hd * wl:(hd + 1) * wl], preferred_element_type=F32)
                return m_new, jnp.exp2(m - m_new) * acc + pv

            init = (jnp.full((2 * tq, 1), NEG_BIG, F32), jnp.zeros((2 * tq, wl), F32))
            carry = lax.fori_loop(0, n_blocks - 1, lambda t, c: block(t, c, False), init)
            write_out(u, block(n_blocks - 1, carry, True)[1])


def _diff_attn(q, k, v, kmax, lam_vecs, subln_w, lam_init, lp):
    tq, tk, nh, nq = ATT_Q_BLOCK, ATT_K_BLOCK, ATT_HEADS, ATT_Q_PER_STEP
    lpk = k.shape[0]
    assert tk % (nq * tq) == 0 and lpk % tk == 0 and lpk >= lp and DIFF_HEADS % nh == 0
    blk = pl.BlockSpec((nq * tq, nh * LANES), lambda h, i: (i, h))
    resident = pl.BlockSpec((lpk, 2 * nh * LANES), lambda h, i: (0, h), pipeline_mode=pl.Buffered(1))
    full = lambda arr: pl.BlockSpec(arr.shape, lambda h, i: (0,) * arr.ndim)
    n_units = nq * nh
    return pl.pallas_call(
        functools.partial(_attn_kernel, lam_init=lam_init, tq=tq, tk=tk, nh=nh, nq=nq),
        grid=(DIFF_HEADS // nh, -(-lp // (nq * tq))),
        in_specs=[full(lam_vecs), full(subln_w), pl.BlockSpec((SUBLANES, nh * LANES), lambda h, i: (0, h)),
                  blk, resident, resident],
        out_specs=blk,
        out_shape=jax.ShapeDtypeStruct((lpk, D_MODEL), BF16),
        scratch_shapes=[pltpu.VMEM((n_units, 2 * tq, 2 * LANES), BF16),
                        pltpu.VMEM((n_units, 2, 2 * tq, tk), BF16),
                        pltpu.VMEM((n_units, 2 * tq, 2 * LANES), F32)],
        compiler_params=pltpu.CompilerParams(dimension_semantics=("parallel", "arbitrary"),
                                             vmem_limit_bytes=VMEM_LIMIT),
        name="diff_attn",
    )(lam_vecs, subln_w, kmax, q, k, v)


def _odd_out_kernel(h_ref, o_ref, wout_ref, g1_ref, g2_ref, g3_ref, w1_ref, w2_ref, out_ref):
    m = jnp.dot(o_ref[...], wout_ref[...], preferred_element_type=F32)
    hm = h_ref[...] + _rms(m, g1_ref[...], RMS_EPS)
    out_ref[...] = _mlp_residual(hm, g2_ref[...], g3_ref[...], w1_ref, w2_ref)


def _odd_out(h, o, w_out, g1, g2, g3, w1, w2):
    lp = h.shape[0]
    tm = _pick_tile(lp, ROW_TILES)
    row_spec = lambda width: pl.BlockSpec((tm, width), lambda i: (i, 0))
    full = lambda arr: pl.BlockSpec(arr.shape, lambda i: (0,) * arr.ndim)
    return pl.pallas_call(
        _odd_out_kernel,
        grid=(lp // tm,),
        in_specs=[row_spec(D_MODEL), row_spec(D_MODEL), _weight_spec(w_out), full(g1), full(g2), full(g3),
                  _weight_spec(w1), _weight_spec(w2)],
        out_specs=row_spec(D_MODEL),
        out_shape=jax.ShapeDtypeStruct((lp, D_MODEL), F32),
        compiler_params=pltpu.CompilerParams(dimension_semantics=("parallel",),
                                             vmem_limit_bytes=VMEM_LIMIT),
        name="odd_out",
    )(h, o, w_out, g1, g2, g3, w1, w2)


def _forward(x, meta, norm_g, mlp_w1, mlp_w2, ev, od):
    seq = x.shape[0]
    length = N_META + seq
    lp = -(-length // ROW_ALIGN) * ROW_ALIGN
    h = jnp.concatenate([meta.astype(x.dtype), x, jnp.zeros((lp - length, D_MODEL), x.dtype)], axis=0)

    lpk = -(-lp // ATT_K_BLOCK) * ATT_K_BLOCK
    inv = ROPE_THETA ** (-np.arange(0, DIFF_HEAD, 2, dtype=np.float64) / DIFF_HEAD)
    ang = np.arange(lpk, dtype=np.float64)[:, None] * inv[None, :]
    ang = np.concatenate([ang, ang, ang, ang], axis=-1)
    first_half = (np.arange(LANES) % DIFF_HEAD) < DIFF_HEAD // 2
    cos = jnp.asarray(np.cos(ang), F32)
    sin_lo = jnp.asarray(np.where(first_half, -np.sin(ang), 0.0), F32)
    sin_hi = jnp.asarray(np.where(first_half, 0.0, np.sin(ang)), F32)

    head_id = jnp.arange(RWKV_WIDTH) // RWKV_HEAD
    seg = (head_id[:, None] == head_id[None, :]).astype(BF16)
    row2 = lambda t: t.reshape(1, -1)

    depth = norm_g.shape[0]
    for i in range(depth):
        g = norm_g[i]
        g0, g1, g2, g3 = (row2(g[n]) for n in range(4))
        w1 = mlp_w1[i].astype(BF16)
        w2 = mlp_w2[i].astype(BF16)
        j = i // 2
        if i % 2 == 0:
            (w_in, mu, w0, w_up, a0, a_up, g_up, k_k, k_a, r_k, ln_w, ln_b, pool_w, pool_scale,
             w_out) = (t[j] for t in ev)
            zeros = jnp.zeros((DECAY_RANK, RWKV_WIDTH), F32)
            lora_w = jnp.concatenate([jnp.concatenate([w_up, zeros], axis=1),
                                      jnp.concatenate([zeros, a_up], axis=1)], axis=0).astype(BF16)
            r, k2, v, na, b, logw, gate, z = _even_in(
                h, g0, w_in.astype(BF16), row2(mu), row2(w0), lora_w, row2(a0), g_up.astype(BF16),
                row2(k_k), row2(k_a), seg, pool_w.astype(BF16), row2(pool_scale))
            o = _rwkv_scan(r, k2, v, na, b, logw)
            h = _even_out(h, o, r, k2, v, gate, z, row2(ln_w), row2(ln_b), row2(r_k), seg,
                          w_out.astype(BF16), g1, g2, g3, w1, w2)
        else:
            w_in, lam_vecs, subln_w, w_out = (t[j] for t in od)
            q, k, v, kmax = _odd_in(h, g0, w_in.astype(BF16), cos, sin_lo, sin_hi)
            lam_init = 0.8 - 0.6 * math.exp(-0.3 * i)
            o = _diff_attn(q, k, v, kmax, lam_vecs, row2(subln_w), lam_init, lp)
            h = _odd_out(h, o, w_out.astype(BF16), g1, g2, g3, w1, w2)
    return h[N_META:length]


def kernel(x, meta, norm_g, mlp_w1, mlp_w2, ev_w_in, ev_mu, ev_w0, ev_w_up, ev_a0, ev_a_up, ev_g_up, ev_k_k,
           ev_k_a, ev_r_k, ev_ln_w, ev_ln_b, ev_pool_w, ev_pool_scale, ev_w_out, od_w_in, od_lambda,
           od_subln_w, od_w_out):
    ev = (ev_w_in, ev_mu, ev_w0, ev_w_up, ev_a0, ev_a_up, ev_g_up, ev_k_k, ev_k_a, ev_r_k, ev_ln_w, ev_ln_b,
          ev_pool_w, ev_pool_scale, ev_w_out)
    od = (od_w_in, od_lambda, od_subln_w, od_w_out)
    outs = [_forward(x[bi], meta, norm_g, mlp_w1, mlp_w2, ev, od) for bi in range(x.shape[0])]
    return jnp.stack(outs, axis=0)
```

```python
import functools
import math

import jax
import jax.numpy as jnp
import numpy as np
from jax import lax
from jax.experimental import pallas as pl
from jax.experimental.pallas import tpu as pltpu

F32, BF16 = jnp.float32, jnp.bfloat16

D_MODEL = 1024
N_META = 16
RMS_EPS = 1e-6
D_FF = 4 * D_MODEL
RWKV_HEAD = 64
RWKV_WIDTH = D_MODEL // 2
DECAY_RANK = 64
ICLR_RANK = 64
GATE_RANK = 128
GN_EPS = RWKV_HEAD * 1e-5
POOL_WIDTH = D_MODEL - RWKV_WIDTH
POOL_WINDOWS = (2, 4, 8, 16)
POOL_GROUP_W = POOL_WIDTH // len(POOL_WINDOWS)
POOL_CARRY = 16
SHIFT_WIDTH = 3 * RWKV_WIDTH + DECAY_RANK + ICLR_RANK + GATE_RANK
EVEN_IN = SHIFT_WIDTH + POOL_WIDTH
DIFF_HEADS = 8
DIFF_HEAD = D_MODEL // (2 * DIFF_HEADS)
SUBLN_EPS = 1e-5
ROPE_THETA = 10000.0

LANES = 128
SUBLANES = 8
HALF = LANES // 2
ROW_ALIGN = 256
ROW_TILES = (640, ROW_ALIGN)
KK_NORM_FLOOR = 1e-12
CHUNK = 64
SOLVE_BLOCK = 8
SCAN_CHUNKS = 4
ATT_Q_BLOCK = 256
ATT_K_BLOCK = 1024
ATT_HEADS = 2
ATT_Q_PER_STEP = 2
FF_CHUNK = 1024
NEG_BIG = -1e30
BOUND_SLACK = 1.02
MIN_ROW_SUM = 2.0 ** -100
VMEM_LIMIT = 56 * 1024 * 1024


def _pick_tile(n, candidates):
    for c in candidates:
        if n % c == 0:
            return c
    raise ValueError(f"no tile in {candidates} divides {n}")


def _rms(t, g, eps):
    return t * lax.rsqrt(jnp.mean(t * t, axis=-1, keepdims=True) + eps) * g


def _split2(x):
    hi = x.astype(BF16)
    lo = (x - hi.astype(F32)).astype(BF16)
    return hi, lo


def _dot(a, b):
    return jnp.dot(a.astype(BF16), b.astype(BF16), preferred_element_type=F32)


def _dot_nt(a, b):
    return lax.dot_general(a.astype(BF16), b.astype(BF16), (((1,), (1,)), ((), ())),
                           preferred_element_type=F32)


def _head_sum(x, seg_ref):
    xh, xl = _split2(x)
    d = functools.partial(jnp.dot, preferred_element_type=F32)
    return d(xh, seg_ref[...]) + d(xl, seg_ref[...])


def _sigmoid(x):
    return 1.0 / (1.0 + jnp.exp(-x))


def _softplus(x):
    return jnp.maximum(x, 0.0) + jnp.log(1.0 + jnp.exp(-jnp.abs(x)))


def _mlp_residual(hm, g2, g3, w1_ref, w2_ref):
    n = _rms(hm, g2, RMS_EPS).astype(BF16)
    acc = jnp.zeros(hm.shape, F32)
    for c in range(D_FF // FF_CHUNK):
        cols = slice(c * FF_CHUNK, (c + 1) * FF_CHUNK)
        a = jnp.dot(n, w1_ref[:, cols], preferred_element_type=F32)
        a = jnp.square(jnp.maximum(a, 0.0)).astype(BF16)
        acc = acc + jnp.dot(a, w2_ref[cols, :], preferred_element_type=F32)
    return hm + _rms(acc, g3, RMS_EPS)


def _even_in_kernel(h_ref, g0_ref, win_ref, mu_ref, w0_ref, lora_ref, a0_ref, gup_ref,
                    kk_ref, ka_ref, seg_ref, poolw_ref, pscale_ref,
                    r_out, k_out, v_out, na_out, b_out, lw_out, g_out, z_out,
                    ycarry, ucarry, *, tm):
    i = pl.program_id(0)

    @pl.when(i == 0)
    def _():
        ycarry[...] = jnp.zeros(ycarry.shape, F32)
        ucarry[...] = jnp.zeros(ucarry.shape, F32)

    hn = _rms(h_ref[...], g0_ref[...], RMS_EPS).astype(BF16)
    y = jnp.dot(hn, win_ref[...], preferred_element_type=F32)

    ysh = y[:, :SHIFT_WIDTH]
    row = lax.broadcasted_iota(jnp.int32, (tm, 1), 0)
    prev = jnp.where(row == 0, ycarry[SUBLANES - 1:SUBLANES, :], pltpu.roll(ysh, 1, axis=0))
    ycarry[...] = ysh[tm - SUBLANES:, :]
    ys = ysh + (prev - ysh) * mu_ref[...]

    rw = RWKV_WIDTH
    r = ys[:, 0:rw]
    k = ys[:, rw:2 * rw]
    v = ys[:, 2 * rw:3 * rw]
    wa = ys[:, 3 * rw:3 * rw + LANES]
    gd = ys[:, 3 * rw + LANES:SHIFT_WIDTH]

    lane = lax.broadcasted_iota(jnp.int32, (1, LANES), 1)
    lora_in = jnp.where(lane < DECAY_RANK, jnp.tanh(wa), wa)
    lora = _dot(lora_in, lora_ref[...])
    wlog = -_softplus(-(w0_ref[...] + lora[:, :rw])) - 0.5
    logw = -jnp.exp(wlog)
    a = _sigmoid(a0_ref[...] + lora[:, rw:])
    g = _dot(_sigmoid(gd), gup_ref[...])

    kk = k * kk_ref[...]
    kk = kk * lax.rsqrt(jnp.maximum(_head_sum(kk * kk, seg_ref), KK_NORM_FLOOR ** 2))
    k2 = k * (1.0 + (a - 1.0) * ka_ref[...])

    r_out[...] = r
    k_out[...] = k2
    v_out[...] = v
    na_out[...] = -kk
    b_out[...] = kk * a
    lw_out[...] = logw
    g_out[...] = g

    u = y[:, SHIFT_WIDTH:]
    ext = jnp.concatenate([ucarry[...], u], axis=0)
    ucarry[...] = u[tm - POOL_CARRY:, :]
    t_idx = i * tm + row
    for gi, win in enumerate(POOL_WINDOWS):
        cols = slice(gi * POOL_GROUP_W, (gi + 1) * POOL_GROUP_W)
        s = ext[:, cols]
        span = 1
        while span < win:
            s = s + pltpu.roll(s, span, axis=0)
            span *= 2
        cnt = jnp.minimum(t_idx + 1, win).astype(F32)
        d = s[POOL_CARRY:, :] / cnt - u[:, cols]
        z_out[:, cols] = _dot(d, poolw_ref[gi]) * pscale_ref[:, cols]


def _even_in(h, g0, w_in, mu, w0, lora_w, a0, g_up, k_k, k_a, seg, pool_w, pool_scale):
    lp = h.shape[0]
    tm = _pick_tile(lp, ROW_TILES)
    rw = RWKV_WIDTH
    row_spec = lambda width: pl.BlockSpec((tm, width), lambda i: (i, 0))
    full = lambda arr: pl.BlockSpec(arr.shape, lambda i: (0,) * arr.ndim)
    out_sds = jax.ShapeDtypeStruct((lp, rw), F32)
    return pl.pallas_call(
        functools.partial(_even_in_kernel, tm=tm),
        grid=(lp // tm,),
        in_specs=[row_spec(D_MODEL), full(g0), full(w_in), full(mu), full(w0), full(lora_w), full(a0),
                  full(g_up), full(k_k), full(k_a), full(seg), full(pool_w), full(pool_scale)],
        out_specs=[row_spec(rw)] * 8,
        out_shape=[out_sds] * 8,
        scratch_shapes=[pltpu.VMEM((SUBLANES, SHIFT_WIDTH), F32), pltpu.VMEM((POOL_CARRY, POOL_WIDTH), F32)],
        compiler_params=pltpu.CompilerParams(dimension_semantics=("arbitrary",),
                                             vmem_limit_bytes=VMEM_LIMIT),
        name="even_in",
    )(h, g0, w_in, mu, w0, lora_w, a0, g_up, k_k, k_a, seg, pool_w, pool_scale)


def _scan_kernel(r_ref, k_ref, v_ref, na_ref, b_ref, lw_ref, o_ref, h_scr):
    @pl.when(pl.program_id(0) == 0)
    def _():
        h_scr[...] = jnp.zeros(h_scr.shape, F32)

    c = CHUNK
    n_chunks = r_ref.shape[0] // c
    row = lax.broadcasted_iota(jnp.int32, (c, c), 0)
    col = lax.broadcasted_iota(jnp.int32, (c, c), 1)
    tri = jnp.where(col <= row, 1.0, 0.0).astype(BF16)
    lane = lax.broadcasted_iota(jnp.int32, (1, LANES), 1)
    mlo = lane < HALF
    prow = lax.broadcasted_iota(jnp.int32, (LANES, LANES), 0)
    pcol = lax.broadcasted_iota(jnp.int32, (LANES, LANES), 1)
    same_head = (prow < HALF) == (pcol < HALF)
    diag = prow == pcol
    trow = lax.broadcasted_iota(jnp.int32, (c, LANES), 0)
    tcol = lax.broadcasted_iota(jnp.int32, (c, LANES), 1)
    tcol = jnp.where(tcol >= HALF, tcol - HALF, tcol)
    strict2 = tcol < trow
    incl2 = jnp.concatenate([tcol <= trow] * 2, axis=0)
    same_blk = (tcol // SOLVE_BLOCK) == (trow // SOLVE_BLOCK)
    lane2 = lax.broadcasted_iota(jnp.int32, (1, 2 * LANES), 1)
    zeros_c = jnp.zeros((c, LANES), F32)
    zeros_2c = jnp.zeros((c, 2 * LANES), F32)
    n_pairs = RWKV_WIDTH // LANES
    pair_cols = [slice(p * LANES, (p + 1) * LANES) for p in range(n_pairs)]
    d = functools.partial(jnp.dot, preferred_element_type=F32)

    prep = []
    for ci in range(n_chunks):
        rows = slice(ci * c, (ci + 1) * c)
        lw = lw_ref[rows, :]
        lw_hi = lw.astype(BF16)
        lw_r = lw - lw_hi.astype(F32)
        lw_mid = lw_r.astype(BF16)
        lw_lo = (lw_r - lw_mid.astype(F32)).astype(BF16)
        cum = d(tri, lw_hi) + (d(tri, lw_mid) + d(tri, lw_lo))
        cum_end = cum[c - 1:c, :]
        e_neg = jnp.exp(-cum)
        e_end = jnp.exp(cum_end - cum)
        b_all = b_ref[rows, :]
        k_all = k_ref[rows, :]
        prep.append(dict(r_t=r_ref[rows, :] * jnp.exp(cum), a_t=na_ref[rows, :] * jnp.exp(cum - lw),
                         b_t=b_all * e_neg, k_t=k_all * e_neg, b_h=b_all * e_end, k_h=k_all * e_end,
                         v=v_ref[rows, :], p_end=jnp.exp(cum_end)))
    units = [(ci, p) for ci in range(n_chunks) for p in range(n_pairs)]

    a_all = {}
    for ci, p in units:
        cols = pair_cols[p]
        rt, at = prep[ci]["r_t"][:, cols], prep[ci]["a_t"][:, cols]
        lhs4 = jnp.concatenate([jnp.where(mlo, at, zeros_c), jnp.where(mlo, zeros_c, at),
                                jnp.where(mlo, rt, zeros_c), jnp.where(mlo, zeros_c, rt)], axis=0)
        a_all[ci, p] = _dot_nt(lhs4, jnp.concatenate([prep[ci]["b_t"][:, cols], prep[ci]["k_t"][:, cols]], axis=0))

    heads = []
    for ci, p in units:
        cols = pair_cols[p]
        at, vp = prep[ci]["a_t"][:, cols], prep[ci]["v"][:, cols]
        at_sw = pltpu.roll(at, HALF, axis=1)
        vp_sw = pltpu.roll(vp, HALF, axis=1)
        for hh in range(2):
            nk = jnp.where(strict2, a_all[ci, p][hh * c:(hh + 1) * c], 0.0)
            av = _dot(nk, jnp.concatenate([zeros_c, vp_sw if hh == 0 else vp], axis=0))
            x0 = jnp.where(mlo, at if hh == 0 else at_sw, av)
            nk_sw = pltpu.roll(nk, HALF, axis=1)
            n_split = jnp.where(mlo, jnp.where(same_blk, 0.0, nk), jnp.where(same_blk, nk_sw, 0.0))
            heads.append(jnp.concatenate([x0, n_split], axis=1))

    for _ in range(int(math.log2(SOLVE_BLOCK))):
        nxt = []
        for y in heads:
            prod = _dot(y[:, LANES:], jnp.concatenate([zeros_2c, y], axis=0))
            nxt.append(jnp.where(lane2 >= LANES + HALF, prod, y + prod))
        heads = nxt
    for _ in range(int(math.log2(c // SOLVE_BLOCK))):
        nxt = []
        for y in heads:
            prod = _dot(y[:, LANES:], jnp.concatenate([y, zeros_2c], axis=0))
            nxt.append(jnp.where(lane2 < LANES, y + prod, prod))
        heads = nxt

    big = {}
    for n, (ci, p) in enumerate(units):
        cols = pair_cols[p]
        x_lo, x_hi = heads[2 * n][:, :LANES], heads[2 * n + 1][:, :LANES]
        w_p = jnp.where(mlo, x_lo, pltpu.roll(x_hi, HALF, axis=1))
        u0_p = jnp.where(mlo, pltpu.roll(x_lo, HALF, axis=1), x_hi)
        rhs = jnp.concatenate([jnp.concatenate([w_p, u0_p], axis=1),
                               jnp.concatenate([zeros_c, prep[ci]["v"][:, cols]], axis=1)], axis=0)
        a_r = jnp.where(incl2, a_all[ci, p][2 * c:], 0.0)
        bk_t = jnp.concatenate([prep[ci]["b_h"][:, cols], prep[ci]["k_h"][:, cols]], axis=0).T
        big[ci, p] = _dot(jnp.concatenate([a_r, bk_t], axis=0), rhs)

    state = [h_scr[p] for p in range(n_pairs)]
    for ci in range(n_chunks):
        rows = slice(ci * c, (ci + 1) * c)
        for p, cols in enumerate(pair_cols):
            res = big[ci, p]
            q_hat = prep[ci]["r_t"][:, cols] + jnp.where(mlo, res[:c, :LANES], res[c:2 * c, :LANES])
            o_hat = jnp.where(mlo, res[:c, LANES:], res[c:2 * c, LANES:])
            g_mat = (jnp.where(same_head, res[2 * c:, :LANES], 0.0)
                     + jnp.where(diag, prep[ci]["p_end"][:, cols], 0.0))
            j_mat = jnp.where(same_head, res[2 * c:, LANES:], 0.0)
            st = _dot(jnp.concatenate([q_hat, g_mat], axis=0), state[p])
            o_ref[rows, cols] = st[:c] + o_hat
            state[p] = st[c:] + j_mat
    for p in range(n_pairs):
        h_scr[p] = state[p]


def _rwkv_scan(r, k2, v, na, b, logw):
    lp, rw = r.shape
    rows = CHUNK * SCAN_CHUNKS
    assert lp % rows == 0
    spec = pl.BlockSpec((rows, rw), lambda i: (i, 0))
    return pl.pallas_call(
        _scan_kernel,
        grid=(lp // rows,),
        in_specs=[spec] * 6,
        out_specs=spec,
        out_shape=jax.ShapeDtypeStruct((lp, rw), F32),
        scratch_shapes=[pltpu.VMEM((rw // LANES, LANES, LANES), F32)],
        compiler_params=pltpu.CompilerParams(dimension_semantics=("arbitrary",),
                                             vmem_limit_bytes=VMEM_LIMIT),
        name="rwkv_scan",
    )(r, k2, v, na, b, logw)


def _even_out_kernel(h_ref, o_ref, r_ref, k_ref, v_ref, g_ref, z_ref, lnw_ref, lnb_ref, rk_ref, seg_ref,
                     wout_ref, g1_ref, g2_ref, g3_ref, w1_ref, w2_ref, out_ref):
    inv_n = 1.0 / RWKV_HEAD
    o = o_ref[...]
    mean = _head_sum(o, seg_ref) * inv_n
    dev = o - mean
    var = _dot(dev * dev, seg_ref[...]) * inv_n
    on = dev * lax.rsqrt(var + GN_EPS) * lnw_ref[...] + lnb_ref[...]
    bonus = _dot(r_ref[...] * k_ref[...] * rk_ref[...], seg_ref[...]) * v_ref[...]
    om = (on + bonus) * g_ref[...]
    rw = RWKV_WIDTH
    m = _dot(om, wout_ref[:rw, :]) + _dot(z_ref[...], wout_ref[rw:, :])
    hm = h_ref[...] + _rms(m, g1_ref[...], RMS_EPS)
    out_ref[...] = _mlp_residual(hm, g2_ref[...], g3_ref[...], w1_ref, w2_ref)


def _weight_spec(arr):
    return pl.BlockSpec(arr.shape, lambda i: (0,) * arr.ndim, pipeline_mode=pl.Buffered(1))


def _even_out(h, o, r, k2, v, g, z, ln_w, ln_b, r_k, seg, w_out, g1, g2, g3, w1, w2):
    lp = h.shape[0]
    tm = _pick_tile(lp, ROW_TILES)
    row_spec = lambda width: pl.BlockSpec((tm, width), lambda i: (i, 0))
    full = lambda arr: pl.BlockSpec(arr.shape, lambda i: (0,) * arr.ndim)
    rw = RWKV_WIDTH
    return pl.pallas_call(
        _even_out_kernel,
        grid=(lp // tm,),
        in_specs=[row_spec(D_MODEL)] + [row_spec(rw)] * 6 +
                 [full(ln_w), full(ln_b), full(r_k), full(seg), _weight_spec(w_out), full(g1), full(g2),
                  full(g3), _weight_spec(w1), _weight_spec(w2)],
        out_specs=row_spec(D_MODEL),
        out_shape=jax.ShapeDtypeStruct((lp, D_MODEL), F32),
        compiler_params=pltpu.CompilerParams(dimension_semantics=("parallel",),
                                             vmem_limit_bytes=VMEM_LIMIT),
        name="even_out",
    )(h, o, r, k2, v, g, z, ln_w, ln_b, r_k, seg, w_out, g1, g2, g3, w1, w2)


def _odd_in_kernel(h_ref, g0_ref, w_ref, cos_ref, sin_lo_ref, sin_hi_ref, q_out, k_out, v_out, kmax_out):
    @pl.when(pl.program_id(0) == 0)
    def _():
        kmax_out[...] = jnp.zeros(kmax_out.shape, F32)

    hn = _rms(h_ref[...], g0_ref[...], RMS_EPS).astype(BF16)
    y = jnp.dot(hn, w_ref[...], preferred_element_type=F32)
    cos = cos_ref[...]
    sin_lo = sin_lo_ref[...]
    sin_hi = sin_hi_ref[...]
    half = DIFF_HEAD // 2

    def rope(t):
        return (t * cos + pltpu.roll(t, LANES - half, axis=1) * sin_lo + pltpu.roll(t, half, axis=1) * sin_hi)

    scale = DIFF_HEAD ** -0.5 * math.log2(math.e)
    tm = y.shape[0]
    lane = lax.broadcasted_iota(jnp.int32, (tm, LANES), 1)
    minus_one = jnp.where(lane == 0, -1.0, 0.0).astype(BF16)
    ones = jnp.ones((tm, LANES), BF16)
    for j in range(D_MODEL // LANES):
        cols = slice(j * LANES, (j + 1) * LANES)
        kcols = slice(D_MODEL + j * LANES, D_MODEL + (j + 1) * LANES)
        vcols = slice(2 * D_MODEL + j * LANES, 2 * D_MODEL + (j + 1) * LANES)
        wide = slice(2 * j * LANES, (2 * j + 1) * LANES)
        wide_hi = slice((2 * j + 1) * LANES, (2 * j + 2) * LANES)
        q_out[:, cols] = (rope(y[:, cols]) * scale).astype(BF16)
        kb = rope(y[:, kcols]).astype(BF16)
        k_out[:, wide] = kb
        k_out[:, wide_hi] = minus_one
        v_out[:, wide] = y[:, vcols].astype(BF16)
        v_out[:, wide_hi] = ones
        kf = kb.astype(F32)
        knorm = jnp.sqrt(jnp.max(jnp.sum(kf * kf, axis=-1, keepdims=True), axis=0, keepdims=True))
        kmax_out[:, cols] = jnp.maximum(kmax_out[:, cols], jnp.broadcast_to(knorm, (SUBLANES, LANES)))


def _odd_in(h, g0, w, cos, sin_lo, sin_hi):
    lp, lpk = h.shape[0], cos.shape[0]
    tm = ROW_ALIGN
    last = lp // tm - 1
    row_spec = lambda width: pl.BlockSpec((tm, width), lambda i: (i, 0))
    full = lambda arr: pl.BlockSpec(arr.shape, lambda i: (0,) * arr.ndim)
    wide = jax.ShapeDtypeStruct((lpk, 2 * D_MODEL), BF16)
    return pl.pallas_call(
        _odd_in_kernel,
        grid=(lpk // tm,),
        in_specs=[pl.BlockSpec((tm, D_MODEL), lambda i: (jnp.minimum(i, last), 0)), full(g0),
                  _weight_spec(w), row_spec(LANES), row_spec(LANES), row_spec(LANES)],
        out_specs=[row_spec(D_MODEL), row_spec(2 * D_MODEL), row_spec(2 * D_MODEL),
                   pl.BlockSpec((SUBLANES, D_MODEL), lambda i: (0, 0))],
        out_shape=[jax.ShapeDtypeStruct((lpk, D_MODEL), BF16), wide, wide,
                   jax.ShapeDtypeStruct((SUBLANES, D_MODEL), F32)],
        compiler_params=pltpu.CompilerParams(dimension_semantics=("arbitrary",),
                                             vmem_limit_bytes=VMEM_LIMIT),
        name="odd_in",
    )(h, g0, w, cos, sin_lo, sin_hi)


def _attn_kernel(lam_ref, sw_ref, kmax_ref, q_ref, k_ref, v_ref, o_ref, q2_scr, p_scr, acc_scr,
                 *, lam_init, tq, tk, nh, nq):
    i0 = pl.program_id(1) * nq
    lane = lax.broadcasted_iota(jnp.int32, (1, LANES), 1)
    mlo = lane < HALF
    nt = (((1,), (1,)), ((), ()))
    sub = tk // tq
    n_blocks = (i0 * tq) // tk + 1
    wl = 2 * LANES
    units = [(qb, hd) for qb in range(nq) for hd in range(nh)]

    def diagonal_mask(qb, off, width):
        qrow = lax.broadcasted_iota(jnp.int32, (2 * tq, width), 0)
        qrow = jnp.where(qrow >= tq, qrow - tq, qrow)
        kcol = lax.broadcasted_iota(jnp.int32, (2 * tq, width), 1)
        return kcol - qrow <= (i0 + qb) * tq - off

    for u, (qb, hd) in enumerate(units):
        q = q_ref[qb * tq:(qb + 1) * tq, hd * LANES:(hd + 1) * LANES]
        zq = jnp.zeros_like(q)
        q2 = jnp.concatenate([jnp.where(mlo, q, zq), jnp.where(mlo, zq, q)], axis=0)
        q2f = q2.astype(F32)
        bound = (jnp.sqrt(jnp.sum(q2f * q2f, axis=-1, keepdims=True))
                 * kmax_ref[0:1, hd * LANES:hd * LANES + 1] * BOUND_SLACK)
        q2_scr[u] = jnp.concatenate([q2, jnp.where(lane == 0, bound, 0.0).astype(BF16)], axis=1)

    def weights_into(u, slot, t, width):
        qb, hd = units[u]
        off = pl.multiple_of(t * tk, tk)
        w = tk if width is None else width
        s = lax.dot_general(q2_scr[u], k_ref[pl.ds(off, w), hd * wl:(hd + 1) * wl], nt,
                            preferred_element_type=F32)
        if width is not None:
            s = jnp.where(diagonal_mask(qb, off, w), s, NEG_BIG)
        p_scr[u, slot, :, :w] = jnp.exp2(s).astype(BF16)

    def add_weighted_values(u, slot, t, width):
        hd = units[u][1]
        off = pl.multiple_of(t * tk, tk)
        w = tk if width is None else width
        acc_scr[u] += jnp.dot(p_scr[u, slot, :, :w], v_ref[pl.ds(off, w), hd * wl:(hd + 1) * wl],
                              preferred_element_type=F32)

    def tick(t, slot, widths):
        for u, (qb, _) in enumerate(units):
            weights_into(u, slot, t, None if widths is None else widths[qb])
            add_weighted_values(u, 1 - slot, t - 1, None)

    def drain(slot, t, widths):
        for u, (qb, _) in enumerate(units):
            add_weighted_values(u, slot, t, widths[qb])

    acc_scr[...] = jnp.zeros(acc_scr.shape, F32)

    @pl.when(n_blocks > 1)
    def _():
        for u in range(len(units)):
            weights_into(u, 0, 0, None)

    def pair(u, carry):
        tick(2 * u + 1, 1, None)
        tick(2 * u + 2, 0, None)
        return carry

    n_pairs = jnp.maximum(n_blocks - 2, 0) // 2
    lax.fori_loop(0, n_pairs, pair, 0)
    t1 = 2 * n_pairs + 1

    for r0 in range(0, sub, nq):
        widths = [(r0 + qb + 1) * tq for qb in range(nq)]
        here = i0 % sub == r0

        @pl.when(jnp.logical_and(n_blocks == 1, here))
        def _():
            for u, (qb, _) in enumerate(units):
                weights_into(u, 0, 0, widths[qb])
            drain(0, 0, widths)

        @pl.when(jnp.logical_and(jnp.logical_and(n_blocks > 1, n_blocks % 2 == 0), here))
        def _():
            tick(t1, 1, widths)
            drain(1, t1, widths)

        @pl.when(jnp.logical_and(jnp.logical_and(n_blocks > 1, n_blocks % 2 == 1), here))
        def _():
            tick(t1, 1, None)
            tick(t1 + 1, 0, widths)
            drain(0, t1 + 1, widths)

    lv = lam_ref[...]
    lam = (jnp.exp(jnp.sum(lv[0:1] * lv[1:2], axis=-1, keepdims=True))
           - jnp.exp(jnp.sum(lv[2:3] * lv[3:4], axis=-1, keepdims=True)) + lam_init)

    def write_out(u, acc):
        qb, hd = units[u]
        o = acc[:, :LANES] / acc[:, LANES:]
        o = o[:tq] - lam * o[tq:]
        o = _rms(o, sw_ref[...], SUBLN_EPS) * (1.0 - lam_init)
        o_ref[qb * tq:(qb + 1) * tq, hd * LANES:(hd + 1) * LANES] = o.astype(BF16)

    for u in range(len(units)):
        write_out(u, acc_scr[u])

    smallest = jnp.min(acc_scr[:, :, LANES:LANES + 1])
    @pl.when(jnp.logical_not(smallest >= MIN_ROW_SUM))
    def _():
        for u, (qb, hd) in enumerate(units):
            q2 = q2_scr[u, :, :LANES]

            def block(t, carry, diagonal):
                m, acc = carry
                off = pl.multiple_of(t * tk, tk)
                s = lax.dot_general(q2, k_ref[pl.ds(off, tk), hd * wl:hd * wl + LANES], nt,
                                    preferred_element_type=F32)
                if diagonal:
                    s = jnp.where(diagonal_mask(qb, off, tk), s, NEG_BIG)
                m_new = jnp.maximum(m, jnp.max(s, axis=-1, keepdims=True))
                p = jnp.exp2(s - m_new).astype(BF16)
                pv = jnp.dot(p, v_ref[pl.ds(off, tk), hd * wl:(hd + 1) * wl], preferred_element_type=F32)
                return m_new, jnp.exp2(m - m_new) * acc + pv

            init = (jnp.full((2 * tq, 1), NEG_BIG, F32), jnp.zeros((2 * tq, wl), F32))
            carry = lax.fori_loop(0, n_blocks - 1, lambda t, c: block(t, c, False), init)
            write_out(u, block(n_blocks - 1, carry, True)[1])


def _diff_attn(q, k, v, kmax, lam_vecs, subln_w, lam_init, lp):
    tq, tk, nh, nq = ATT_Q_BLOCK, ATT_K_BLOCK, ATT_HEADS, ATT_Q_PER_STEP
    lpk = k.shape[0]
    assert tk % (nq * tq) == 0 and lpk % tk == 0 and lpk >= lp and DIFF_HEADS % nh == 0
    blk = pl.BlockSpec((nq * tq, nh * LANES), lambda h, i: (i, h))
    resident = pl.BlockSpec((lpk, 2 * nh * LANES), lambda h, i: (0, h), pipeline_mode=pl.Buffered(1))
    full = lambda arr: pl.BlockSpec(arr.shape, lambda h, i: (0,) * arr.ndim)
    n_units = nq * nh
    return pl.pallas_call(
        functools.partial(_attn_kernel, lam_init=lam_init, tq=tq, tk=tk, nh=nh, nq=nq),
        grid=(DIFF_HEADS // nh, -(-lp // (nq * tq))),
        in_specs=[full(lam_vecs), full(subln_w), pl.BlockSpec((SUBLANES, nh * LANES), lambda h, i: (0, h)),
                  blk, resident, resident],
        out_specs=blk,
        out_shape=jax.ShapeDtypeStruct((lpk, D_MODEL), BF16),
        scratch_shapes=[pltpu.VMEM((n_units, 2 * tq, 2 * LANES), BF16),
                        pltpu.VMEM((n_units, 2, 2 * tq, tk), BF16),
                        pltpu.VMEM((n_units, 2 * tq, 2 * LANES), F32)],
        compiler_params=pltpu.CompilerParams(dimension_semantics=("parallel", "arbitrary"),
                                             vmem_limit_bytes=VMEM_LIMIT),
        name="diff_attn",
    )(lam_vecs, subln_w, kmax, q, k, v)


def _odd_out_kernel(h_ref, o_ref, wout_ref, g1_ref, g2_ref, g3_ref, w1_ref, w2_ref, out_ref):
    m = jnp.dot(o_ref[...], wout_ref[...], preferred_element_type=F32)
    hm = h_ref[...] + _rms(m, g1_ref[...], RMS_EPS)
    out_ref[...] = _mlp_residual(hm, g2_ref[...], g3_ref[...], w1_ref, w2_ref)


def _odd_out(h, o, w_out, g1, g2, g3, w1, w2):
    lp = h.shape[0]
    tm = _pick_tile(lp, ROW_TILES)
    row_spec = lambda width: pl.BlockSpec((tm, width), lambda i: (i, 0))
    full = lambda arr: pl.BlockSpec(arr.shape, lambda i: (0,) * arr.ndim)
    return pl.pallas_call(
        _odd_out_kernel,
        grid=(lp // tm,),
        in_specs=[row_spec(D_MODEL), row_spec(D_MODEL), _weight_spec(w_out), full(g1), full(g2), full(g3),
                  _weight_spec(w1), _weight_spec(w2)],
        out_specs=row_spec(D_MODEL),
        out_shape=jax.ShapeDtypeStruct((lp, D_MODEL), F32),
        compiler_params=pltpu.CompilerParams(dimension_semantics=("parallel",),
                                             vmem_limit_bytes=VMEM_LIMIT),
        name="odd_out",
    )(h, o, w_out, g1, g2, g3, w1, w2)


def _forward(x, meta, norm_g, mlp_w1, mlp_w2, ev, od):
    seq = x.shape[0]
    length = N_META + seq
    lp = -(-length // ROW_ALIGN) * ROW_ALIGN
    h = jnp.concatenate([meta.astype(x.dtype), x, jnp.zeros((lp - length, D_MODEL), x.dtype)], axis=0)

    lpk = -(-lp // ATT_K_BLOCK) * ATT_K_BLOCK
    inv = ROPE_THETA ** (-np.arange(0, DIFF_HEAD, 2, dtype=np.float64) / DIFF_HEAD)
    ang = np.arange(lpk, dtype=np.float64)[:, None] * inv[None, :]
    ang = np.concatenate([ang, ang, ang, ang], axis=-1)
    first_half = (np.arange(LANES) % DIFF_HEAD) < DIFF_HEAD // 2
    cos = jnp.asarray(np.cos(ang), F32)
    sin_lo = jnp.asarray(np.where(first_half, -np.sin(ang), 0.0), F32)
    sin_hi = jnp.asarray(np.where(first_half, 0.0, np.sin(ang)), F32)

    head_id = jnp.arange(RWKV_WIDTH) // RWKV_HEAD
    seg = (head_id[:, None] == head_id[None, :]).astype(BF16)
    row2 = lambda t: t.reshape(1, -1)

    depth = norm_g.shape[0]
    for i in range(depth):
        g = norm_g[i]
        g0, g1, g2, g3 = (row2(g[n]) for n in range(4))
        w1 = mlp_w1[i].astype(BF16)
        w2 = mlp_w2[i].astype(BF16)
        j = i // 2
        if i % 2 == 0:
            (w_in, mu, w0, w_up, a0, a_up, g_up, k_k, k_a, r_k, ln_w, ln_b, pool_w, pool_scale,
             w_out) = (t[j] for t in ev)
            zeros = jnp.zeros((DECAY_RANK, RWKV_WIDTH), F32)
            lora_w = jnp.concatenate([jnp.concatenate([w_up, zeros], axis=1),
                                      jnp.concatenate([zeros, a_up], axis=1)], axis=0).astype(BF16)
            r, k2, v, na, b, logw, gate, z = _even_in(
                h, g0, w_in.astype(BF16), row2(mu), row2(w0), lora_w, row2(a0), g_up.astype(BF16),
                row2(k_k), row2(k_a), seg, pool_w.astype(BF16), row2(pool_scale))
            o = _rwkv_scan(r, k2, v, na, b, logw)
            h = _even_out(h, o, r, k2, v, gate, z, row2(ln_w), row2(ln_b), row2(r_k), seg,
                          w_out.astype(BF16), g1, g2, g3, w1, w2)
        else:
            w_in, lam_vecs, subln_w, w_out = (t[j] for t in od)
            q, k, v, kmax = _odd_in(h, g0, w_in.astype(BF16), cos, sin_lo, sin_hi)
            lam_init = 0.8 - 0.6 * math.exp(-0.3 * i)
            o = _diff_attn(q, k, v, kmax, lam_vecs, row2(subln_w), lam_init, lp)
            h = _odd_out(h, o, w_out.astype(BF16), g1, g2, g3, w1, w2)
    return h[N_META:length]


def kernel(x, meta, norm_g, mlp_w1, mlp_w2, ev_w_in, ev_mu, ev_w0, ev_w_up, ev_a0, ev_a_up, ev_g_up, ev_k_k,
           ev_k_a, ev_r_k, ev_ln_w, ev_ln_b, ev_pool_w, ev_pool_scale, ev_w_out, od_w_in, od_lambda,
           od_subln_w, od_w_out):
    ev = (ev_w_in, ev_mu, ev_w0, ev_w_up, ev_a0, ev_a_up, ev_g_up, ev_k_k, ev_k_a, ev_r_k, ev_ln_w, ev_ln_b,
          ev_pool_w, ev_pool_scale, ev_w_out)
    od = (od_w_in, od_lambda, od_subln_w, od_w_out)
    outs = [_forward(x[bi], meta, norm_g, mlp_w1, mlp_w2, ev, od) for bi in range(x.shape[0])]
    return jnp.stack(outs, axis=0)
```
